```python
import jax, jax.numpy as jnp
from jax import lax
import numpy as np

D_MODEL = 1024
BATCH = 8
SEQ = 2048
DEPTH = 4
DEC_BATCH = 32
DEC_SEQ = 8
PAST_LEN = 16384
PAGE_SIZE = 128

N_A_LAYERS = DEPTH // 2
N_B_LAYERS = DEPTH - N_A_LAYERS
MIX_W = D_MODEL
MEM_HEADS = 4
MEM_HEAD_DIM = 64
MEM_W = MEM_HEADS * MEM_HEAD_DIM
N_MEM = 256
POOL_W = MIX_W - MEM_W
POOL_WINDOWS = (2, 4, 8, 16)
POOL_GROUPS = len(POOL_WINDOWS)
POOL_GROUP_W = POOL_W // POOL_GROUPS
POOL_STATE = max(POOL_WINDOWS) - 1
MLA_HEADS = 12
QK_NOPE = 64
QK_ROPE = 32
QK_DIM = QK_NOPE + QK_ROPE
V_HEAD = 64
KV_LORA = 256
Q_LORA = 384
KV_ROW = KV_LORA + QK_ROPE
ROPE_BASE = 10000.0
D_FF = -(-(8 * D_MODEL) // (3 * 256)) * 256
Q_BLOCK = 128
EPS = 1e-6
NEG = -1e30

kernel_name = 'yoco_pool_mla_memory_decoder_step'


def rmsnorm(x, g):
    xf = x.astype(jnp.float32)
    y = xf * lax.rsqrt(jnp.mean(xf * xf, axis=-1, keepdims=True) + EPS)
    return (y * g.astype(jnp.float32)).astype(x.dtype)


def rope(x, pos):
    half = x.shape[-1] // 2
    inv_freq = ROPE_BASE ** (-jnp.arange(half, dtype=jnp.float32) / half)
    ang = pos.astype(jnp.float32)[:, None] * inv_freq[None, :]
    shape = (1, pos.shape[0]) + (1,) * (x.ndim - 3) + (half,)
    cos = jnp.cos(ang).reshape(shape)
    sin = jnp.sin(ang).reshape(shape)
    xf = x.astype(jnp.float32)
    x1, x2 = xf[..., :half], xf[..., half:]
    return jnp.concatenate([x1 * cos - x2 * sin, x1 * sin + x2 * cos], axis=-1).astype(x.dtype)


def pool_mixer(u, prev, pos, w_grp, scale):
    b, s, _ = u.shape
    p_len = prev.shape[1]
    ext = jnp.concatenate([prev.astype(u.dtype), u], axis=1)
    csum = jnp.pad(jnp.cumsum(ext.astype(jnp.float32), axis=1), ((0, 0), (1, 0), (0, 0)))
    outs = []
    for g, w in enumerate(POOL_WINDOWS):
        sl = slice(g * POOL_GROUP_W, (g + 1) * POOL_GROUP_W)
        win_sum = csum[:, p_len + 1:p_len + s + 1, sl] - csum[:, p_len + 1 - w:p_len + s + 1 - w, sl]
        cnt = jnp.minimum(w, pos + 1).astype(jnp.float32)
        outs.append(win_sum / cnt[None, :, None])
    pooled = jnp.stack(outs, axis=2)
    diff = (pooled - u.astype(jnp.float32).reshape(b, s, POOL_GROUPS, POOL_GROUP_W)).astype(u.dtype)
    mixed = jnp.einsum('bsgc,gce->bsge', diff, w_grp).reshape(b, s, POOL_W)
    return mixed * scale, ext[:, -p_len:]


def memory_kv(mem, norm_mem, w_mem_kv, g_mem_k):
    b, m, _ = mem.shape
    mf = mem.astype(jnp.float32)
    hm = mf * lax.rsqrt(jnp.mean(mf * mf, axis=-1, keepdims=True) + EPS)
    hm = (hm[None] * norm_mem.astype(jnp.float32)[:, None, None, :]).astype(mem.dtype)
    kv = jnp.einsum('lbmd,lde->lbme', hm, w_mem_kv)
    k = kv[..., :MEM_W].reshape(DEPTH, b, m, MEM_HEADS, MEM_HEAD_DIM)
    k = rmsnorm(k, g_mem_k[:, None, None, None, :])
    v = kv[..., MEM_W:].reshape(DEPTH, b, m, MEM_HEADS, MEM_HEAD_DIM)
    return k, v


def mem_attend(qm, mk, mv, g_q):
    b, s, _ = qm.shape
    q = rmsnorm(qm.reshape(b, s, MEM_HEADS, MEM_HEAD_DIM), g_q)
    sc = jnp.einsum('bshd,bmhd->bhsm', q, mk).astype(jnp.float32) * MEM_HEAD_DIM ** -0.5
    p = jax.nn.softmax(sc, axis=-1).astype(mv.dtype)
    return jnp.einsum('bhsm,bmhd->bshd', p, mv).reshape(b, s, MEM_W)


def shared_kv_rows(x, pos, norm_kv, w_dkv, g_kv_lora):
    h = rmsnorm(x, norm_kv)
    ck = h @ w_dkv
    c = rmsnorm(ck[..., :KV_LORA], g_kv_lora)
    kr = rope(ck[..., KV_LORA:], pos)
    return jnp.concatenate([c, kr], axis=-1)


def key_inv_rms(kv, w_uk):
    def one(rows):
        kn = (rows[:, :KV_LORA] @ w_uk).astype(jnp.float32).reshape(rows.shape[0], MLA_HEADS, QK_NOPE)
        kr = rows[:, KV_LORA:].astype(jnp.float32)
        ss = jnp.sum(kn * kn, axis=-1) + jnp.sum(kr * kr, axis=-1, keepdims=True)
        return lax.rsqrt(ss / QK_DIM + EPS)
    return lax.map(one, kv)


def mla_queries(cq_raw, pos, g_q_lora, w_uq, g_q, g_k_nope, g_k_rope, w_uk):
    b, s, _ = cq_raw.shape
    cq = rmsnorm(cq_raw, g_q_lora)
    q = rmsnorm((cq @ w_uq).reshape(b, s, MLA_HEADS, QK_DIM), g_q)
    q_nope = q[..., :QK_NOPE] * g_k_nope
    q_rope = rope(q[..., QK_NOPE:], pos) * jnp.concatenate([g_k_rope, g_k_rope])
    q_lat = jnp.einsum('bshd,chd->bshc', q_nope, w_uk.reshape(KV_LORA, MLA_HEADS, QK_NOPE))
    return q_lat, q_rope


def mla_attend(q_lat, q_rope, q_pos, kv, k_inv, k_pos):
    c = kv[..., :KV_LORA]
    kr = kv[..., KV_LORA:]
    sc = jnp.einsum('bshc,bkc->bhsk', q_lat, c) + jnp.einsum('bshr,bkr->bhsk', q_rope, kr)
    sc = sc.astype(jnp.float32) * jnp.swapaxes(k_inv, 1, 2)[:, :, None, :] * QK_DIM ** -0.5
    sc = jnp.where((k_pos[None, :] <= q_pos[:, None])[None, None], sc, NEG)
    p = jax.nn.softmax(sc, axis=-1).astype(c.dtype)
    return jnp.einsum('bhsk,bkc->bshc', p, c)


def mla_blocked(q_lat, q_rope, q_pos, kv, k_inv, k_pos):
    b, s = q_lat.shape[0], q_lat.shape[1]
    if s > Q_BLOCK and s % Q_BLOCK == 0:
        nb = s // Q_BLOCK
        def split(a):
            return jnp.moveaxis(a.reshape((b, nb, Q_BLOCK) + a.shape[2:]), 1, 0)
        out = lax.map(lambda blk: mla_attend(blk[0], blk[1], blk[2], kv, k_inv, k_pos),
                      (split(q_lat), split(q_rope), q_pos.reshape(nb, Q_BLOCK)))
        return jnp.moveaxis(out, 0, 1).reshape((b, s) + out.shape[3:])
    return mla_attend(q_lat, q_rope, q_pos, kv, k_inv, k_pos)


def trunk(x, pos, pool_prev, mem_k, mem_v, kv_past, p):
    b, s, _ = x.shape
    pool_new = []
    kv_new = None
    for l in range(DEPTH):
        h = rmsnorm(x, p['norm_mix'][l])
        if l < N_A_LAYERS:
            proj = h @ p['w_in_a'][l]
            tok, st = pool_mixer(proj[..., :POOL_W], pool_prev[l], pos, p['w_pool_grp'][l], p['pool_scale'][l])
            pool_new.append(st)
            qm = proj[..., POOL_W:]
        else:
            j = l - N_A_LAYERS
            if j == 0:
                kv_new = shared_kv_rows(x, pos, p['norm_kv'], p['w_dkv'], p['g_kv_lora'])
                kv_all = kv_new if kv_past is None else jnp.concatenate([kv_past, kv_new.astype(kv_past.dtype)], axis=1)
                k_inv = key_inv_rms(kv_all, p['w_uk'])
                k_pos = jnp.arange(kv_all.shape[1])
            proj = h @ p['w_in_b'][j]
            q_lat, q_rope = mla_queries(proj[..., :Q_LORA], pos, p['g_q_lora'][j], p['w_uq'][j], p['g_q'][j],
                                        p['g_k_nope'], p['g_k_rope'], p['w_uk'])
            o_lat = mla_blocked(q_lat, q_rope, pos, kv_all, k_inv, k_pos)
            tok = jnp.einsum('bshc,chd->bshd', o_lat,
                             p['w_uv'].reshape(KV_LORA, MLA_HEADS, V_HEAD)).reshape(b, s, MLA_HEADS * V_HEAD)
            qm = proj[..., Q_LORA:]
        mo = mem_attend(qm, mem_k[l], mem_v[l], p['g_mem_q'][l])
        x = x + jnp.concatenate([tok, mo], axis=-1) @ p['w_out'][l]
        gu = rmsnorm(x, p['norm_ffn'][l]) @ p['w_ffn_in'][l]
        x = x + (jax.nn.silu(gu[..., :D_FF]) * gu[..., D_FF:]) @ p['w_ffn_out'][l]
    return x, kv_new, jnp.stack(pool_new)


def setup_inputs(seed: int = 0) -> dict:
    key = jax.random.key(seed)
    keys = jax.random.split(key, 40)
    ctr = [0]
    def nk():
        ctr[0] += 1
        return keys[ctr[0] - 1]
    def nrm(shape, scale=1.0):
        return jax.random.normal(nk(), shape, jnp.float32) * scale
    def gain(shape):
        return 1.0 + 0.1 * jax.random.normal(nk(), shape, jnp.float32)
    n_pages = PAST_LEN // PAGE_SIZE
    n_used = DEC_BATCH * n_pages
    n_pool = n_used + (n_used + 3) // 4
    d = {}
    d['x_prompt'] = nrm((BATCH, SEQ, D_MODEL))
    d['x_sample'] = nrm((DEC_BATCH, DEC_SEQ, D_MODEL))
    d['cache_kv'] = nrm((n_pool, PAGE_SIZE, KV_ROW))
    d['state_pool'] = nrm((N_A_LAYERS, DEC_BATCH, POOL_STATE, POOL_W))
    d['cache_mem_k'] = nrm((DEPTH, DEC_BATCH, N_MEM, MEM_HEADS, MEM_HEAD_DIM))
    d['cache_mem_v'] = nrm((DEPTH, DEC_BATCH, N_MEM, MEM_HEADS, MEM_HEAD_DIM))
    perm = jax.random.permutation(nk(), n_pool).astype(jnp.int32)
    d['page_table'] = perm[:n_used].reshape(DEC_BATCH, n_pages)
    d['mem_prompt'] = nrm((BATCH, N_MEM, D_MODEL))
    d['norm_mix'] = gain((DEPTH, D_MODEL))
    d['norm_ffn'] = gain((DEPTH, D_MODEL))
    d['w_out'] = nrm((DEPTH, MIX_W, D_MODEL), MIX_W ** -0.5)
    d['w_ffn_in'] = nrm((DEPTH, D_MODEL, 2 * D_FF), D_MODEL ** -0.5)
    d['w_ffn_out'] = nrm((DEPTH, D_FF, D_MODEL), D_FF ** -0.5)
    d['norm_mem'] = gain((DEPTH, D_MODEL))
    d['w_mem_kv'] = nrm((DEPTH, D_MODEL, 2 * MEM_W), D_MODEL ** -0.5)
    d['g_mem_q'] = gain((DEPTH, MEM_HEAD_DIM))
    d['g_mem_k'] = gain((DEPTH, MEM_HEAD_DIM))
    d['w_in_a'] = nrm((N_A_LAYERS, D_MODEL, POOL_W + MEM_W), D_MODEL ** -0.5)
    d['w_pool_grp'] = nrm((N_A_LAYERS, POOL_GROUPS, POOL_GROUP_W, POOL_GROUP_W), POOL_GROUP_W ** -0.5)
    d['pool_scale'] = gain((N_A_LAYERS, POOL_W))
    d['w_in_b'] = nrm((N_B_LAYERS, D_MODEL, Q_LORA + MEM_W), D_MODEL ** -0.5)
    d['g_q_lora'] = gain((N_B_LAYERS, Q_LORA))
    d['w_uq'] = nrm((N_B_LAYERS, Q_LORA, MLA_HEADS * QK_DIM), Q_LORA ** -0.5)
    d['g_q'] = gain((N_B_LAYERS, QK_DIM))
    d['norm_kv'] = gain((D_MODEL,))
    d['w_dkv'] = nrm((D_MODEL, KV_ROW), D_MODEL ** -0.5)
    d['g_kv_lora'] = gain((KV_LORA,))
    d['w_uk'] = nrm((KV_LORA, MLA_HEADS * QK_NOPE), KV_LORA ** -0.5)
    d['w_uv'] = nrm((KV_LORA, MLA_HEADS * V_HEAD), KV_LORA ** -0.5)
    d['g_k_nope'] = gain((QK_NOPE,))
    d['g_k_rope'] = gain((QK_ROPE // 2,))
    return d


def reference(x_prompt, x_sample, cache_kv, state_pool, cache_mem_k, cache_mem_v, page_table, mem_prompt,
              norm_mix, norm_ffn, w_out, w_ffn_in, w_ffn_out, norm_mem, w_mem_kv, g_mem_q, g_mem_k,
              w_in_a, w_pool_grp, pool_scale, w_in_b, g_q_lora, w_uq, g_q,
              norm_kv, w_dkv, g_kv_lora, w_uk, w_uv, g_k_nope, g_k_rope):
    p = dict(norm_mix=norm_mix, norm_ffn=norm_ffn, w_out=w_out, w_ffn_in=w_ffn_in, w_ffn_out=w_ffn_out,
             g_mem_q=g_mem_q, w_in_a=w_in_a, w_pool_grp=w_pool_grp, pool_scale=pool_scale,
             w_in_b=w_in_b, g_q_lora=g_q_lora, w_uq=w_uq, g_q=g_q, norm_kv=norm_kv, w_dkv=w_dkv,
             g_kv_lora=g_kv_lora, w_uk=w_uk, w_uv=w_uv, g_k_nope=g_k_nope, g_k_rope=g_k_rope)
    b_p, s_p, _ = x_prompt.shape
    pos_p = jnp.arange(s_p)
    mem_k_prompt, mem_v_prompt = memory_kv(mem_prompt, norm_mem, w_mem_kv, g_mem_k)
    pool0 = jnp.zeros((N_A_LAYERS, b_p, POOL_STATE, POOL_W), x_prompt.dtype)
    y_prompt, kv_prompt, pool_prompt = trunk(x_prompt, pos_p, pool0, mem_k_prompt, mem_v_prompt, None, p)
    n_seq, n_pages = page_table.shape
    past_len = n_pages * cache_kv.shape[1]
    kv_past = cache_kv[page_table].reshape(n_seq, past_len, KV_ROW)
    pos_s = past_len + jnp.arange(x_sample.shape[1])
    y_sample, kv_sample, pool_sample = trunk(x_sample, pos_s, state_pool, cache_mem_k, cache_mem_v, kv_past, p)
    return (y_prompt, y_sample, kv_prompt, kv_sample, pool_prompt, pool_sample, mem_k_prompt, mem_v_prompt)
```

```python
import functools

import numpy as np
import jax
import jax.numpy as jnp
from jax import lax
from jax.experimental import pallas as pl
from jax.experimental.pallas import tpu as pltpu

D_MODEL = 1024
DEPTH = 4
N_A_LAYERS = 2
MEM_HEADS = 4
MEM_HEAD_DIM = 64
MEM_W = MEM_HEADS * MEM_HEAD_DIM
N_MEM = 256
POOL_W = D_MODEL - MEM_W
POOL_WINDOWS = (2, 4, 8, 16)
POOL_GROUP_W = POOL_W // len(POOL_WINDOWS)
POOL_STATE = max(POOL_WINDOWS) - 1
POOL_PAD = POOL_STATE + 1
MLA_HEADS = 12
QK_NOPE = 64
QK_ROPE = 32
QK_DIM = QK_NOPE + QK_ROPE
V_HEAD = 64
KV_LORA = 256
Q_LORA = 384
KV_ROW = KV_LORA + QK_ROPE
ROPE_BASE = 10000.0
D_FF = 2816
EPS = 1e-6
NEG = -1e30

LANE = 128
SLOT = LANE
Q_SLOTS = MLA_HEADS * SLOT
VMEM_LIMIT = 56 * 1024 * 1024

BF16 = jnp.bfloat16
F32 = jnp.float32
NT_DIMS = (((1,), (1,)), ((), ()))


def _params(n_grid):
    return pltpu.CompilerParams(dimension_semantics=("arbitrary",) * n_grid,
                                vmem_limit_bytes=VMEM_LIMIT)


def _dot(a, b):
    return jnp.dot(a, b, preferred_element_type=F32)


def _dot_nt(a, b):
    return lax.dot_general(a, b, NT_DIMS, preferred_element_type=F32)


def _split_dot(x, w):
    hi = x.astype(BF16)
    lo = (x - hi.astype(F32)).astype(BF16)
    return _dot(hi, w) + _dot(lo, w)


def _rms(x, g):
    return x * lax.rsqrt(jnp.mean(x * x, axis=-1, keepdims=True) + EPS) * g


def _mem_attend(qm, mk_ref, mv_ref, gq, gsum, nb, ts):
    ssq = _split_dot(qm * qm, gsum)
    qn = qm * lax.rsqrt(ssq * (1.0 / MEM_HEAD_DIM) + EPS) * (gq * MEM_HEAD_DIM ** -0.5)
    q3 = qn.reshape(nb, ts, MEM_W)
    mk = mk_ref[...].astype(BF16)
    mv = mv_ref[...].astype(BF16)
    lane = lax.broadcasted_iota(jnp.int32, (1, 1, MEM_W), 2)
    out = jnp.zeros((nb, ts, MEM_W), F32)
    for h in range(MEM_HEADS):
        in_head = (lane >= h * MEM_HEAD_DIM) & (lane < (h + 1) * MEM_HEAD_DIM)
        qh = jnp.where(in_head, q3, 0.0).astype(BF16)
        s = jnp.einsum('bqd,bmd->bqm', qh, mk, preferred_element_type=F32)
        s = s - jnp.max(s, axis=-1, keepdims=True)
        p = jnp.exp(s)
        p = p / jnp.sum(p, axis=-1, keepdims=True)
        o = jnp.einsum('bqm,bmd->bqd', p.astype(BF16), mv, preferred_element_type=F32)
        out = out + jnp.where(in_head, o, 0.0)
    return out.reshape(nb * ts, MEM_W)


def _memkv_kernel(mem_ref, gnorm_ref, w_ref, gk_ref, gsum_ref, k_ref, v_ref):
    x = mem_ref[...]
    xn = x * lax.rsqrt(jnp.mean(x * x, axis=-1, keepdims=True) + EPS)
    hm = (xn * gnorm_ref[0]).astype(BF16)
    kv = _dot(hm, w_ref[0])
    k = kv[:, :MEM_W]
    ssq = _split_dot(k * k, gsum_ref[...])
    k_ref[0] = k * lax.rsqrt(ssq * (1.0 / MEM_HEAD_DIM) + EPS) * gk_ref[0]
    v_ref[0] = kv[:, MEM_W:]


def _memkv(mem, norm_mem, w_mem_kv, g_mem_k, gsum):
    n = mem.shape[0]
    tm = 512
    return pl.pallas_call(
        _memkv_kernel,
        grid=(DEPTH, n // tm),
        in_specs=[
            pl.BlockSpec((tm, D_MODEL), lambda l, i: (i, 0)),
            pl.BlockSpec((1, 1, D_MODEL), lambda l, i: (l, 0, 0)),
            pl.BlockSpec((1, D_MODEL, 2 * MEM_W), lambda l, i: (l, 0, 0)),
            pl.BlockSpec((1, 1, MEM_W), lambda l, i: (l, 0, 0)),
            pl.BlockSpec((MEM_W, MEM_W), lambda l, i: (0, 0)),
        ],
        out_specs=[pl.BlockSpec((1, tm, MEM_W), lambda l, i: (l, i, 0)),
                   pl.BlockSpec((1, tm, MEM_W), lambda l, i: (l, i, 0))],
        out_shape=[jax.ShapeDtypeStruct((DEPTH, n, MEM_W), F32)] * 2,
        compiler_params=_params(2),
        name="memkv",
    )(mem, norm_mem, w_mem_kv, g_mem_k, gsum)


def _mixer_a_kernel(x_ref, g_ref, w_in_ref, prev_ref, wpool_ref, pscale_ref,
                    mk_ref, mv_ref, gq_ref, gsum_ref, w_out_ref,
                    xo_ref, pool_ref, ext_ref, *, nb, ts, pos0):
    j = pl.program_id(1)
    rows = nb * ts
    x = x_ref[...].reshape(rows, D_MODEL)
    h = _rms(x, g_ref[...]).astype(BF16)
    proj = _dot(h, w_in_ref[...])
    u = proj[:, :POOL_W]
    qm = proj[:, POOL_W:]

    @pl.when(j == 0)
    def _():
        ext_ref[:, 0:POOL_PAD, :] = jnp.zeros((nb, POOL_PAD, POOL_W), F32)
        ext_ref[:, 1:POOL_PAD, :] = prev_ref[...]

    @pl.when(j > 0)
    def _():
        ext_ref[:, 0:POOL_PAD, :] = ext_ref[:, ts:ts + POOL_PAD, :]

    ext_ref[:, POOL_PAD:, :] = u.reshape(nb, ts, POOL_W)
    e = ext_ref[...].reshape(nb * (POOL_PAD + ts), POOL_W)
    s2 = e + pltpu.roll(e, 1, 0)
    s4 = s2 + pltpu.roll(s2, 2, 0)
    s8 = s4 + pltpu.roll(s4, 4, 0)
    s16 = s8 + pltpu.roll(s8, 8, 0)

    def tile_rows(a):
        return a.reshape(nb, POOL_PAD + ts, POOL_W)[:, POOL_PAD:, :]

    t_idx = lax.broadcasted_iota(jnp.int32, (1, ts, 1), 1)
    pos1 = (pos0 + 1 + j * ts + t_idx).astype(F32)
    col = lax.broadcasted_iota(jnp.int32, (1, 1, POOL_W), 2)
    pooled = None
    for g, (w, sw) in enumerate(zip(POOL_WINDOWS, (s2, s4, s8, s16))):
        val = tile_rows(sw) / jnp.minimum(float(w), pos1)
        if pooled is None:
            pooled = val
        else:
            pooled = jnp.where(col >= g * POOL_GROUP_W, val, pooled)
    diff = (pooled.reshape(rows, POOL_W) - u).astype(BF16)
    mixed = _dot(diff, wpool_ref[...]) * pscale_ref[...]

    mo = _mem_attend(qm, mk_ref, mv_ref, gq_ref[...], gsum_ref[...], nb, ts)

    y = _dot(mixed.astype(BF16), w_out_ref[0:POOL_W, :])
    y = y + _dot(mo.astype(BF16), w_out_ref[POOL_W:, :])
    xo_ref[...] = (x + y).reshape(nb, ts, D_MODEL)

    @pl.when(j == pl.num_programs(1) - 1)
    def _():
        pool_ref[...] = ext_ref[:, ts + 1:ts + POOL_PAD, :]


def _mixer_a(x, prev, mk, mv, wl, *, nb, ts, pos0):
    b, s, _ = x.shape
    kern = functools.partial(_mixer_a_kernel, nb=nb, ts=ts, pos0=pos0)
    const = lambda i, j: (0, 0)
    return pl.pallas_call(
        kern,
        grid=(b // nb, s // ts),
        in_specs=[
            pl.BlockSpec((nb, ts, D_MODEL), lambda i, j: (i, j, 0)),
            pl.BlockSpec((1, D_MODEL), const),
            pl.BlockSpec((D_MODEL, D_MODEL), const),
            pl.BlockSpec((nb, POOL_STATE, POOL_W), lambda i, j: (i, 0, 0)),
            pl.BlockSpec((POOL_W, POOL_W), const),
            pl.BlockSpec((1, POOL_W), const),
            pl.BlockSpec((nb, N_MEM, MEM_W), lambda i, j: (i, 0, 0)),
            pl.BlockSpec((nb, N_MEM, MEM_W), lambda i, j: (i, 0, 0)),
            pl.BlockSpec((1, MEM_W), const),
            pl.BlockSpec((MEM_W, MEM_W), const),
            pl.BlockSpec((D_MODEL, D_MODEL), const),
        ],
        out_specs=[pl.BlockSpec((nb, ts, D_MODEL), lambda i, j: (i, j, 0)),
                   pl.BlockSpec((nb, POOL_STATE, POOL_W), lambda i, j: (i, 0, 0))],
        out_shape=[jax.ShapeDtypeStruct((b, s, D_MODEL), F32),
                   jax.ShapeDtypeStruct((b, POOL_STATE, POOL_W), F32)],
        scratch_shapes=[pltpu.VMEM((nb, POOL_PAD + ts, POOL_W), F32)],
        compiler_params=_params(2),
        name="mixer_a",
    )(x, wl['g_mix'], wl['w_in'], prev, wl['w_pool'], wl['pool_scale'],
      mk, mv, wl['g_mem_q'], wl['gsum'], wl['w_out'])


FFN_CHUNK = 1408


def _ffn_kernel(x_ref, g_ref, w_in_ref, w_out_ref, o_ref):
    x = x_ref[...]
    h = _rms(x, g_ref[...]).astype(BF16)
    acc = x
    for c0 in range(0, D_FF, FFN_CHUNK):
        gate = _dot(h, w_in_ref[:, c0:c0 + FFN_CHUNK])
        up = _dot(h, w_in_ref[:, D_FF + c0:D_FF + c0 + FFN_CHUNK])
        a = (gate * jax.nn.sigmoid(gate) * up).astype(BF16)
        acc = acc + _dot(a, w_out_ref[c0:c0 + FFN_CHUNK, :])
    o_ref[...] = acc


def _ffn(x, g, w_in, w_out, *, tm):
    n = x.shape[0]
    const = lambda i: (0, 0)
    return pl.pallas_call(
        _ffn_kernel,
        grid=(n // tm,),
        in_specs=[
            pl.BlockSpec((tm, D_MODEL), lambda i: (i, 0)),
            pl.BlockSpec((1, D_MODEL), const),
            pl.BlockSpec((D_MODEL, 2 * D_FF), const, pipeline_mode=pl.Buffered(1)),
            pl.BlockSpec((D_FF, D_MODEL), const, pipeline_mode=pl.Buffered(1)),
        ],
        out_specs=pl.BlockSpec((tm, D_MODEL), lambda i: (i, 0)),
        out_shape=jax.ShapeDtypeStruct((n, D_MODEL), F32),
        compiler_params=_params(1),
        name="ffn",
    )(x, g, w_in, w_out)


def _kvrows_kernel(x_ref, g_ref, w_ref, gl_ref, cos_ref, sin_ref, *rest, with_kv):
    if with_kv:
        wk_ref, wv_ref, kv_ref, kn_ref, v_ref = rest
    else:
        (kv_ref,) = rest
    x = x_ref[...]
    h = _rms(x, g_ref[...]).astype(BF16)
    ck = _dot(h, w_ref[...])
    c = _rms(ck[:, :KV_LORA], gl_ref[...])
    kr = (ck[:, KV_LORA:KV_ROW] * cos_ref[...]
          + ck[:, KV_LORA + LANE:KV_LORA + LANE + QK_ROPE] * sin_ref[...])
    kv_ref[:, 0:KV_LORA] = c
    kv_ref[:, KV_LORA:KV_ROW] = kr
    if with_kv:
        cb = c.astype(BF16)
        kn = _dot(cb, wk_ref[0:KV_LORA, :]) + _dot(kr.astype(BF16), wk_ref[KV_LORA:KV_ROW, :])
        for hd in range(MLA_HEADS):
            sl = slice(hd * SLOT, (hd + 1) * SLOT)
            kh = kn[:, sl]
            inv = lax.rsqrt(jnp.sum(kh * kh, axis=-1, keepdims=True) * (1.0 / QK_DIM) + EPS)
            kn_ref[:, sl] = (kh * inv).astype(BF16)
        v_ref[...] = _dot(cb, wv_ref[...]).astype(BF16)


def _kvrows(x, wk, cos32, sin32, *, tm, seq_tiles, with_kv):
    n = x.shape[0]
    const = lambda i: (0, 0)
    in_specs = [
        pl.BlockSpec((tm, D_MODEL), lambda i: (i, 0)),
        pl.BlockSpec((1, D_MODEL), const),
        pl.BlockSpec((D_MODEL, 4 * LANE), const),
        pl.BlockSpec((1, KV_LORA), const),
        pl.BlockSpec((tm, QK_ROPE), lambda i: (i % seq_tiles, 0)),
        pl.BlockSpec((tm, QK_ROPE), lambda i: (i % seq_tiles, 0)),
    ]
    args = [x, wk['norm_kv'], wk['w_dkv_ext'], wk['g_kv_lora'], cos32, sin32]
    out_specs = [pl.BlockSpec((tm, KV_ROW), lambda i: (i, 0))]
    out_shape = [jax.ShapeDtypeStruct((n, KV_ROW), F32)]
    if with_kv:
        in_specs += [pl.BlockSpec((KV_ROW, Q_SLOTS), const),
                     pl.BlockSpec((KV_LORA, MLA_HEADS * V_HEAD), const)]
        args += [wk['w_kslots'], wk['w_uv']]
        out_specs += [pl.BlockSpec((tm, Q_SLOTS), lambda i: (i, 0)),
                      pl.BlockSpec((tm, MLA_HEADS * V_HEAD), lambda i: (i, 0))]
        out_shape += [jax.ShapeDtypeStruct((n, Q_SLOTS), BF16),
                      jax.ShapeDtypeStruct((n, MLA_HEADS * V_HEAD), BF16)]
    return pl.pallas_call(
        functools.partial(_kvrows_kernel, with_kv=with_kv),
        grid=(n // tm,),
        in_specs=in_specs, out_specs=out_specs, out_shape=out_shape,
        compiler_params=_params(1),
        name="kvrows",
    )(*args)


def _qproj_kernel(x_ref, g_ref, w_in_ref, gql_ref, wm_ref, wr_ref, ga_ref, gb_ref,
                  cos_ref, sin_ref, *rest, absorb):
    if absorb:
        wabs_ref, qabs_ref, qm_ref = rest
    else:
        qf_ref, qm_ref = rest
    x = x_ref[...]
    h = _rms(x, g_ref[...]).astype(BF16)
    proj = _dot(h, w_in_ref[...])
    qm_ref[...] = proj[:, Q_LORA:]
    cq = _rms(proj[:, :Q_LORA], gql_ref[...]).astype(BF16)
    q_main = _dot(cq, wm_ref[...])
    q_rot = _dot(cq, wr_ref[...])
    a_tab = ga_ref[...] * cos_ref[...]
    b_tab = gb_ref[...] * sin_ref[...]
    for hd in range(MLA_HEADS):
        sl = slice(hd * SLOT, (hd + 1) * SLOT)
        qh = q_main[:, sl]
        inv = lax.rsqrt(jnp.sum(qh * qh, axis=-1, keepdims=True) * (1.0 / QK_DIM) + EPS)
        qf = ((qh * a_tab + q_rot[:, sl] * b_tab) * (inv * QK_DIM ** -0.5)).astype(BF16)
        if absorb:
            qabs_ref[hd] = _dot(qf, wabs_ref[hd])
        else:
            qf_ref[:, sl] = qf


def _qproj(x, wl, wk, cos_t, sin_t, *, tm, seq_tiles, absorb):
    n = x.shape[0]
    const = lambda i: (0, 0)
    in_specs = [
        pl.BlockSpec((tm, D_MODEL), lambda i: (i, 0)),
        pl.BlockSpec((1, D_MODEL), const),
        pl.BlockSpec((D_MODEL, Q_LORA + MEM_W), const),
        pl.BlockSpec((1, Q_LORA), const),
        pl.BlockSpec((Q_LORA, Q_SLOTS), const),
        pl.BlockSpec((Q_LORA, Q_SLOTS), const),
        pl.BlockSpec((1, SLOT), const),
        pl.BlockSpec((1, SLOT), const),
        pl.BlockSpec((tm, SLOT), lambda i: (i % seq_tiles, 0)),
        pl.BlockSpec((tm, SLOT), lambda i: (i % seq_tiles, 0)),
    ]
    args = [x, wl['g_mix'], wl['w_in'], wl['g_q_lora'], wl['w_uq_main'], wl['w_uq_rot'],
            wl['g_a'], wl['g_b'], cos_t, sin_t]
    if absorb:
        in_specs.append(pl.BlockSpec((MLA_HEADS, SLOT, KV_ROW), lambda i: (0, 0, 0)))
        args.append(wk['w_abs'])
        out_specs = [pl.BlockSpec((MLA_HEADS, tm, KV_ROW), lambda i: (0, i, 0))]
        out_shape = [jax.ShapeDtypeStruct((MLA_HEADS, n, KV_ROW), F32)]
    else:
        out_specs = [pl.BlockSpec((tm, Q_SLOTS), lambda i: (i, 0))]
        out_shape = [jax.ShapeDtypeStruct((n, Q_SLOTS), BF16)]
    out_specs.append(pl.BlockSpec((tm, MEM_W), lambda i: (i, 0)))
    out_shape.append(jax.ShapeDtypeStruct((n, MEM_W), F32))
    return pl.pallas_call(
        functools.partial(_qproj_kernel, absorb=absorb),
        grid=(n // tm,),
        in_specs=in_specs, out_specs=out_specs, out_shape=out_shape,
        compiler_params=_params(1),
        name="qproj",
    )(*args)


def _mla_prompt_kernel(q_ref, k_ref, v_ref, o_ref, *, tq):
    qi = pl.program_id(2)
    q = q_ref[0]
    lane = lax.broadcasted_iota(jnp.int32, (1, 2 * V_HEAD), 1)
    first = lane < V_HEAD
    row = qi * tq + lax.broadcasted_iota(jnp.int32, (tq, 1), 0)

    def body(kb, carry):
        m0, l0, m1, l1, acc = carry
        start = pl.multiple_of(kb * tq, tq)
        k = k_ref[0, pl.ds(start, tq), :]
        v = v_ref[0, pl.ds(start, tq), :]
        col = start + lax.broadcasted_iota(jnp.int32, (1, tq), 1)
        visible = col <= row
        new = []
        for hd, (m, l) in enumerate(((m0, l0), (m1, l1))):
            sl = slice(hd * SLOT, (hd + 1) * SLOT)
            s = _dot_nt(q[:, sl], k[:, sl])
            s = jnp.where(visible, s, NEG)
            m_new = jnp.maximum(m, jnp.max(s, axis=-1, keepdims=True))
            alpha = jnp.exp(m - m_new)
            p = jnp.exp(s - m_new)
            l_new = alpha * l + jnp.sum(p, axis=-1, keepdims=True)
            mine = first if hd == 0 else jnp.logical_not(first)
            vh = jnp.where(mine, v, jnp.zeros_like(v))
            acc = acc * jnp.where(mine, alpha, 1.0) + _dot(p.astype(BF16), vh)
            new += [m_new, l_new]
        return (*new, acc)

    init = (jnp.full((tq, 1), NEG, F32), jnp.zeros((tq, 1), F32),
            jnp.full((tq, 1), NEG, F32), jnp.zeros((tq, 1), F32),
            jnp.zeros((tq, 2 * V_HEAD), F32))
    m0, l0, m1, l1, acc = lax.fori_loop(0, qi + 1, body, init)
    o_ref[0] = (acc / jnp.where(first, l0, l1)).astype(BF16)


def _mla_prompt(qf, kn, v, *, tq):
    b, s, _ = qf.shape
    pairs = MLA_HEADS // 2
    return pl.pallas_call(
        functools.partial(_mla_prompt_kernel, tq=tq),
        grid=(b, pairs, s // tq),
        in_specs=[
            pl.BlockSpec((1, tq, 2 * SLOT), lambda i, hp, qi: (i, qi, hp)),
            pl.BlockSpec((1, s, 2 * SLOT), lambda i, hp, qi: (i, 0, hp)),
            pl.BlockSpec((1, s, 2 * V_HEAD), lambda i, hp, qi: (i, 0, hp)),
        ],
        out_specs=pl.BlockSpec((1, tq, 2 * V_HEAD), lambda i, hp, qi: (i, qi, hp)),
        out_shape=jax.ShapeDtypeStruct((b, s, MLA_HEADS * V_HEAD), BF16),
        compiler_params=_params(3),
        name="mla_prompt",
    )(qf, kn, v)


Q_ROWS = MLA_HEADS * 8


def _mla_sample_kernel(pt_ref, qabs_ref, wukt_ref, kvnew_ref, *rest, pps, n_tok):
    page_refs = rest[:pps]
    o_ref = rest[pps]
    m_ref, l_ref, acc_ref, lhs_ref, knew_ref = rest[pps + 1:]
    step = pl.program_id(1)
    qabs = qabs_ref[...].reshape(Q_ROWS, KV_ROW)
    q_rope = qabs[:, KV_LORA:].astype(BF16)

    @pl.when(step == 0)
    def _():
        m_ref[...] = jnp.full(m_ref.shape, NEG, F32)
        l_ref[...] = jnp.zeros(l_ref.shape, F32)
        acc_ref[...] = jnp.zeros(acc_ref.shape, F32)
        lhs_ref[0:MLA_HEADS * QK_NOPE, :] = wukt_ref[...]
        lhs_ref[MLA_HEADS * QK_NOPE:, :] = qabs[:, :KV_LORA].astype(BF16)

    ones = jnp.ones((8, QK_ROPE), BF16)

    def attend(kv, causal):
        t = kv.shape[0]
        c = kv[:, :KV_LORA].astype(BF16)
        kr32 = kv[:, KV_LORA:]
        kr = kr32.astype(BF16)
        r1 = _dot_nt(lhs_ref[...], c)
        kn = r1[:MLA_HEADS * QK_NOPE]
        ss = jnp.sum((kn * kn).reshape(MLA_HEADS, QK_NOPE, t), axis=1)
        kr2 = kr32 * kr32
        kr2_hi = kr2.astype(BF16)
        kr2_lo = (kr2 - kr2_hi.astype(F32)).astype(BF16)
        krsq = (_dot_nt(ones, kr2_hi) + _dot_nt(ones, kr2_lo))[0:1]
        kinv = lax.rsqrt((ss + krsq) * (1.0 / QK_DIM) + EPS)
        sc = r1[MLA_HEADS * QK_NOPE:] + _dot_nt(q_rope, kr)
        sc = sc.reshape(MLA_HEADS, n_tok, t) * kinv[:, None, :]
        if causal:
            tok = lax.broadcasted_iota(jnp.int32, (1, n_tok, t), 1)
            key = lax.broadcasted_iota(jnp.int32, (1, n_tok, t), 2)
            sc = jnp.where(key <= tok, sc, NEG)
        sc = sc.reshape(Q_ROWS, t)
        m_prev = m_ref[...]
        m_new = jnp.maximum(m_prev, jnp.max(sc, axis=-1, keepdims=True))
        alpha = jnp.exp(m_prev - m_new)
        p = jnp.exp(sc - m_new)
        l_ref[...] = alpha * l_ref[...] + jnp.sum(p, axis=-1, keepdims=True)
        acc_ref[...] = alpha * acc_ref[...] + _dot(p.astype(BF16), c)
        m_ref[...] = m_new

    attend(jnp.concatenate([r[0] for r in page_refs], axis=0), causal=False)

    @pl.when(step == pl.num_programs(1) - 1)
    def _():
        knew_ref[...] = jnp.zeros(knew_ref.shape, F32)
        knew_ref[0:n_tok, :] = kvnew_ref[0]
        attend(knew_ref[...], causal=True)
        o = acc_ref[...] / l_ref[...]
        for hd in range(MLA_HEADS):
            o_ref[0, :, hd * KV_LORA:(hd + 1) * KV_LORA] = o[hd * n_tok:(hd + 1) * n_tok, :]


def _mla_sample(page_table, qabs, w_ukt, kv_new, cache_kv, *, pps):
    n_seq, n_pages = page_table.shape
    n_tok = kv_new.shape[1]
    page = cache_kv.shape[1]
    assert n_tok == 8 and n_pages % pps == 0

    def page_spec(i):
        return pl.BlockSpec((1, page, KV_ROW), lambda b, s, pt: (pt[b, s * pps + i], 0, 0))

    grid_spec = pltpu.PrefetchScalarGridSpec(
        num_scalar_prefetch=1,
        grid=(n_seq, n_pages // pps),
        in_specs=[
            pl.BlockSpec((MLA_HEADS, 1, n_tok, KV_ROW), lambda b, s, pt: (0, b, 0, 0)),
            pl.BlockSpec((MLA_HEADS * QK_NOPE, KV_LORA), lambda b, s, pt: (0, 0)),
            pl.BlockSpec((1, n_tok, KV_ROW), lambda b, s, pt: (b, 0, 0)),
        ] + [page_spec(i) for i in range(pps)],
        out_specs=pl.BlockSpec((1, n_tok, MLA_HEADS * KV_LORA), lambda b, s, pt: (b, 0, 0)),
        scratch_shapes=[
            pltpu.VMEM((Q_ROWS, 1), F32),
            pltpu.VMEM((Q_ROWS, 1), F32),
            pltpu.VMEM((Q_ROWS, KV_LORA), F32),
            pltpu.VMEM((MLA_HEADS * QK_NOPE + Q_ROWS, KV_LORA), BF16),
            pltpu.VMEM((LANE, KV_ROW), F32),
        ],
    )
    return pl.pallas_call(
        functools.partial(_mla_sample_kernel, pps=pps, n_tok=n_tok),
        grid_spec=grid_spec,
        out_shape=jax.ShapeDtypeStruct((n_seq, n_tok, MLA_HEADS * KV_LORA), F32),
        compiler_params=_params(2),
        name="mla_sample",
    )(page_table, qabs.reshape(MLA_HEADS, n_seq, n_tok, KV_ROW), w_ukt, kv_new,
      *([cache_kv] * pps))


def _out_b_kernel(x_ref, tok_ref, qm_ref, mk_ref, mv_ref, gq_ref, gsum_ref, w_out_ref,
                  *rest, nb, ts, latent):
    if latent:
        wuv_ref, o_ref = rest
        tok = _dot(tok_ref[...].astype(BF16), wuv_ref[...]).astype(BF16)
    else:
        (o_ref,) = rest
        tok = tok_ref[...]
    mo = _mem_attend(qm_ref[...], mk_ref, mv_ref, gq_ref[...], gsum_ref[...], nb, ts)
    y = _dot(tok, w_out_ref[0:POOL_W, :]) + _dot(mo.astype(BF16), w_out_ref[POOL_W:, :])
    o_ref[...] = x_ref[...] + y


def _out_b(x, tok, qm, mk, mv, wl, wk, *, nb, ts, latent):
    n = x.shape[0]
    rows = nb * ts
    tok_w = tok.shape[1]
    const = lambda i: (0, 0)
    seq_len = n // mk.shape[0]
    mem_idx = lambda i: (i * rows // (seq_len * nb), 0, 0)
    in_specs = [
        pl.BlockSpec((rows, D_MODEL), lambda i: (i, 0)),
        pl.BlockSpec((rows, tok_w), lambda i: (i, 0)),
        pl.BlockSpec((rows, MEM_W), lambda i: (i, 0)),
        pl.BlockSpec((nb, N_MEM, MEM_W), mem_idx),
        pl.BlockSpec((nb, N_MEM, MEM_W), mem_idx),
        pl.BlockSpec((1, MEM_W), const),
        pl.BlockSpec((MEM_W, MEM_W), const),
        pl.BlockSpec((D_MODEL, D_MODEL), const),
    ]
    args = [x, tok, qm, mk, mv, wl['g_mem_q'], wl['gsum'], wl['w_out']]
    if latent:
        in_specs.append(pl.BlockSpec((MLA_HEADS * KV_LORA, MLA_HEADS * V_HEAD), const))
        args.append(wk['w_uv_bd'])
    return pl.pallas_call(
        functools.partial(_out_b_kernel, nb=nb, ts=ts, latent=latent),
        grid=(n // rows,),
        in_specs=in_specs,
        out_specs=pl.BlockSpec((rows, D_MODEL), lambda i: (i, 0)),
        out_shape=jax.ShapeDtypeStruct((n, D_MODEL), F32),
        compiler_params=_params(1),
        name="out_b",
    )(*args)


def _rope_tables(pos):
    half = QK_ROPE // 2
    inv_freq = ROPE_BASE ** (-jnp.arange(half, dtype=F32) / half)
    ang = pos.astype(F32)[:, None] * inv_freq[None, :]
    cos32 = jnp.concatenate([jnp.cos(ang)] * 2, axis=-1)
    sin32 = jnp.concatenate([jnp.sin(ang)] * 2, axis=-1)
    n = pos.shape[0]
    cos_t = jnp.concatenate([jnp.ones((n, QK_NOPE), F32), cos32,
                             jnp.zeros((n, SLOT - QK_DIM), F32)], axis=-1)
    sin_t = jnp.concatenate([jnp.zeros((n, QK_NOPE), F32), sin32,
                             jnp.zeros((n, SLOT - QK_DIM), F32)], axis=-1)
    return cos32, sin32, cos_t, sin_t


def _rot_cols(w):
    half = QK_ROPE // 2
    return jnp.concatenate([-w[..., half:], w[..., :half]], axis=-1)


def _prep_weights(norm_mix, norm_ffn, w_out, w_ffn_in, w_ffn_out, g_mem_q, w_in_a, w_pool_grp,
                  pool_scale, w_in_b, g_q_lora, w_uq, g_q, norm_kv, w_dkv, g_kv_lora, w_uk, w_uv,
                  g_k_nope, g_k_rope):
    half = QK_ROPE // 2
    gsum = jnp.asarray(np.kron(np.eye(MEM_HEADS), np.ones((MEM_HEAD_DIM, MEM_HEAD_DIM))), BF16)
    layers = []
    for l in range(DEPTH):
        wl = dict(
            g_mix=norm_mix[l][None], g_ffn=norm_ffn[l][None],
            w_out=w_out[l].astype(BF16),
            w_ffn_in=w_ffn_in[l].astype(BF16), w_ffn_out=w_ffn_out[l].astype(BF16),
            g_mem_q=jnp.tile(g_mem_q[l], MEM_HEADS)[None], gsum=gsum)
        if l < N_A_LAYERS:
            wp = jnp.zeros((POOL_W, POOL_W), F32)
            for g in range(len(POOL_WINDOWS)):
                sl = slice(g * POOL_GROUP_W, (g + 1) * POOL_GROUP_W)
                wp = wp.at[sl, sl].set(w_pool_grp[l, g])
            wl.update(w_in=w_in_a[l].astype(BF16), w_pool=wp.astype(BF16),
                      pool_scale=pool_scale[l][None])
        else:
            j = l - N_A_LAYERS
            wq = w_uq[j].reshape(Q_LORA, MLA_HEADS, QK_DIM)
            zpad = jnp.zeros((Q_LORA, MLA_HEADS, SLOT - QK_DIM), F32)
            w_main = jnp.concatenate([wq, zpad], axis=-1).reshape(Q_LORA, Q_SLOTS)
            w_rot = jnp.concatenate([jnp.zeros((Q_LORA, MLA_HEADS, QK_NOPE), F32),
                                     _rot_cols(wq[..., QK_NOPE:]), zpad],
                                    axis=-1).reshape(Q_LORA, Q_SLOTS)
            gk2 = jnp.concatenate([g_k_rope, g_k_rope])
            gq = g_q[j]
            g_rot = jnp.concatenate([gq[QK_NOPE + half:], gq[QK_NOPE:QK_NOPE + half]])
            pad = jnp.zeros((SLOT - QK_DIM,), F32)
            g_a = jnp.concatenate([gq[:QK_NOPE] * g_k_nope, gq[QK_NOPE:] * gk2, pad])
            g_b = jnp.concatenate([jnp.zeros((QK_NOPE,), F32), g_rot * gk2, pad])
            wl.update(w_in=w_in_b[j].astype(BF16), g_q_lora=g_q_lora[j][None],
                      w_uq_main=w_main.astype(BF16), w_uq_rot=w_rot.astype(BF16),
                      g_a=g_a[None], g_b=g_b[None])
        layers.append(wl)

    zcol = jnp.zeros((D_MODEL, LANE - QK_ROPE), F32)
    w_dkv_ext = jnp.concatenate([w_dkv, zcol, _rot_cols(w_dkv[:, KV_LORA:]), zcol], axis=-1)
    wuk_h = w_uk.reshape(KV_LORA, MLA_HEADS, QK_NOPE)
    top = jnp.concatenate([wuk_h, jnp.zeros((KV_LORA, MLA_HEADS, SLOT - QK_NOPE), F32)], axis=-1)
    eye = jnp.eye(QK_ROPE, dtype=F32)
    bot = jnp.concatenate([jnp.zeros((QK_ROPE, QK_NOPE), F32), eye,
                           jnp.zeros((QK_ROPE, SLOT - QK_DIM), F32)], axis=-1)
    bot = jnp.broadcast_to(bot[:, None, :], (QK_ROPE, MLA_HEADS, SLOT))
    w_kslots = jnp.concatenate([top, bot], axis=0).reshape(KV_ROW, Q_SLOTS)
    w_abs = jnp.zeros((MLA_HEADS, SLOT, KV_ROW), F32)
    w_abs = w_abs.at[:, :QK_NOPE, :KV_LORA].set(jnp.transpose(wuk_h, (1, 2, 0)))
    w_abs = w_abs.at[:, QK_NOPE:QK_DIM, KV_LORA:].set(jnp.broadcast_to(eye, (MLA_HEADS,) + eye.shape))
    wuv_h = w_uv.reshape(KV_LORA, MLA_HEADS, V_HEAD)
    w_uv_bd = jnp.einsum('chd,hg->hcgd', wuv_h, jnp.eye(MLA_HEADS, dtype=F32))
    w_uv_bd = w_uv_bd.reshape(MLA_HEADS * KV_LORA, MLA_HEADS * V_HEAD)
    shared = dict(
        norm_kv=norm_kv[None], g_kv_lora=g_kv_lora[None],
        w_dkv_ext=w_dkv_ext.astype(BF16), w_kslots=w_kslots.astype(BF16),
        w_uv=w_uv.astype(BF16), w_abs=w_abs.astype(BF16),
        w_ukt=jnp.transpose(w_uk).astype(BF16), w_uv_bd=w_uv_bd.astype(BF16))
    return layers, shared


def _trunk(x, pos0, pool_prev, mem_k, mem_v, layers, shared, *, nb, ts, tm, past):
    b, s, _ = x.shape
    n = b * s
    pos = pos0 + jnp.arange(s)
    tabs = _rope_tables(pos)
    if s >= tm:
        tmk, seq_tiles = tm, s // tm
    else:
        reps = min(tm, n) // s
        tmk, seq_tiles = reps * s, 1
        tabs = tuple(jnp.tile(t, (reps, 1)) for t in tabs)
    cos32, sin32, cos_t, sin_t = tabs
    pool_new = []
    kv_new = None
    for l in range(DEPTH):
        wl = layers[l]
        if l < N_A_LAYERS:
            x, st = _mixer_a(x, pool_prev[l], mem_k[l], mem_v[l], wl, nb=nb, ts=ts, pos0=pos0)
            pool_new.append(st)
            xf = x.reshape(n, D_MODEL)
        else:
            xf = x.reshape(n, D_MODEL)
            if l == N_A_LAYERS:
                if past is None:
                    kv_new, kn, v = _kvrows(xf, shared, cos32, sin32, tm=tmk,
                                            seq_tiles=seq_tiles, with_kv=True)
                else:
                    (kv_new,) = _kvrows(xf, shared, cos32, sin32, tm=tmk,
                                        seq_tiles=seq_tiles, with_kv=False)
            if past is None:
                qf, qm = _qproj(xf, wl, shared, cos_t, sin_t, tm=tmk, seq_tiles=seq_tiles,
                                absorb=False)
                tok = _mla_prompt(qf.reshape(b, s, Q_SLOTS), kn.reshape(b, s, Q_SLOTS),
                                  v.reshape(b, s, MLA_HEADS * V_HEAD), tq=256)
                tok = tok.reshape(n, MLA_HEADS * V_HEAD)
                nb_o, ts_o = 1, tmk
            else:
                cache_kv, page_table = past
                qabs, qm = _qproj(xf, wl, shared, cos_t, sin_t, tm=tmk, seq_tiles=seq_tiles,
                                  absorb=True)
                tok = _mla_sample(page_table, qabs, shared['w_ukt'],
                                  kv_new.reshape(b, s, KV_ROW), cache_kv, pps=16)
                tok = tok.reshape(n, MLA_HEADS * KV_LORA)
                nb_o, ts_o = nb, ts
            xf = _out_b(xf, tok, qm, mem_k[l], mem_v[l], wl, shared, nb=nb_o, ts=ts_o,
                        latent=past is not None)
        xf = _ffn(xf, wl['g_ffn'], wl['w_ffn_in'], wl['w_ffn_out'], tm=min(tm, n))
        x = xf.reshape(b, s, D_MODEL)
    return x, kv_new.reshape(b, s, KV_ROW), jnp.stack(pool_new)


def kernel(x_prompt, x_sample, cache_kv, state_pool, cache_mem_k, cache_mem_v, page_table, mem_prompt,
           norm_mix, norm_ffn, w_out, w_ffn_in, w_ffn_out, norm_mem, w_mem_kv, g_mem_q, g_mem_k,
           w_in_a, w_pool_grp, pool_scale, w_in_b, g_q_lora, w_uq, g_q,
           norm_kv, w_dkv, g_kv_lora, w_uk, w_uv, g_k_nope, g_k_rope):
    layers, shared = _prep_weights(norm_mix, norm_ffn, w_out, w_ffn_in, w_ffn_out, g_mem_q, w_in_a,
                                   w_pool_grp, pool_scale, w_in_b, g_q_lora, w_uq, g_q, norm_kv,
                                   w_dkv, g_kv_lora, w_uk, w_uv, g_k_nope, g_k_rope)
    b_p, s_p, _ = x_prompt.shape
    n_seq, s_s, _ = x_sample.shape

    mem_flat = mem_prompt.reshape(b_p * N_MEM, D_MODEL)
    mk_p, mv_p = _memkv(mem_flat, norm_mem[:, None, :], w_mem_kv.astype(BF16),
                        jnp.tile(g_mem_k, (1, MEM_HEADS))[:, None, :], layers[0]['gsum'])
    mk_p = mk_p.reshape(DEPTH, b_p, N_MEM, MEM_W)
    mv_p = mv_p.reshape(DEPTH, b_p, N_MEM, MEM_W)
    pool0 = jnp.zeros((N_A_LAYERS, b_p, POOL_STATE, POOL_W), F32)
    y_p, kv_p, pool_p = _trunk(x_prompt, 0, pool0, mk_p, mv_p, layers, shared,
                               nb=1, ts=256, tm=512, past=None)

    past_len = page_table.shape[1] * cache_kv.shape[1]
    mk_s = cache_mem_k.reshape(DEPTH, n_seq, N_MEM, MEM_W)
    mv_s = cache_mem_v.reshape(DEPTH, n_seq, N_MEM, MEM_W)
    y_s, kv_s, pool_s = _trunk(x_sample, past_len, state_pool, mk_s, mv_s, layers, shared,
                               nb=n_seq // 2, ts=s_s, tm=512, past=(cache_kv, page_table))

    shape5 = (DEPTH, b_p, N_MEM, MEM_HEADS, MEM_HEAD_DIM)
    return (y_p, y_s, kv_p, kv_s, pool_p, pool_s, mk_p.reshape(shape5), mv_p.reshape(shape5))
```

```python
import functools

import numpy as np
import jax
import jax.numpy as jnp
from jax import lax
from jax.experimental import pallas as pl
from jax.experimental.pallas import tpu as pltpu

D_MODEL = 1024
DEPTH = 4
N_A_LAYERS = 2
MEM_HEADS = 4
MEM_HEAD_DIM = 64
MEM_W = MEM_HEADS * MEM_HEAD_DIM
N_MEM = 256
POOL_W = D_MODEL - MEM_W
POOL_WINDOWS = (2, 4, 8, 16)
POOL_GROUP_W = POOL_W // len(POOL_WINDOWS)
POOL_STATE = max(POOL_WINDOWS) - 1
POOL_PAD = POOL_STATE + 1
MLA_HEADS = 12
QK_NOPE = 64
QK_ROPE = 32
QK_DIM = QK_NOPE + QK_ROPE
V_HEAD = 64
KV_LORA = 256
Q_LORA = 384
KV_ROW = KV_LORA + QK_ROPE
ROPE_BASE = 10000.0
D_FF = 2816
EPS = 1e-6
NEG = -1e30
LOG2E = 1.4426950408889634

LANE = 128
SLOT = LANE
Q_SLOTS = MLA_HEADS * SLOT
VMEM_LIMIT = 56 * 1024 * 1024

BF16 = jnp.bfloat16
F32 = jnp.float32
NT_DIMS = (((1,), (1,)), ((), ()))


def _params(n_grid):
    return pltpu.CompilerParams(dimension_semantics=("arbitrary",) * n_grid,
                                vmem_limit_bytes=VMEM_LIMIT)


def _dot(a, b):
    return jnp.dot(a, b, preferred_element_type=F32)


def _dot_nt(a, b):
    return lax.dot_general(a, b, NT_DIMS, preferred_element_type=F32)


def _split_dot(x, w):
    hi = x.astype(BF16)
    lo = (x - hi.astype(F32)).astype(BF16)
    return _dot(hi, w) + _dot(lo, w)


def _rms(x, g):
    return x * lax.rsqrt(jnp.mean(x * x, axis=-1, keepdims=True) + EPS) * g


def _mem_attend(qm, mkt_ref, mvt_ref, gq, gsum, nb, ts):
    ssq = _split_dot(qm * qm, gsum)
    qn = qm * lax.rsqrt(ssq * (1.0 / MEM_HEAD_DIM) + EPS) * (gq * MEM_HEAD_DIM ** -0.5)
    q3 = qn.reshape(nb, ts, MEM_W)
    mkt = mkt_ref[...].astype(BF16)
    mvt = mvt_ref[...].astype(BF16)
    lane = lax.broadcasted_iota(jnp.int32, (1, 1, MEM_W), 2)
    out = jnp.zeros((nb, ts, MEM_W), F32)
    for h in range(MEM_HEADS):
        in_head = (lane >= h * MEM_HEAD_DIM) & (lane < (h + 1) * MEM_HEAD_DIM)
        qh = jnp.where(in_head, q3, 0.0).astype(BF16)
        s = jnp.einsum('bqd,bdm->bqm', qh, mkt, preferred_element_type=F32)
        s = s - jnp.max(s, axis=-1, keepdims=True)
        p = jnp.exp(s)
        p = p / jnp.sum(p, axis=-1, keepdims=True)
        o = jnp.einsum('bqm,bdm->bqd', p.astype(BF16), mvt, preferred_element_type=F32)
        out = out + jnp.where(in_head, o, 0.0)
    return out.reshape(nb * ts, MEM_W)


def _memkv_kernel(mem_ref, gnorm_ref, wt_ref, gk_ref, kt_ref, vt_ref):
    x = mem_ref[0]
    xn = x * lax.rsqrt(jnp.mean(x * x, axis=-1, keepdims=True) + EPS)
    hm = (xn * gnorm_ref[0]).astype(BF16)
    kvt = _dot_nt(wt_ref[0], hm)
    kt = kvt[:MEM_W].reshape(MEM_HEADS, MEM_HEAD_DIM, N_MEM)
    inv = lax.rsqrt(jnp.mean(kt * kt, axis=1, keepdims=True) + EPS)
    kt_ref[0, 0] = (kt * inv * gk_ref[0]).reshape(MEM_W, N_MEM)
    vt_ref[0, 0] = kvt[MEM_W:]


def _memkv(mem, norm_mem, w_mem_kv_t, g_mem_k):
    b = mem.shape[0]
    return pl.pallas_call(
        _memkv_kernel,
        grid=(DEPTH, b),
        in_specs=[
            pl.BlockSpec((1, N_MEM, D_MODEL), lambda l, i: (i, 0, 0)),
            pl.BlockSpec((1, 1, D_MODEL), lambda l, i: (l, 0, 0)),
            pl.BlockSpec((1, 2 * MEM_W, D_MODEL), lambda l, i: (l, 0, 0)),
            pl.BlockSpec((1, 1, MEM_HEAD_DIM, 1), lambda l, i: (l, 0, 0, 0)),
        ],
        out_specs=[pl.BlockSpec((1, 1, MEM_W, N_MEM), lambda l, i: (l, i, 0, 0)),
                   pl.BlockSpec((1, 1, MEM_W, N_MEM), lambda l, i: (l, i, 0, 0))],
        out_shape=[jax.ShapeDtypeStruct((DEPTH, b, MEM_W, N_MEM), F32)] * 2,
        compiler_params=_params(2),
        name="memkv",
    )(mem, norm_mem, w_mem_kv_t, g_mem_k)


def _mixer_a_kernel(x_ref, g_ref, w_in_ref, prev_ref, wpool_ref, pscale_ref,
                    mk_ref, mv_ref, gq_ref, gsum_ref, w_out_ref,
                    xo_ref, pool_ref, ext_ref, *, nb, ts, pos0):
    j = pl.program_id(1)
    rows = nb * ts
    x = x_ref[...].reshape(rows, D_MODEL)
    h = _rms(x, g_ref[...]).astype(BF16)
    proj = _dot(h, w_in_ref[...])
    u = proj[:, :POOL_W]
    qm = proj[:, POOL_W:]

    @pl.when(j == 0)
    def _():
        ext_ref[:, 0:POOL_PAD, :] = jnp.zeros((nb, POOL_PAD, POOL_W), F32)
        ext_ref[:, 1:POOL_PAD, :] = prev_ref[...]

    @pl.when(j > 0)
    def _():
        ext_ref[:, 0:POOL_PAD, :] = ext_ref[:, ts:ts + POOL_PAD, :]

    ext_ref[:, POOL_PAD:, :] = u.reshape(nb, ts, POOL_W)
    e = ext_ref[...].reshape(nb * (POOL_PAD + ts), POOL_W)
    s2 = e + pltpu.roll(e, 1, 0)
    s4 = s2 + pltpu.roll(s2, 2, 0)
    s8 = s4 + pltpu.roll(s4, 4, 0)
    s16 = s8 + pltpu.roll(s8, 8, 0)

    def tile_rows(a):
        return a.reshape(nb, POOL_PAD + ts, POOL_W)[:, POOL_PAD:, :]

    t_idx = lax.broadcasted_iota(jnp.int32, (1, ts, 1), 1)
    pos1 = (pos0 + 1 + j * ts + t_idx).astype(F32)
    col = lax.broadcasted_iota(jnp.int32, (1, 1, POOL_W), 2)
    pooled = None
    for g, (w, sw) in enumerate(zip(POOL_WINDOWS, (s2, s4, s8, s16))):
        val = tile_rows(sw) / jnp.minimum(float(w), pos1)
        if pooled is None:
            pooled = val
        else:
            pooled = jnp.where(col >= g * POOL_GROUP_W, val, pooled)
    diff = (pooled.reshape(rows, POOL_W) - u).astype(BF16)
    mixed = _dot(diff, wpool_ref[...]) * pscale_ref[...]

    mo = _mem_attend(qm, mk_ref, mv_ref, gq_ref[...], gsum_ref[...], nb, ts)

    y = _dot(mixed.astype(BF16), w_out_ref[0:POOL_W, :])
    y = y + _dot(mo.astype(BF16), w_out_ref[POOL_W:, :])
    xo_ref[...] = (x + y).reshape(nb, ts, D_MODEL)

    @pl.when(j == pl.num_programs(1) - 1)
    def _():
        pool_ref[...] = ext_ref[:, ts + 1:ts + POOL_PAD, :]


def _mixer_a(x, prev, mk, mv, wl, *, nb, ts, pos0):
    b, s, _ = x.shape
    kern = functools.partial(_mixer_a_kernel, nb=nb, ts=ts, pos0=pos0)
    const = lambda i, j: (0, 0)
    return pl.pallas_call(
        kern,
        grid=(b // nb, s // ts),
        in_specs=[
            pl.BlockSpec((nb, ts, D_MODEL), lambda i, j: (i, j, 0)),
            pl.BlockSpec((1, D_MODEL), const),
            pl.BlockSpec((D_MODEL, D_MODEL), const),
            pl.BlockSpec((nb, POOL_STATE, POOL_W), lambda i, j: (i, 0, 0)),
            pl.BlockSpec((POOL_W, POOL_W), const),
            pl.BlockSpec((1, POOL_W), const),
            pl.BlockSpec((nb, MEM_W, N_MEM), lambda i, j: (i, 0, 0)),
            pl.BlockSpec((nb, MEM_W, N_MEM), lambda i, j: (i, 0, 0)),
            pl.BlockSpec((1, MEM_W), const),
            pl.BlockSpec((MEM_W, MEM_W), const),
            pl.BlockSpec((D_MODEL, D_MODEL), const),
        ],
        out_specs=[pl.BlockSpec((nb, ts, D_MODEL), lambda i, j: (i, j, 0)),
                   pl.BlockSpec((nb, POOL_STATE, POOL_W), lambda i, j: (i, 0, 0))],
        out_shape=[jax.ShapeDtypeStruct((b, s, D_MODEL), F32),
                   jax.ShapeDtypeStruct((b, POOL_STATE, POOL_W), F32)],
        scratch_shapes=[pltpu.VMEM((nb, POOL_PAD + ts, POOL_W), F32)],
        compiler_params=_params(2),
        name="mixer_a",
    )(x, wl['g_mix'], wl['w_in'], prev, wl['w_pool'], wl['pool_scale'],
      mk, mv, wl['g_mem_q'], wl['gsum'], wl['w_out'])


FFN_CHUNK = 1408


def _ffn_kernel(x_ref, g_ref, w_in_ref, w_out_ref, o_ref):
    x = x_ref[...]
    h = _rms(x, g_ref[...]).astype(BF16)
    acc = x
    for c0 in range(0, D_FF, FFN_CHUNK):
        gate = _dot(h, w_in_ref[:, c0:c0 + FFN_CHUNK])
        up = _dot(h, w_in_ref[:, D_FF + c0:D_FF + c0 + FFN_CHUNK])
        a = (gate * jax.nn.sigmoid(gate) * up).astype(BF16)
        acc = acc + _dot(a, w_out_ref[c0:c0 + FFN_CHUNK, :])
    o_ref[...] = acc


def _ffn(x, g, w_in, w_out, *, tm):
    n = x.shape[0]
    const = lambda i: (0, 0)
    return pl.pallas_call(
        _ffn_kernel,
        grid=(n // tm,),
        in_specs=[
            pl.BlockSpec((tm, D_MODEL), lambda i: (i, 0)),
            pl.BlockSpec((1, D_MODEL), const),
            pl.BlockSpec((D_MODEL, 2 * D_FF), const, pipeline_mode=pl.Buffered(1)),
            pl.BlockSpec((D_FF, D_MODEL), const, pipeline_mode=pl.Buffered(1)),
        ],
        out_specs=pl.BlockSpec((tm, D_MODEL), lambda i: (i, 0)),
        out_shape=jax.ShapeDtypeStruct((n, D_MODEL), F32),
        compiler_params=_params(1),
        name="ffn",
    )(x, g, w_in, w_out)


def _kvrows_kernel(x_ref, g_ref, w_ref, gl_ref, cos_ref, sin_ref, *rest, with_kv):
    if with_kv:
        wk_ref, wv_ref, kv_ref, kn_ref, v_ref = rest
    else:
        (kv_ref,) = rest
    x = x_ref[...]
    h = _rms(x, g_ref[...]).astype(BF16)
    ck = _dot(h, w_ref[...])
    c = _rms(ck[:, :KV_LORA], gl_ref[...])
    kr = (ck[:, KV_LORA:KV_ROW] * cos_ref[...]
          + ck[:, KV_LORA + LANE:KV_LORA + LANE + QK_ROPE] * sin_ref[...])
    kv_ref[:, 0:KV_LORA] = c
    kv_ref[:, KV_LORA:KV_ROW] = kr
    if with_kv:
        cb = c.astype(BF16)
        kn = _dot(cb, wk_ref[0:KV_LORA, :]) + _dot(kr.astype(BF16), wk_ref[KV_LORA:KV_ROW, :])
        for hd in range(MLA_HEADS):
            sl = slice(hd * SLOT, (hd + 1) * SLOT)
            kh = kn[:, sl]
            inv = lax.rsqrt(jnp.sum(kh * kh, axis=-1, keepdims=True) * (1.0 / QK_DIM) + EPS)
            kn_ref[:, sl] = (kh * inv).astype(BF16)
        vt = _dot_nt(wv_ref[...], cb)
        row = lax.broadcasted_iota(jnp.int32, (Q_SLOTS, 1), 0)
        v_ref[0] = jnp.where(row % SLOT == V_HEAD, 1.0, vt).astype(BF16)


def _kvrows(x, wk, cos32, sin32, *, tm, seq_tiles, with_kv):
    n = x.shape[0]
    const = lambda i: (0, 0)
    in_specs = [
        pl.BlockSpec((tm, D_MODEL), lambda i: (i, 0)),
        pl.BlockSpec((1, D_MODEL), const),
        pl.BlockSpec((D_MODEL, 4 * LANE), const),
        pl.BlockSpec((1, KV_LORA), const),
        pl.BlockSpec((tm, QK_ROPE), lambda i: (i % seq_tiles, 0)),
        pl.BlockSpec((tm, QK_ROPE), lambda i: (i % seq_tiles, 0)),
    ]
    args = [x, wk['norm_kv'], wk['w_dkv_ext'], wk['g_kv_lora'], cos32, sin32]
    out_specs = [pl.BlockSpec((tm, KV_ROW), lambda i: (i, 0))]
    out_shape = [jax.ShapeDtypeStruct((n, KV_ROW), F32)]
    if with_kv:
        in_specs += [pl.BlockSpec((KV_ROW, Q_SLOTS), const),
                     pl.BlockSpec((Q_SLOTS, KV_LORA), const)]
        args += [wk['w_kslots'], wk['w_uvt_slots']]
        out_specs += [pl.BlockSpec((tm, Q_SLOTS), lambda i: (i, 0)),
                      pl.BlockSpec((1, Q_SLOTS, tm), lambda i: (i // seq_tiles, 0, i % seq_tiles))]
        out_shape += [jax.ShapeDtypeStruct((n, Q_SLOTS), BF16),
                      jax.ShapeDtypeStruct((n // (tm * seq_tiles), Q_SLOTS, tm * seq_tiles), BF16)]
    return pl.pallas_call(
        functools.partial(_kvrows_kernel, with_kv=with_kv),
        grid=(n // tm,),
        in_specs=in_specs, out_specs=out_specs, out_shape=out_shape,
        compiler_params=_params(1),
        name="kvrows",
    )(*args)


def _qproj_kernel(x_ref, g_ref, w_in_ref, gql_ref, wm_ref, wr_ref, ga_ref, gb_ref,
                  cos_ref, sin_ref, *rest, absorb):
    if absorb:
        wabs_ref, qabs_ref, qm_ref = rest
    else:
        qf_ref, qm_ref = rest
    x = x_ref[...]
    h = _rms(x, g_ref[...]).astype(BF16)
    proj = _dot(h, w_in_ref[...])
    qm_ref[...] = proj[:, Q_LORA:]
    cq = _rms(proj[:, :Q_LORA], gql_ref[...]).astype(BF16)
    q_main = _dot(cq, wm_ref[...])
    q_rot = _dot(cq, wr_ref[...])
    a_tab = ga_ref[...] * cos_ref[...]
    b_tab = gb_ref[...] * sin_ref[...]
    for hd in range(MLA_HEADS):
        sl = slice(hd * SLOT, (hd + 1) * SLOT)
        qh = q_main[:, sl]
        inv = lax.rsqrt(jnp.sum(qh * qh, axis=-1, keepdims=True) * (1.0 / QK_DIM) + EPS)
        qf = ((qh * a_tab + q_rot[:, sl] * b_tab) * (inv * (QK_DIM ** -0.5 * LOG2E))).astype(BF16)
        if absorb:
            qabs_ref[hd] = _dot(qf, wabs_ref[hd])
        else:
            qf_ref[:, sl] = qf


def _qproj(x, wl, wk, cos_t, sin_t, *, tm, seq_tiles, absorb):
    n = x.shape[0]
    const = lambda i: (0, 0)
    in_specs = [
        pl.BlockSpec((tm, D_MODEL), lambda i: (i, 0)),
        pl.BlockSpec((1, D_MODEL), const),
        pl.BlockSpec((D_MODEL, Q_LORA + MEM_W), const),
        pl.BlockSpec((1, Q_LORA), const),
        pl.BlockSpec((Q_LORA, Q_SLOTS), const),
        pl.BlockSpec((Q_LORA, Q_SLOTS), const),
        pl.BlockSpec((1, SLOT), const),
        pl.BlockSpec((1, SLOT), const),
        pl.BlockSpec((tm, SLOT), lambda i: (i % seq_tiles, 0)),
        pl.BlockSpec((tm, SLOT), lambda i: (i % seq_tiles, 0)),
    ]
    args = [x, wl['g_mix'], wl['w_in'], wl['g_q_lora'], wl['w_uq_main'], wl['w_uq_rot'],
            wl['g_a'], wl['g_b'], cos_t, sin_t]
    if absorb:
        in_specs.append(pl.BlockSpec((MLA_HEADS, SLOT, KV_ROW), lambda i: (0, 0, 0)))
        args.append(wk['w_abs'])
        out_specs = [pl.BlockSpec((MLA_HEADS, tm, KV_ROW), lambda i: (0, i, 0))]
        out_shape = [jax.ShapeDtypeStruct((MLA_HEADS, n, KV_ROW), F32)]
    else:
        out_specs = [pl.BlockSpec((tm, Q_SLOTS), lambda i: (i, 0))]
        out_shape = [jax.ShapeDtypeStruct((n, Q_SLOTS), BF16)]
    out_specs.append(pl.BlockSpec((tm, MEM_W), lambda i: (i, 0)))
    out_shape.append(jax.ShapeDtypeStruct((n, MEM_W), F32))
    return pl.pallas_call(
        functools.partial(_qproj_kernel, absorb=absorb),
        grid=(n // tm,),
        in_specs=in_specs, out_specs=out_specs, out_shape=out_shape,
        compiler_params=_params(1),
        name="qproj",
    )(*args)


def _mla_prompt_kernel(q_ref, k_ref, vt_ref, o_ref, *, tq, nh):
    qi = pl.program_id(2)
    q = q_ref[0]
    q_idx = qi * tq + lax.broadcasted_iota(jnp.int32, (1, tq), 1)
    heads = range(nh)
    slots = [slice(hd * SLOT, (hd + 1) * SLOT) for hd in heads]

    def block(kb, carry, masked):
        start = pl.multiple_of(kb * tq, tq)
        k = k_ref[0, pl.ds(start, tq), :]
        vt = vt_ref[0, :, pl.ds(start, tq)]
        sts = [_dot_nt(k[:, sl], q[:, sl]) for sl in slots]
        if masked:
            k_idx = start + lax.broadcasted_iota(jnp.int32, (tq, 1), 0)
            sts = [jnp.where(k_idx <= q_idx, st, NEG) for st in sts]
        ms = [jnp.maximum(carry[2 * hd], jnp.max(sts[hd], axis=0, keepdims=True))
              for hd in heads]
        ps = [jnp.exp2(sts[hd] - ms[hd]).astype(BF16) for hd in heads]
        pvs = [_dot(vt[slots[hd], :], ps[hd]) for hd in heads]
        new = []
        for hd in heads:
            alpha = jnp.exp2(carry[2 * hd] - ms[hd])
            new += [ms[hd], carry[2 * hd + 1] * alpha + pvs[hd]]
        return tuple(new)

    init = (jnp.full((1, tq), NEG, F32), jnp.zeros((SLOT, tq), F32)) * nh
    carry = lax.fori_loop(0, qi, lambda kb, c: block(kb, c, False), init)
    carry = block(qi, carry, True)
    outs = [carry[2 * hd + 1][:V_HEAD] / carry[2 * hd + 1][V_HEAD:V_HEAD + 1] for hd in heads]
    o_ref[0] = jnp.concatenate(outs, axis=0).T.astype(BF16)


def _mla_prompt(qf, kn, vt, *, tq, nh):
    b, s, _ = qf.shape
    return pl.pallas_call(
        functools.partial(_mla_prompt_kernel, tq=tq, nh=nh),
        grid=(b, MLA_HEADS // nh, s // tq),
        in_specs=[
            pl.BlockSpec((1, tq, nh * SLOT), lambda i, hg, qi: (i, qi, hg)),
            pl.BlockSpec((1, s, nh * SLOT), lambda i, hg, qi: (i, 0, hg)),
            pl.BlockSpec((1, nh * SLOT, s), lambda i, hg, qi: (i, hg, 0)),
        ],
        out_specs=pl.BlockSpec((1, tq, nh * V_HEAD), lambda i, hg, qi: (i, qi, hg)),
        out_shape=jax.ShapeDtypeStruct((b, s, MLA_HEADS * V_HEAD), BF16),
        compiler_params=_params(3),
        name="mla_prompt",
    )(qf, kn, vt)


Q_ROWS = MLA_HEADS * 8
N_KN = MLA_HEADS * QK_NOPE


def _mla_sample_kernel(pt_ref, qabs_ref, wukt_ref, kvnew_ref, *rest, pps, n_tok, first_layer):
    page_refs = rest[:pps]
    rest = rest[pps:]
    if first_layer:
        o_ref, kinv_out_ref, m_ref, l_ref, acc_ref, lhs_ref = rest
    else:
        kinv_in_ref, o_ref, m_ref, l_ref, acc_ref, lhs_ref = rest
    step = pl.program_id(1)
    qabs = qabs_ref[...].reshape(Q_ROWS, KV_ROW)
    q_rope = qabs[:, KV_LORA:].astype(BF16)

    @pl.when(step == 0)
    def _():
        m_ref[...] = jnp.full(m_ref.shape, NEG, F32)
        l_ref[...] = jnp.zeros(l_ref.shape, F32)
        acc_ref[...] = jnp.zeros(acc_ref.shape, F32)
        lhs_ref[0:N_KN, :] = wukt_ref[...]
        lhs_ref[N_KN:, :] = qabs[:, :KV_LORA].astype(BF16)

    def attend(kvt, kinv, causal):
        t = kvt.shape[1]
        ct = kvt[:KV_LORA].astype(BF16)
        krt = kvt[KV_LORA:]
        if kinv is None:
            r1 = _dot(lhs_ref[...], ct)
            kn = r1[:N_KN]
            ss = jnp.sum((kn * kn).reshape(MLA_HEADS, QK_NOPE, t), axis=1)
            krsq = jnp.sum(krt * krt, axis=0, keepdims=True)
            kinv = lax.rsqrt((ss + krsq) * (1.0 / QK_DIM) + EPS)
            s_lat = r1[N_KN:]
        else:
            s_lat = _dot(lhs_ref[N_KN:, :], ct)
        sc = s_lat + _dot(q_rope, krt.astype(BF16))
        sc = sc.reshape(MLA_HEADS, n_tok, t) * kinv[:, None, :]
        if causal:
            tok = lax.broadcasted_iota(jnp.int32, (1, n_tok, t), 1)
            key = lax.broadcasted_iota(jnp.int32, (1, n_tok, t), 2)
            sc = jnp.where(key <= tok, sc, NEG)
        sc = sc.reshape(Q_ROWS, t)
        m_prev = m_ref[...]
        m_new = jnp.maximum(m_prev, jnp.max(sc, axis=-1, keepdims=True))
        alpha = jnp.exp2(m_prev - m_new)
        p = jnp.exp2(sc - m_new)
        l_ref[...] = alpha * l_ref[...] + jnp.sum(p, axis=-1, keepdims=True)
        acc_ref[...] = alpha * acc_ref[...] + _dot_nt(p.astype(BF16), ct)
        m_ref[...] = m_new
        return kinv

    kvt = jnp.concatenate([r[0] for r in page_refs], axis=1)
    if first_layer:
        kinv_out_ref[0] = attend(kvt, None, causal=False)
    else:
        attend(kvt, kinv_in_ref[0], causal=False)

    @pl.when(step == pl.num_programs(1) - 1)
    def _():
        attend(kvnew_ref[0], None, causal=True)
        o = acc_ref[...] / l_ref[...]
        for hd in range(MLA_HEADS):
            o_ref[0, :, hd * KV_LORA:(hd + 1) * KV_LORA] = o[hd * n_tok:(hd + 1) * n_tok, :]


def _mla_sample(page_table, qabs, w_ukt, kv_new_t, cache_t, kinv, *, pps):
    n_seq, n_pages = page_table.shape
    n_tok = qabs.shape[1] // n_seq
    page = cache_t.shape[2]
    first_layer = kinv is None
    t_step = pps * page
    assert n_tok == 8 and n_pages % pps == 0

    def page_spec(i):
        return pl.BlockSpec((1, KV_ROW, page), lambda b, s, pt: (pt[b, s * pps + i], 0, 0))

    in_specs = [
        pl.BlockSpec((MLA_HEADS, 1, n_tok, KV_ROW), lambda b, s, pt: (0, b, 0, 0)),
        pl.BlockSpec((N_KN, KV_LORA), lambda b, s, pt: (0, 0)),
        pl.BlockSpec((1, KV_ROW, LANE), lambda b, s, pt: (b, 0, 0)),
    ] + [page_spec(i) for i in range(pps)]
    args = [qabs.reshape(MLA_HEADS, n_seq, n_tok, KV_ROW), w_ukt, kv_new_t] + [cache_t] * pps
    out_specs = [pl.BlockSpec((1, n_tok, MLA_HEADS * KV_LORA), lambda b, s, pt: (b, 0, 0))]
    out_shape = [jax.ShapeDtypeStruct((n_seq, n_tok, MLA_HEADS * KV_LORA), F32)]
    kinv_spec = pl.BlockSpec((1, MLA_HEADS, t_step), lambda b, s, pt: (b, 0, s))
    if first_layer:
        out_specs.append(kinv_spec)
        out_shape.append(jax.ShapeDtypeStruct((n_seq, MLA_HEADS, n_pages * page), F32))
    else:
        in_specs.append(kinv_spec)
        args.append(kinv)
    grid_spec = pltpu.PrefetchScalarGridSpec(
        num_scalar_prefetch=1,
        grid=(n_seq, n_pages // pps),
        in_specs=in_specs,
        out_specs=out_specs,
        scratch_shapes=[
            pltpu.VMEM((Q_ROWS, 1), F32),
            pltpu.VMEM((Q_ROWS, 1), F32),
            pltpu.VMEM((Q_ROWS, KV_LORA), F32),
            pltpu.VMEM((N_KN + Q_ROWS, KV_LORA), BF16),
        ],
    )
    return pl.pallas_call(
        functools.partial(_mla_sample_kernel, pps=pps, n_tok=n_tok, first_layer=first_layer),
        grid_spec=grid_spec,
        out_shape=out_shape,
        compiler_params=_params(2),
        name="mla_sample",
    )(page_table, *args)


def _out_b_kernel(x_ref, tok_ref, qm_ref, mk_ref, mv_ref, gq_ref, gsum_ref, w_out_ref,
                  *rest, nb, ts, latent):
    if latent:
        wuv_ref, o_ref = rest
        tok = _dot(tok_ref[...].astype(BF16), wuv_ref[...]).astype(BF16)
    else:
        (o_ref,) = rest
        tok = tok_ref[...]
    mo = _mem_attend(qm_ref[...], mk_ref, mv_ref, gq_ref[...], gsum_ref[...], nb, ts)
    y = _dot(tok, w_out_ref[0:POOL_W, :]) + _dot(mo.astype(BF16), w_out_ref[POOL_W:, :])
    o_ref[...] = x_ref[...] + y


def _out_b(x, tok, qm, mk, mv, wl, wk, *, nb, ts, latent):
    n = x.shape[0]
    rows = nb * ts
    tok_w = tok.shape[1]
    const = lambda i: (0, 0)
    seq_len = n // mk.shape[0]
    mem_idx = lambda i: (i * rows // (seq_len * nb), 0, 0)
    in_specs = [
        pl.BlockSpec((rows, D_MODEL), lambda i: (i, 0)),
        pl.BlockSpec((rows, tok_w), lambda i: (i, 0)),
        pl.BlockSpec((rows, MEM_W), lambda i: (i, 0)),
        pl.BlockSpec((nb, MEM_W, N_MEM), mem_idx),
        pl.BlockSpec((nb, MEM_W, N_MEM), mem_idx),
        pl.BlockSpec((1, MEM_W), const),
        pl.BlockSpec((MEM_W, MEM_W), const),
        pl.BlockSpec((D_MODEL, D_MODEL), const),
    ]
    args = [x, tok, qm, mk, mv, wl['g_mem_q'], wl['gsum'], wl['w_out']]
    if latent:
        in_specs.append(pl.BlockSpec((MLA_HEADS * KV_LORA, MLA_HEADS * V_HEAD), const))
        args.append(wk['w_uv_bd'])
    return pl.pallas_call(
        functools.partial(_out_b_kernel, nb=nb, ts=ts, latent=latent),
        grid=(n // rows,),
        in_specs=in_specs,
        out_specs=pl.BlockSpec((rows, D_MODEL), lambda i: (i, 0)),
        out_shape=jax.ShapeDtypeStruct((n, D_MODEL), F32),
        compiler_params=_params(1),
        name="out_b",
    )(*args)


def _rope_tables(pos):
    half = QK_ROPE // 2
    inv_freq = ROPE_BASE ** (-jnp.arange(half, dtype=F32) / half)
    ang = pos.astype(F32)[:, None] * inv_freq[None, :]
    cos32 = jnp.concatenate([jnp.cos(ang)] * 2, axis=-1)
    sin32 = jnp.concatenate([jnp.sin(ang)] * 2, axis=-1)
    n = pos.shape[0]
    cos_t = jnp.concatenate([jnp.ones((n, QK_NOPE), F32), cos32,
                             jnp.zeros((n, SLOT - QK_DIM), F32)], axis=-1)
    sin_t = jnp.concatenate([jnp.zeros((n, QK_NOPE), F32), sin32,
                             jnp.zeros((n, SLOT - QK_DIM), F32)], axis=-1)
    return cos32, sin32, cos_t, sin_t


def _rot_cols(w):
    half = QK_ROPE // 2
    return jnp.concatenate([-w[..., half:], w[..., :half]], axis=-1)


def _prep_weights(norm_mix, norm_ffn, w_out, w_ffn_in, w_ffn_out, g_mem_q, w_in_a, w_pool_grp,
                  pool_scale, w_in_b, g_q_lora, w_uq, g_q, norm_kv, w_dkv, g_kv_lora, w_uk, w_uv,
                  g_k_nope, g_k_rope):
    half = QK_ROPE // 2
    gsum = jnp.asarray(np.kron(np.eye(MEM_HEADS), np.ones((MEM_HEAD_DIM, MEM_HEAD_DIM))), BF16)
    layers = []
    for l in range(DEPTH):
        wl = dict(
            g_mix=norm_mix[l][None], g_ffn=norm_ffn[l][None],
            w_out=w_out[l].astype(BF16),
            w_ffn_in=w_ffn_in[l].astype(BF16), w_ffn_out=w_ffn_out[l].astype(BF16),
            g_mem_q=jnp.tile(g_mem_q[l], MEM_HEADS)[None], gsum=gsum)
        if l < N_A_LAYERS:
            wp = jnp.zeros((POOL_W, POOL_W), F32)
            for g in range(len(POOL_WINDOWS)):
                sl = slice(g * POOL_GROUP_W, (g + 1) * POOL_GROUP_W)
                wp = wp.at[sl, sl].set(w_pool_grp[l, g])
            wl.update(w_in=w_in_a[l].astype(BF16), w_pool=wp.astype(BF16),
                      pool_scale=pool_scale[l][None])
        else:
            j = l - N_A_LAYERS
            wq = w_uq[j].reshape(Q_LORA, MLA_HEADS, QK_DIM)
            zpad = jnp.zeros((Q_LORA, MLA_HEADS, SLOT - QK_DIM), F32)
            w_main = jnp.concatenate([wq, zpad], axis=-1).reshape(Q_LORA, Q_SLOTS)
            w_rot = jnp.concatenate([jnp.zeros((Q_LORA, MLA_HEADS, QK_NOPE), F32),
                                     _rot_cols(wq[..., QK_NOPE:]), zpad],
                                    axis=-1).reshape(Q_LORA, Q_SLOTS)
            gk2 = jnp.concatenate([g_k_rope, g_k_rope])
            gq = g_q[j]
            g_rot = jnp.concatenate([gq[QK_NOPE + half:], gq[QK_NOPE:QK_NOPE + half]])
            pad = jnp.zeros((SLOT - QK_DIM,), F32)
            g_a = jnp.concatenate([gq[:QK_NOPE] * g_k_nope, gq[QK_NOPE:] * gk2, pad])
            g_b = jnp.concatenate([jnp.zeros((QK_NOPE,), F32), g_rot * gk2, pad])
            wl.update(w_in=w_in_b[j].astype(BF16), g_q_lora=g_q_lora[j][None],
                      w_uq_main=w_main.astype(BF16), w_uq_rot=w_rot.astype(BF16),
                      g_a=g_a[None], g_b=g_b[None])
        layers.append(wl)

    zcol = jnp.zeros((D_MODEL, LANE - QK_ROPE), F32)
    w_dkv_ext = jnp.concatenate([w_dkv, zcol, _rot_cols(w_dkv[:, KV_LORA:]), zcol], axis=-1)
    wuk_h = w_uk.reshape(KV_LORA, MLA_HEADS, QK_NOPE)
    top = jnp.concatenate([wuk_h, jnp.zeros((KV_LORA, MLA_HEADS, SLOT - QK_NOPE), F32)], axis=-1)
    eye = jnp.eye(QK_ROPE, dtype=F32)
    bot = jnp.concatenate([jnp.zeros((QK_ROPE, QK_NOPE), F32), eye,
                           jnp.zeros((QK_ROPE, SLOT - QK_DIM), F32)], axis=-1)
    bot = jnp.broadcast_to(bot[:, None, :], (QK_ROPE, MLA_HEADS, SLOT))
    w_kslots = jnp.concatenate([top, bot], axis=0).reshape(KV_ROW, Q_SLOTS)
    w_abs = jnp.zeros((MLA_HEADS, SLOT, KV_ROW), F32)
    w_abs = w_abs.at[:, :QK_NOPE, :KV_LORA].set(jnp.transpose(wuk_h, (1, 2, 0)))
    w_abs = w_abs.at[:, QK_NOPE:QK_DIM, KV_LORA:].set(jnp.broadcast_to(eye, (MLA_HEADS,) + eye.shape))
    wuv_h = w_uv.reshape(KV_LORA, MLA_HEADS, V_HEAD)
    w_uvt_slots = jnp.concatenate(
        [jnp.transpose(wuv_h, (1, 2, 0)), jnp.zeros((MLA_HEADS, SLOT - V_HEAD, KV_LORA), F32)],
        axis=1).reshape(Q_SLOTS, KV_LORA)
    w_uv_bd = jnp.einsum('chd,hg->hcgd', wuv_h, jnp.eye(MLA_HEADS, dtype=F32))
    w_uv_bd = w_uv_bd.reshape(MLA_HEADS * KV_LORA, MLA_HEADS * V_HEAD)
    shared = dict(
        norm_kv=norm_kv[None], g_kv_lora=g_kv_lora[None],
        w_dkv_ext=w_dkv_ext.astype(BF16), w_kslots=w_kslots.astype(BF16),
        w_uvt_slots=w_uvt_slots.astype(BF16), w_abs=w_abs.astype(BF16),
        w_ukt=jnp.transpose(w_uk).astype(BF16), w_uv_bd=w_uv_bd.astype(BF16))
    return layers, shared


def _trunk(x, pos0, pool_prev, mem_k, mem_v, layers, shared, *, nb, ts, tm, past):
    b, s, _ = x.shape
    n = b * s
    pos = pos0 + jnp.arange(s)
    tabs = _rope_tables(pos)
    if s >= tm:
        tmk, seq_tiles = tm, s // tm
    else:
        reps = min(tm, n) // s
        tmk, seq_tiles = reps * s, 1
        tabs = tuple(jnp.tile(t, (reps, 1)) for t in tabs)
    cos32, sin32, cos_t, sin_t = tabs
    pool_new = []
    kv_new = None
    for l in range(DEPTH):
        wl = layers[l]
        if l < N_A_LAYERS:
            x, st = _mixer_a(x, pool_prev[l], mem_k[l], mem_v[l], wl, nb=nb, ts=ts, pos0=pos0)
            pool_new.append(st)
            xf = x.reshape(n, D_MODEL)
        else:
            xf = x.reshape(n, D_MODEL)
            if l == N_A_LAYERS:
                if past is None:
                    kv_new, kn, vt = _kvrows(xf, shared, cos32, sin32, tm=tmk,
                                             seq_tiles=seq_tiles, with_kv=True)
                else:
                    (kv_new,) = _kvrows(xf, shared, cos32, sin32, tm=tmk,
                                        seq_tiles=seq_tiles, with_kv=False)
            if past is None:
                qf, qm = _qproj(xf, wl, shared, cos_t, sin_t, tm=tmk, seq_tiles=seq_tiles,
                                absorb=False)
                tok = _mla_prompt(qf.reshape(b, s, Q_SLOTS), kn.reshape(b, s, Q_SLOTS), vt, tq=256,
                                  nh=6)
                tok = tok.reshape(n, MLA_HEADS * V_HEAD)
                nb_o, ts_o = 1, tmk
            else:
                cache_t, page_table = past
                qabs, qm = _qproj(xf, wl, shared, cos_t, sin_t, tm=tmk, seq_tiles=seq_tiles,
                                  absorb=True)
                if l == N_A_LAYERS:
                    kv_new_t = jnp.pad(jnp.swapaxes(kv_new.reshape(b, s, KV_ROW), 1, 2),
                                       ((0, 0), (0, 0), (0, LANE - s)))
                    tok, kinv = _mla_sample(page_table, qabs, shared['w_ukt'], kv_new_t, cache_t,
                                            None, pps=16)
                else:
                    (tok,) = _mla_sample(page_table, qabs, shared['w_ukt'], kv_new_t, cache_t,
                                         kinv, pps=16)
                tok = tok.reshape(n, MLA_HEADS * KV_LORA)
                nb_o, ts_o = nb, ts
            xf = _out_b(xf, tok, qm, mem_k[l], mem_v[l], wl, shared, nb=nb_o, ts=ts_o,
                        latent=past is not None)
        xf = _ffn(xf, wl['g_ffn'], wl['w_ffn_in'], wl['w_ffn_out'], tm=min(tm, n))
        x = xf.reshape(b, s, D_MODEL)
    return x, kv_new.reshape(b, s, KV_ROW), jnp.stack(pool_new)


def kernel(x_prompt, x_sample, cache_kv, state_pool, cache_mem_k, cache_mem_v, page_table, mem_prompt,
           norm_mix, norm_ffn, w_out, w_ffn_in, w_ffn_out, norm_mem, w_mem_kv, g_mem_q, g_mem_k,
           w_in_a, w_pool_grp, pool_scale, w_in_b, g_q_lora, w_uq, g_q,
           norm_kv, w_dkv, g_kv_lora, w_uk, w_uv, g_k_nope, g_k_rope):
    layers, shared = _prep_weights(norm_mix, norm_ffn, w_out, w_ffn_in, w_ffn_out, g_mem_q, w_in_a,
                                   w_pool_grp, pool_scale, w_in_b, g_q_lora, w_uq, g_q, norm_kv,
                                   w_dkv, g_kv_lora, w_uk, w_uv, g_k_nope, g_k_rope)
    b_p, s_p, _ = x_prompt.shape
    n_seq, s_s, _ = x_sample.shape

    mk_p, mv_p = _memkv(mem_prompt, norm_mem[:, None, :],
                        jnp.swapaxes(w_mem_kv, 1, 2).astype(BF16), g_mem_k[:, None, :, None])
    pool0 = jnp.zeros((N_A_LAYERS, b_p, POOL_STATE, POOL_W), F32)
    y_p, kv_p, pool_p = _trunk(x_prompt, 0, pool0, mk_p, mv_p, layers, shared,
                               nb=1, ts=256, tm=512, past=None)

    past_len = page_table.shape[1] * cache_kv.shape[1]
    mk_s = jnp.transpose(cache_mem_k, (0, 1, 3, 4, 2)).reshape(DEPTH, n_seq, MEM_W, N_MEM)
    mv_s = jnp.transpose(cache_mem_v, (0, 1, 3, 4, 2)).reshape(DEPTH, n_seq, MEM_W, N_MEM)
    cache_t = jnp.swapaxes(cache_kv, 1, 2)
    y_s, kv_s, pool_s = _trunk(x_sample, past_len, state_pool, mk_s, mv_s, layers, shared,
                               nb=n_seq // 2, ts=s_s, tm=512, past=(cache_t, page_table))

    def mem_out(a):
        a = a.reshape(DEPTH, b_p, MEM_HEADS, MEM_HEAD_DIM, N_MEM)
        return jnp.transpose(a, (0, 1, 4, 2, 3))

    return (y_p, y_s, kv_p, kv_s, pool_p, pool_s, mem_out(mk_p), mem_out(mv_p))
```

```python
import functools

import numpy as np
import jax
import jax.numpy as jnp
from jax import lax
from jax.experimental import pallas as pl
from jax.experimental.pallas import tpu as pltpu

D_MODEL = 1024
DEPTH = 4
N_A_LAYERS = 2
MEM_HEADS = 4
MEM_HEAD_DIM = 64
MEM_W = MEM_HEADS * MEM_HEAD_DIM
N_MEM = 256
POOL_W = D_MODEL - MEM_W
POOL_WINDOWS = (2, 4, 8, 16)
POOL_GROUP_W = POOL_W // len(POOL_WINDOWS)
POOL_STATE = max(POOL_WINDOWS) - 1
POOL_PAD = POOL_STATE + 1
MLA_HEADS = 12
QK_NOPE = 64
QK_ROPE = 32
QK_DIM = QK_NOPE + QK_ROPE
V_HEAD = 64
KV_LORA = 256
Q_LORA = 384
KV_ROW = KV_LORA + QK_ROPE
ROPE_BASE = 10000.0
D_FF = 2816
EPS = 1e-6
NEG = -1e30
LOG2E = 1.4426950408889634

LANE = 128
SLOT = LANE
Q_SLOTS = MLA_HEADS * SLOT
VMEM_LIMIT = 56 * 1024 * 1024

BF16 = jnp.bfloat16
F32 = jnp.float32
NT_DIMS = (((1,), (1,)), ((), ()))


def _params(n_grid, flags=None):
    return pltpu.CompilerParams(dimension_semantics=("arbitrary",) * n_grid,
                                vmem_limit_bytes=VMEM_LIMIT, flags=flags)


def _dot(a, b):
    return jnp.dot(a, b, preferred_element_type=F32)


def _dot_nt(a, b):
    return lax.dot_general(a, b, NT_DIMS, preferred_element_type=F32)


def _split_dot(x, w):
    hi = x.astype(BF16)
    lo = (x - hi.astype(F32)).astype(BF16)
    return _dot(hi, w) + _dot(lo, w)


def _rms(x, g):
    return x * lax.rsqrt(jnp.mean(x * x, axis=-1, keepdims=True) + EPS) * g


def _mem_attend(qm, mkt_ref, mvt_ref, gq, gsum, nb, ts):
    ssq = _split_dot(qm * qm, gsum)
    qn = qm * lax.rsqrt(ssq * (1.0 / MEM_HEAD_DIM) + EPS) * (gq * MEM_HEAD_DIM ** -0.5)
    q3 = qn.reshape(nb, ts, MEM_W)
    mkt = mkt_ref[...].astype(BF16)
    mvt = mvt_ref[...].astype(BF16)
    lane = lax.broadcasted_iota(jnp.int32, (1, 1, MEM_W), 2)
    heads = range(MEM_HEADS)
    in_head = [(lane >= h * MEM_HEAD_DIM) & (lane < (h + 1) * MEM_HEAD_DIM) for h in heads]
    qs = [jnp.where(in_head[h], q3, 0.0).astype(BF16) for h in heads]
    ss = [jnp.einsum('bqd,bdm->bqm', qs[h], mkt, preferred_element_type=F32) for h in heads]
    ps = [jnp.exp(s - jnp.max(s, axis=-1, keepdims=True)) for s in ss]
    ps = [(p / jnp.sum(p, axis=-1, keepdims=True)).astype(BF16) for p in ps]
    os_ = [jnp.einsum('bqm,bdm->bqd', p, mvt, preferred_element_type=F32) for p in ps]
    out = jnp.where(in_head[0], os_[0], 0.0)
    for h in range(1, MEM_HEADS):
        out = jnp.where(in_head[h], os_[h], out)
    return out.reshape(nb * ts, MEM_W)


def _memkv_kernel(mem_ref, gnorm_ref, wt_ref, gk_ref, kt_ref, vt_ref):
    x = mem_ref[0]
    xn = x * lax.rsqrt(jnp.mean(x * x, axis=-1, keepdims=True) + EPS)
    hm = (xn * gnorm_ref[0]).astype(BF16)
    kvt = _dot_nt(wt_ref[0], hm)
    kt = kvt[:MEM_W].reshape(MEM_HEADS, MEM_HEAD_DIM, N_MEM)
    inv = lax.rsqrt(jnp.mean(kt * kt, axis=1, keepdims=True) + EPS)
    kt_ref[0, 0] = (kt * inv * gk_ref[0]).reshape(MEM_W, N_MEM)
    vt_ref[0, 0] = kvt[MEM_W:]


def _memkv(mem, norm_mem, w_mem_kv_t, g_mem_k):
    b = mem.shape[0]
    return pl.pallas_call(
        _memkv_kernel,
        grid=(DEPTH, b),
        in_specs=[
            pl.BlockSpec((1, N_MEM, D_MODEL), lambda l, i: (i, 0, 0)),
            pl.BlockSpec((1, 1, D_MODEL), lambda l, i: (l, 0, 0)),
            pl.BlockSpec((1, 2 * MEM_W, D_MODEL), lambda l, i: (l, 0, 0)),
            pl.BlockSpec((1, 1, MEM_HEAD_DIM, 1), lambda l, i: (l, 0, 0, 0)),
        ],
        out_specs=[pl.BlockSpec((1, 1, MEM_W, N_MEM), lambda l, i: (l, i, 0, 0)),
                   pl.BlockSpec((1, 1, MEM_W, N_MEM), lambda l, i: (l, i, 0, 0))],
        out_shape=[jax.ShapeDtypeStruct((DEPTH, b, MEM_W, N_MEM), F32)] * 2,
        compiler_params=_params(2),
        name="memkv",
    )(mem, norm_mem, w_mem_kv_t, g_mem_k)


def _mixer_a_kernel(x_ref, g_ref, w_in_ref, prev_ref, wpool_ref, pscale_ref,
                    mk_ref, mv_ref, gq_ref, gsum_ref, w_out_ref,
                    xo_ref, pool_ref, ext_ref, *, nb, ts, pos0):
    j = pl.program_id(1)
    rows = nb * ts
    x = x_ref[...].reshape(rows, D_MODEL)
    h = _rms(x, g_ref[...]).astype(BF16)
    proj = _dot(h, w_in_ref[...])
    u = proj[:, :POOL_W]
    qm = proj[:, POOL_W:]

    @pl.when(j == 0)
    def _():
        ext_ref[:, 0:POOL_PAD, :] = jnp.zeros((nb, POOL_PAD, POOL_W), F32)
        ext_ref[:, 1:POOL_PAD, :] = prev_ref[...]

    @pl.when(j > 0)
    def _():
        ext_ref[:, 0:POOL_PAD, :] = ext_ref[:, ts:ts + POOL_PAD, :]

    ext_ref[:, POOL_PAD:, :] = u.reshape(nb, ts, POOL_W)
    e = ext_ref[...].reshape(nb * (POOL_PAD + ts), POOL_W)
    s2 = e + pltpu.roll(e, 1, 0)
    s4 = s2 + pltpu.roll(s2, 2, 0)
    s8 = s4 + pltpu.roll(s4, 4, 0)
    s16 = s8 + pltpu.roll(s8, 8, 0)

    def tile_rows(a):
        return a.reshape(nb, POOL_PAD + ts, POOL_W)[:, POOL_PAD:, :]

    t_idx = lax.broadcasted_iota(jnp.int32, (1, ts, 1), 1)
    pos1 = (pos0 + 1 + j * ts + t_idx).astype(F32)
    col = lax.broadcasted_iota(jnp.int32, (1, 1, POOL_W), 2)
    pooled = None
    for g, (w, sw) in enumerate(zip(POOL_WINDOWS, (s2, s4, s8, s16))):
        val = tile_rows(sw) / jnp.minimum(float(w), pos1)
        if pooled is None:
            pooled = val
        else:
            pooled = jnp.where(col >= g * POOL_GROUP_W, val, pooled)
    diff = (pooled.reshape(rows, POOL_W) - u).astype(BF16)
    mixed = _dot(diff, wpool_ref[...]) * pscale_ref[...]

    mo = _mem_attend(qm, mk_ref, mv_ref, gq_ref[...], gsum_ref[...], nb, ts)

    y = _dot(mixed.astype(BF16), w_out_ref[0:POOL_W, :])
    y = y + _dot(mo.astype(BF16), w_out_ref[POOL_W:, :])
    xo_ref[...] = (x + y).reshape(nb, ts, D_MODEL)

    @pl.when(j == pl.num_programs(1) - 1)
    def _():
        pool_ref[...] = ext_ref[:, ts + 1:ts + POOL_PAD, :]


def _mixer_a(x, prev, mk, mv, wl, *, nb, ts, pos0):
    b, s, _ = x.shape
    kern = functools.partial(_mixer_a_kernel, nb=nb, ts=ts, pos0=pos0)
    const = lambda i, j: (0, 0)
    return pl.pallas_call(
        kern,
        grid=(b // nb, s // ts),
        in_specs=[
            pl.BlockSpec((nb, ts, D_MODEL), lambda i, j: (i, j, 0)),
            pl.BlockSpec((1, D_MODEL), const),
            pl.BlockSpec((D_MODEL, D_MODEL), const),
            pl.BlockSpec((nb, POOL_STATE, POOL_W), lambda i, j: (i, 0, 0)),
            pl.BlockSpec((POOL_W, POOL_W), const),
            pl.BlockSpec((1, POOL_W), const),
            pl.BlockSpec((nb, MEM_W, N_MEM), lambda i, j: (i, 0, 0)),
            pl.BlockSpec((nb, MEM_W, N_MEM), lambda i, j: (i, 0, 0)),
            pl.BlockSpec((1, MEM_W), const),
            pl.BlockSpec((MEM_W, MEM_W), const),
            pl.BlockSpec((D_MODEL, D_MODEL), const),
        ],
        out_specs=[pl.BlockSpec((nb, ts, D_MODEL), lambda i, j: (i, j, 0)),
                   pl.BlockSpec((nb, POOL_STATE, POOL_W), lambda i, j: (i, 0, 0))],
        out_shape=[jax.ShapeDtypeStruct((b, s, D_MODEL), F32),
                   jax.ShapeDtypeStruct((b, POOL_STATE, POOL_W), F32)],
        scratch_shapes=[pltpu.VMEM((nb, POOL_PAD + ts, POOL_W), F32)],
        compiler_params=_params(2),
        name="mixer_a",
    )(x, wl['g_mix'], wl['w_in'], prev, wl['w_pool'], wl['pool_scale'],
      mk, mv, wl['g_mem_q'], wl['gsum'], wl['w_out'])


FFN_CHUNK = 1408


def _ffn_kernel(x_ref, g_ref, w_in_ref, w_out_ref, o_ref):
    x = x_ref[...]
    h = _rms(x, g_ref[...]).astype(BF16)
    acc = x
    for c0 in range(0, D_FF, FFN_CHUNK):
        gate = _dot(h, w_in_ref[:, c0:c0 + FFN_CHUNK])
        up = _dot(h, w_in_ref[:, D_FF + c0:D_FF + c0 + FFN_CHUNK])
        a = (gate * jax.nn.sigmoid(gate) * up).astype(BF16)
        acc = acc + _dot(a, w_out_ref[c0:c0 + FFN_CHUNK, :])
    o_ref[...] = acc


def _ffn(x, g, w_in, w_out, *, tm):
    n = x.shape[0]
    const = lambda i: (0, 0)
    return pl.pallas_call(
        _ffn_kernel,
        grid=(n // tm,),
        in_specs=[
            pl.BlockSpec((tm, D_MODEL), lambda i: (i, 0)),
            pl.BlockSpec((1, D_MODEL), const),
            pl.BlockSpec((D_MODEL, 2 * D_FF), const, pipeline_mode=pl.Buffered(1)),
            pl.BlockSpec((D_FF, D_MODEL), const, pipeline_mode=pl.Buffered(1)),
        ],
        out_specs=pl.BlockSpec((tm, D_MODEL), lambda i: (i, 0)),
        out_shape=jax.ShapeDtypeStruct((n, D_MODEL), F32),
        compiler_params=_params(1),
        name="ffn",
    )(x, g, w_in, w_out)


def _kvrows_kernel(x_ref, g_ref, w_ref, gl_ref, cos_ref, sin_ref, *rest, with_kv):
    if with_kv:
        wk_ref, wv_ref, kv_ref, kn_ref, v_ref = rest
    else:
        (kv_ref,) = rest
    x = x_ref[...]
    h = _rms(x, g_ref[...]).astype(BF16)
    ck = _dot(h, w_ref[...])
    c = _rms(ck[:, :KV_LORA], gl_ref[...])
    kr = (ck[:, KV_LORA:KV_ROW] * cos_ref[...]
          + ck[:, KV_LORA + LANE:KV_LORA + LANE + QK_ROPE] * sin_ref[...])
    kv_ref[:, 0:KV_LORA] = c
    kv_ref[:, KV_LORA:KV_ROW] = kr
    if with_kv:
        cb = c.astype(BF16)
        kn = _dot(cb, wk_ref[0:KV_LORA, :]) + _dot(kr.astype(BF16), wk_ref[KV_LORA:KV_ROW, :])
        for hd in range(MLA_HEADS):
            sl = slice(hd * SLOT, (hd + 1) * SLOT)
            kh = kn[:, sl]
            inv = lax.rsqrt(jnp.sum(kh * kh, axis=-1, keepdims=True) * (1.0 / QK_DIM) + EPS)
            kn_ref[:, sl] = (kh * inv).astype(BF16)
        vt = _dot_nt(wv_ref[...], cb)
        row = lax.broadcasted_iota(jnp.int32, (Q_SLOTS, 1), 0)
        v_ref[0] = jnp.where(row % SLOT == V_HEAD, 1.0, vt).astype(BF16)


def _kvrows(x, wk, cos32, sin32, *, tm, seq_tiles, with_kv):
    n = x.shape[0]
    const = lambda i: (0, 0)
    in_specs = [
        pl.BlockSpec((tm, D_MODEL), lambda i: (i, 0)),
        pl.BlockSpec((1, D_MODEL), const),
        pl.BlockSpec((D_MODEL, 4 * LANE), const),
        pl.BlockSpec((1, KV_LORA), const),
        pl.BlockSpec((tm, QK_ROPE), lambda i: (i % seq_tiles, 0)),
        pl.BlockSpec((tm, QK_ROPE), lambda i: (i % seq_tiles, 0)),
    ]
    args = [x, wk['norm_kv'], wk['w_dkv_ext'], wk['g_kv_lora'], cos32, sin32]
    out_specs = [pl.BlockSpec((tm, KV_ROW), lambda i: (i, 0))]
    out_shape = [jax.ShapeDtypeStruct((n, KV_ROW), F32)]
    if with_kv:
        in_specs += [pl.BlockSpec((KV_ROW, Q_SLOTS), const),
                     pl.BlockSpec((Q_SLOTS, KV_LORA), const)]
        args += [wk['w_kslots'], wk['w_uvt_slots']]
        out_specs += [pl.BlockSpec((tm, Q_SLOTS), lambda i: (i, 0)),
                      pl.BlockSpec((1, Q_SLOTS, tm), lambda i: (i // seq_tiles, 0, i % seq_tiles))]
        out_shape += [jax.ShapeDtypeStruct((n, Q_SLOTS), BF16),
                      jax.ShapeDtypeStruct((n // (tm * seq_tiles), Q_SLOTS, tm * seq_tiles), BF16)]
    return pl.pallas_call(
        functools.partial(_kvrows_kernel, with_kv=with_kv),
        grid=(n // tm,),
        in_specs=in_specs, out_specs=out_specs, out_shape=out_shape,
        compiler_params=_params(1),
        name="kvrows",
    )(*args)


def _qproj_kernel(x_ref, g_ref, w_in_ref, gql_ref, wm_ref, wr_ref, ga_ref, gb_ref,
                  cos_ref, sin_ref, *rest, absorb):
    if absorb:
        wabs_ref, qabs_ref, qm_ref = rest
    else:
        qf_ref, qm_ref = rest
    x = x_ref[...]
    h = _rms(x, g_ref[...]).astype(BF16)
    proj = _dot(h, w_in_ref[...])
    qm_ref[...] = proj[:, Q_LORA:]
    cq = _rms(proj[:, :Q_LORA], gql_ref[...]).astype(BF16)
    q_main = _dot(cq, wm_ref[...])
    q_rot = _dot(cq, wr_ref[...])
    a_tab = ga_ref[...] * cos_ref[...]
    b_tab = gb_ref[...] * sin_ref[...]
    for hd in range(MLA_HEADS):
        sl = slice(hd * SLOT, (hd + 1) * SLOT)
        qh = q_main[:, sl]
        inv = lax.rsqrt(jnp.sum(qh * qh, axis=-1, keepdims=True) * (1.0 / QK_DIM) + EPS)
        qf = ((qh * a_tab + q_rot[:, sl] * b_tab) * (inv * (QK_DIM ** -0.5 * LOG2E))).astype(BF16)
        if absorb:
            qabs_ref[hd] = _dot(qf, wabs_ref[hd])
        else:
            qf_ref[:, sl] = qf


def _qproj(x, wl, wk, cos_t, sin_t, *, tm, seq_tiles, absorb):
    n = x.shape[0]
    const = lambda i: (0, 0)
    in_specs = [
        pl.BlockSpec((tm, D_MODEL), lambda i: (i, 0)),
        pl.BlockSpec((1, D_MODEL), const),
        pl.BlockSpec((D_MODEL, Q_LORA + MEM_W), const),
        pl.BlockSpec((1, Q_LORA), const),
        pl.BlockSpec((Q_LORA, Q_SLOTS), const),
        pl.BlockSpec((Q_LORA, Q_SLOTS), const),
        pl.BlockSpec((1, SLOT), const),
        pl.BlockSpec((1, SLOT), const),
        pl.BlockSpec((tm, SLOT), lambda i: (i % seq_tiles, 0)),
        pl.BlockSpec((tm, SLOT), lambda i: (i % seq_tiles, 0)),
    ]
    args = [x, wl['g_mix'], wl['w_in'], wl['g_q_lora'], wl['w_uq_main'], wl['w_uq_rot'],
            wl['g_a'], wl['g_b'], cos_t, sin_t]
    if absorb:
        in_specs.append(pl.BlockSpec((MLA_HEADS, SLOT, KV_ROW), lambda i: (0, 0, 0)))
        args.append(wk['w_abs'])
        out_specs = [pl.BlockSpec((MLA_HEADS, tm, KV_ROW), lambda i: (0, i, 0))]
        out_shape = [jax.ShapeDtypeStruct((MLA_HEADS, n, KV_ROW), F32)]
    else:
        out_specs = [pl.BlockSpec((tm, Q_SLOTS), lambda i: (i, 0))]
        out_shape = [jax.ShapeDtypeStruct((n, Q_SLOTS), BF16)]
    out_specs.append(pl.BlockSpec((tm, MEM_W), lambda i: (i, 0)))
    out_shape.append(jax.ShapeDtypeStruct((n, MEM_W), F32))
    return pl.pallas_call(
        functools.partial(_qproj_kernel, absorb=absorb),
        grid=(n // tm,),
        in_specs=in_specs, out_specs=out_specs, out_shape=out_shape,
        compiler_params=_params(1),
        name="qproj",
    )(*args)


def _mla_prompt_kernel(q_ref, k_ref, vt_ref, o_ref, *, tq, nh):
    qi = pl.program_id(2)
    q = q_ref[0]
    q_idx = qi * tq + lax.broadcasted_iota(jnp.int32, (1, tq), 1)
    heads = range(nh)
    slots = [slice(hd * SLOT, (hd + 1) * SLOT) for hd in heads]

    def block(kb, carry, masked):
        start = pl.multiple_of(kb * tq, tq)
        k = k_ref[0, pl.ds(start, tq), :]
        vt = vt_ref[0, :, pl.ds(start, tq)]
        sts = [_dot_nt(k[:, sl], q[:, sl]) for sl in slots]
        if masked:
            k_idx = start + lax.broadcasted_iota(jnp.int32, (tq, 1), 0)
            sts = [jnp.where(k_idx <= q_idx, st, NEG) for st in sts]
        ms = [jnp.maximum(carry[2 * hd], jnp.max(sts[hd], axis=0, keepdims=True))
              for hd in heads]
        ps = [jnp.exp2(sts[hd] - ms[hd]).astype(BF16) for hd in heads]
        pvs = [_dot(vt[slots[hd], :], ps[hd]) for hd in heads]
        new = []
        for hd in heads:
            alpha = jnp.exp2(carry[2 * hd] - ms[hd])
            new += [ms[hd], carry[2 * hd + 1] * alpha + pvs[hd]]
        return tuple(new)

    init = (jnp.full((1, tq), NEG, F32), jnp.zeros((SLOT, tq), F32)) * nh
    carry = lax.fori_loop(0, qi, lambda kb, c: block(kb, c, False), init)
    carry = block(qi, carry, True)
    outs = [carry[2 * hd + 1][:V_HEAD] / carry[2 * hd + 1][V_HEAD:V_HEAD + 1] for hd in heads]
    o_ref[0] = jnp.concatenate(outs, axis=0).T.astype(BF16)


def _mla_prompt(qf, kn, vt, *, tq, nh):
    b, s, _ = qf.shape
    return pl.pallas_call(
        functools.partial(_mla_prompt_kernel, tq=tq, nh=nh),
        grid=(b, MLA_HEADS // nh, s // tq),
        in_specs=[
            pl.BlockSpec((1, tq, nh * SLOT), lambda i, hg, qi: (i, qi, hg)),
            pl.BlockSpec((1, s, nh * SLOT), lambda i, hg, qi: (i, 0, hg)),
            pl.BlockSpec((1, nh * SLOT, s), lambda i, hg, qi: (i, hg, 0)),
        ],
        out_specs=pl.BlockSpec((1, tq, nh * V_HEAD), lambda i, hg, qi: (i, qi, hg)),
        out_shape=jax.ShapeDtypeStruct((b, s, MLA_HEADS * V_HEAD), BF16),
        compiler_params=_params(3),
        name="mla_prompt",
    )(qf, kn, vt)


Q_ROWS = MLA_HEADS * 8
N_KN = MLA_HEADS * QK_NOPE
SAMPLE_SUB_PAGES = 4


def _mla_sample_kernel(pt_ref, qabs_ref, wukt_ref, kvnew_ref, *rest, pps, sub_pages, n_tok,
                       first_layer):
    page_refs = rest[:pps]
    rest = rest[pps:]
    if first_layer:
        o_ref, kinv_out_ref, m_ref, l_ref, acc_ref, lhs_ref = rest
    else:
        kinv_in_ref, o_ref, m_ref, l_ref, acc_ref, lhs_ref = rest
    step = pl.program_id(1)
    qabs = qabs_ref[...].reshape(Q_ROWS, KV_ROW)
    q_rope = qabs[:, KV_LORA:].astype(BF16)

    @pl.when(step == 0)
    def _():
        m_ref[...] = jnp.full(m_ref.shape, NEG, F32)
        l_ref[...] = jnp.zeros(l_ref.shape, F32)
        acc_ref[...] = jnp.zeros(acc_ref.shape, F32)
        lhs_ref[0:N_KN, :] = wukt_ref[...]
        lhs_ref[N_KN:, :] = qabs[:, :KV_LORA].astype(BF16)

    def scores(ct, krt, kinv, causal):
        t = ct.shape[1]
        if kinv is None:
            r1 = _dot(lhs_ref[...], ct)
            kn = r1[:N_KN]
            ss = jnp.sum((kn * kn).reshape(MLA_HEADS, QK_NOPE, t), axis=1)
            krsq = jnp.sum(krt * krt, axis=0, keepdims=True)
            kinv = lax.rsqrt((ss + krsq) * (1.0 / QK_DIM) + EPS)
            s_lat = r1[N_KN:]
        else:
            s_lat = _dot(lhs_ref[N_KN:, :], ct)
        sc = s_lat + _dot(q_rope, krt.astype(BF16))
        sc = sc.reshape(MLA_HEADS, n_tok, t) * kinv[:, None, :]
        if causal:
            tok = lax.broadcasted_iota(jnp.int32, (1, n_tok, t), 1)
            key = lax.broadcasted_iota(jnp.int32, (1, n_tok, t), 2)
            sc = jnp.where(key <= tok, sc, NEG)
        return sc.reshape(Q_ROWS, t), kinv

    def accumulate(state, sc, ct):
        m_prev, l_prev, acc = state
        m_new = jnp.maximum(m_prev, jnp.max(sc, axis=-1, keepdims=True))
        alpha = jnp.exp2(m_prev - m_new)
        p = jnp.exp2(sc - m_new)
        l_new = alpha * l_prev + jnp.sum(p, axis=-1, keepdims=True)
        return m_new, l_new, alpha * acc + _dot_nt(p.astype(BF16), ct)

    n_sub = pps // sub_pages
    t_sub = sub_pages * page_refs[0].shape[2]

    def sub_scores(j):
        refs = page_refs[j * sub_pages:(j + 1) * sub_pages]
        ct = jnp.concatenate([r[0, 0:KV_LORA, :].astype(BF16) for r in refs], axis=1)
        krt = jnp.concatenate([r[0, KV_LORA:, :] for r in refs], axis=1)
        if first_layer:
            sc, kinv = scores(ct, krt, None, False)
            kinv_out_ref[0, :, j * t_sub:(j + 1) * t_sub] = kinv
        else:
            sc, _ = scores(ct, krt, kinv_in_ref[0, :, j * t_sub:(j + 1) * t_sub], False)
        return sc, ct

    if first_layer:
        state = (m_ref[...], l_ref[...], acc_ref[...])
        pending = sub_scores(0)
        for j in range(1, n_sub):
            nxt = sub_scores(j)
            state = accumulate(state, *pending)
            pending = nxt
        m_ref[...], l_ref[...], acc_ref[...] = accumulate(state, *pending)
    else:
        subs = [sub_scores(j) for j in range(n_sub)]
        m_prev = m_ref[...]
        m_blk = functools.reduce(jnp.maximum, [sc for sc, _ in subs])
        m_new = jnp.maximum(m_prev, jnp.max(m_blk, axis=-1, keepdims=True))
        alpha = jnp.exp2(m_prev - m_new)
        ps = [jnp.exp2(sc - m_new) for sc, _ in subs]
        l_ref[...] = alpha * l_ref[...] + sum(jnp.sum(p, axis=-1, keepdims=True) for p in ps)
        pv = sum(_dot_nt(p.astype(BF16), ct) for p, (_, ct) in zip(ps, subs))
        acc_ref[...] = alpha * acc_ref[...] + pv
        m_ref[...] = m_new

    @pl.when(step == pl.num_programs(1) - 1)
    def _():
        kvt = kvnew_ref[0]
        ct = kvt[:KV_LORA].astype(BF16)
        sc, _ = scores(ct, kvt[KV_LORA:], None, True)
        _, l_fin, acc = accumulate((m_ref[...], l_ref[...], acc_ref[...]), sc, ct)
        o = acc / l_fin
        for hd in range(MLA_HEADS):
            o_ref[0, :, hd * KV_LORA:(hd + 1) * KV_LORA] = o[hd * n_tok:(hd + 1) * n_tok, :]


def _mla_sample(page_table, qabs, w_ukt, kv_new_t, cache_t, kinv, *, pps):
    n_seq, n_pages = page_table.shape
    n_tok = qabs.shape[1] // n_seq
    page = cache_t.shape[2]
    first_layer = kinv is None
    t_step = pps * page
    assert n_tok == 8 and n_pages % pps == 0

    def page_spec(i):
        return pl.BlockSpec((1, KV_ROW, page), lambda b, s, pt: (pt[b, s * pps + i], 0, 0))

    in_specs = [
        pl.BlockSpec((MLA_HEADS, 1, n_tok, KV_ROW), lambda b, s, pt: (0, b, 0, 0)),
        pl.BlockSpec((N_KN, KV_LORA), lambda b, s, pt: (0, 0)),
        pl.BlockSpec((1, KV_ROW, LANE), lambda b, s, pt: (b, 0, 0)),
    ] + [page_spec(i) for i in range(pps)]
    args = [qabs.reshape(MLA_HEADS, n_seq, n_tok, KV_ROW), w_ukt, kv_new_t] + [cache_t] * pps
    out_specs = [pl.BlockSpec((1, n_tok, MLA_HEADS * KV_LORA), lambda b, s, pt: (b, 0, 0))]
    out_shape = [jax.ShapeDtypeStruct((n_seq, n_tok, MLA_HEADS * KV_LORA), F32)]
    kinv_spec = pl.BlockSpec((1, MLA_HEADS, t_step), lambda b, s, pt: (b, 0, s))
    if first_layer:
        out_specs.append(kinv_spec)
        out_shape.append(jax.ShapeDtypeStruct((n_seq, MLA_HEADS, n_pages * page), F32))
    else:
        in_specs.append(kinv_spec)
        args.append(kinv)
    grid_spec = pltpu.PrefetchScalarGridSpec(
        num_scalar_prefetch=1,
        grid=(n_seq, n_pages // pps),
        in_specs=in_specs,
        out_specs=out_specs,
        scratch_shapes=[
            pltpu.VMEM((Q_ROWS, 1), F32),
            pltpu.VMEM((Q_ROWS, 1), F32),
            pltpu.VMEM((Q_ROWS, KV_LORA), F32),
            pltpu.VMEM((N_KN + Q_ROWS, KV_LORA), BF16),
        ],
    )
    return pl.pallas_call(
        functools.partial(_mla_sample_kernel, pps=pps, sub_pages=SAMPLE_SUB_PAGES, n_tok=n_tok,
                          first_layer=first_layer),
        grid_spec=grid_spec,
        out_shape=out_shape,
        compiler_params=_params(2),
        name="mla_sample",
    )(page_table, *args)


def _out_b_kernel(x_ref, tok_ref, qm_ref, mk_ref, mv_ref, gq_ref, gsum_ref, w_out_ref,
                  *rest, nb, ts, latent):
    if latent:
        wuv_ref, o_ref = rest
        tok = _dot(tok_ref[...].astype(BF16), wuv_ref[...]).astype(BF16)
    else:
        (o_ref,) = rest
        tok = tok_ref[...]
    mo = _mem_attend(qm_ref[...], mk_ref, mv_ref, gq_ref[...], gsum_ref[...], nb, ts)
    y = _dot(tok, w_out_ref[0:POOL_W, :]) + _dot(mo.astype(BF16), w_out_ref[POOL_W:, :])
    o_ref[...] = x_ref[...] + y


def _out_b(x, tok, qm, mk, mv, wl, wk, *, nb, ts, latent):
    n = x.shape[0]
    rows = nb * ts
    tok_w = tok.shape[1]
    const = lambda i: (0, 0)
    seq_len = n // mk.shape[0]
    mem_idx = lambda i: (i * rows // (seq_len * nb), 0, 0)
    in_specs = [
        pl.BlockSpec((rows, D_MODEL), lambda i: (i, 0)),
        pl.BlockSpec((rows, tok_w), lambda i: (i, 0)),
        pl.BlockSpec((rows, MEM_W), lambda i: (i, 0)),
        pl.BlockSpec((nb, MEM_W, N_MEM), mem_idx),
        pl.BlockSpec((nb, MEM_W, N_MEM), mem_idx),
        pl.BlockSpec((1, MEM_W), const),
        pl.BlockSpec((MEM_W, MEM_W), const),
        pl.BlockSpec((D_MODEL, D_MODEL), const),
    ]
    args = [x, tok, qm, mk, mv, wl['g_mem_q'], wl['gsum'], wl['w_out']]
    if latent:
        in_specs.append(pl.BlockSpec((MLA_HEADS * KV_LORA, MLA_HEADS * V_HEAD), const))
        args.append(wk['w_uv_bd'])
    return pl.pallas_call(
        functools.partial(_out_b_kernel, nb=nb, ts=ts, latent=latent),
        grid=(n // rows,),
        in_specs=in_specs,
        out_specs=pl.BlockSpec((rows, D_MODEL), lambda i: (i, 0)),
        out_shape=jax.ShapeDtypeStruct((n, D_MODEL), F32),
        compiler_params=_params(1),
        name="out_b",
    )(*args)


def _rope_tables(pos):
    half = QK_ROPE // 2
    inv_freq = ROPE_BASE ** (-jnp.arange(half, dtype=F32) / half)
    ang = pos.astype(F32)[:, None] * inv_freq[None, :]
    cos32 = jnp.concatenate([jnp.cos(ang)] * 2, axis=-1)
    sin32 = jnp.concatenate([jnp.sin(ang)] * 2, axis=-1)
    n = pos.shape[0]
    cos_t = jnp.concatenate([jnp.ones((n, QK_NOPE), F32), cos32,
                             jnp.zeros((n, SLOT - QK_DIM), F32)], axis=-1)
    sin_t = jnp.concatenate([jnp.zeros((n, QK_NOPE), F32), sin32,
                             jnp.zeros((n, SLOT - QK_DIM), F32)], axis=-1)
    return cos32, sin32, cos_t, sin_t


def _rot_cols(w):
    half = QK_ROPE // 2
    return jnp.concatenate([-w[..., half:], w[..., :half]], axis=-1)


def _prep_weights(norm_mix, norm_ffn, w_out, w_ffn_in, w_ffn_out, g_mem_q, w_in_a, w_pool_grp,
                  pool_scale, w_in_b, g_q_lora, w_uq, g_q, norm_kv, w_dkv, g_kv_lora, w_uk, w_uv,
                  g_k_nope, g_k_rope):
    half = QK_ROPE // 2
    gsum = jnp.asarray(np.kron(np.eye(MEM_HEADS), np.ones((MEM_HEAD_DIM, MEM_HEAD_DIM))), BF16)
    layers = []
    for l in range(DEPTH):
        wl = dict(
            g_mix=norm_mix[l][None], g_ffn=norm_ffn[l][None],
            w_out=w_out[l].astype(BF16),
            w_ffn_in=w_ffn_in[l].astype(BF16), w_ffn_out=w_ffn_out[l].astype(BF16),
            g_mem_q=jnp.tile(g_mem_q[l], MEM_HEADS)[None], gsum=gsum)
        if l < N_A_LAYERS:
            wp = jnp.zeros((POOL_W, POOL_W), F32)
            for g in range(len(POOL_WINDOWS)):
                sl = slice(g * POOL_GROUP_W, (g + 1) * POOL_GROUP_W)
                wp = wp.at[sl, sl].set(w_pool_grp[l, g])
            wl.update(w_in=w_in_a[l].astype(BF16), w_pool=wp.astype(BF16),
                      pool_scale=pool_scale[l][None])
        else:
            j = l - N_A_LAYERS
            wq = w_uq[j].reshape(Q_LORA, MLA_HEADS, QK_DIM)
            zpad = jnp.zeros((Q_LORA, MLA_HEADS, SLOT - QK_DIM), F32)
            w_main = jnp.concatenate([wq, zpad], axis=-1).reshape(Q_LORA, Q_SLOTS)
            w_rot = jnp.concatenate([jnp.zeros((Q_LORA, MLA_HEADS, QK_NOPE), F32),
                                     _rot_cols(wq[..., QK_NOPE:]), zpad],
                                    axis=-1).reshape(Q_LORA, Q_SLOTS)
            gk2 = jnp.concatenate([g_k_rope, g_k_rope])
            gq = g_q[j]
            g_rot = jnp.concatenate([gq[QK_NOPE + half:], gq[QK_NOPE:QK_NOPE + half]])
            pad = jnp.zeros((SLOT - QK_DIM,), F32)
            g_a = jnp.concatenate([gq[:QK_NOPE] * g_k_nope, gq[QK_NOPE:] * gk2, pad])
            g_b = jnp.concatenate([jnp.zeros((QK_NOPE,), F32), g_rot * gk2, pad])
            wl.update(w_in=w_in_b[j].astype(BF16), g_q_lora=g_q_lora[j][None],
                      w_uq_main=w_main.astype(BF16), w_uq_rot=w_rot.astype(BF16),
                      g_a=g_a[None], g_b=g_b[None])
        layers.append(wl)

    zcol = jnp.zeros((D_MODEL, LANE - QK_ROPE), F32)
    w_dkv_ext = jnp.concatenate([w_dkv, zcol, _rot_cols(w_dkv[:, KV_LORA:]), zcol], axis=-1)
    wuk_h = w_uk.reshape(KV_LORA, MLA_HEADS, QK_NOPE)
    top = jnp.concatenate([wuk_h, jnp.zeros((KV_LORA, MLA_HEADS, SLOT - QK_NOPE), F32)], axis=-1)
    eye = jnp.eye(QK_ROPE, dtype=F32)
    bot = jnp.concatenate([jnp.zeros((QK_ROPE, QK_NOPE), F32), eye,
                           jnp.zeros((QK_ROPE, SLOT - QK_DIM), F32)], axis=-1)
    bot = jnp.broadcast_to(bot[:, None, :], (QK_ROPE, MLA_HEADS, SLOT))
    w_kslots = jnp.concatenate([top, bot], axis=0).reshape(KV_ROW, Q_SLOTS)
    w_abs = jnp.zeros((MLA_HEADS, SLOT, KV_ROW), F32)
    w_abs = w_abs.at[:, :QK_NOPE, :KV_LORA].set(jnp.transpose(wuk_h, (1, 2, 0)))
    w_abs = w_abs.at[:, QK_NOPE:QK_DIM, KV_LORA:].set(jnp.broadcast_to(eye, (MLA_HEADS,) + eye.shape))
    wuv_h = w_uv.reshape(KV_LORA, MLA_HEADS, V_HEAD)
    w_uvt_slots = jnp.concatenate(
        [jnp.transpose(wuv_h, (1, 2, 0)), jnp.zeros((MLA_HEADS, SLOT - V_HEAD, KV_LORA), F32)],
        axis=1).reshape(Q_SLOTS, KV_LORA)
    w_uv_bd = jnp.einsum('chd,hg->hcgd', wuv_h, jnp.eye(MLA_HEADS, dtype=F32))
    w_uv_bd = w_uv_bd.reshape(MLA_HEADS * KV_LORA, MLA_HEADS * V_HEAD)
    shared = dict(
        norm_kv=norm_kv[None], g_kv_lora=g_kv_lora[None],
        w_dkv_ext=w_dkv_ext.astype(BF16), w_kslots=w_kslots.astype(BF16),
        w_uvt_slots=w_uvt_slots.astype(BF16), w_abs=w_abs.astype(BF16),
        w_ukt=jnp.transpose(w_uk).astype(BF16), w_uv_bd=w_uv_bd.astype(BF16))
    return layers, shared


def _trunk(x, pos0, pool_prev, mem_k, mem_v, layers, shared, *, nb, ts, tm, past):
    b, s, _ = x.shape
    n = b * s
    pos = pos0 + jnp.arange(s)
    tabs = _rope_tables(pos)
    if s >= tm:
        tmk, seq_tiles = tm, s // tm
    else:
        reps = min(tm, n) // s
        tmk, seq_tiles = reps * s, 1
        tabs = tuple(jnp.tile(t, (reps, 1)) for t in tabs)
    cos32, sin32, cos_t, sin_t = tabs
    pool_new = []
    kv_new = None
    for l in range(DEPTH):
        wl = layers[l]
        if l < N_A_LAYERS:
            x, st = _mixer_a(x, pool_prev[l], mem_k[l], mem_v[l], wl, nb=nb, ts=ts, pos0=pos0)
            pool_new.append(st)
            xf = x.reshape(n, D_MODEL)
        else:
            xf = x.reshape(n, D_MODEL)
            if l == N_A_LAYERS:
                if past is None:
                    kv_new, kn, vt = _kvrows(xf, shared, cos32, sin32, tm=tmk,
                                             seq_tiles=seq_tiles, with_kv=True)
                else:
                    (kv_new,) = _kvrows(xf, shared, cos32, sin32, tm=tmk,
                                        seq_tiles=seq_tiles, with_kv=False)
            if past is None:
                qf, qm = _qproj(xf, wl, shared, cos_t, sin_t, tm=tmk, seq_tiles=seq_tiles,
                                absorb=False)
                tok = _mla_prompt(qf.reshape(b, s, Q_SLOTS), kn.reshape(b, s, Q_SLOTS), vt, tq=256,
                                  nh=12)
                tok = tok.reshape(n, MLA_HEADS * V_HEAD)
                nb_o, ts_o = 1, tmk
            else:
                cache_t, page_table = past
                qabs, qm = _qproj(xf, wl, shared, cos_t, sin_t, tm=tmk, seq_tiles=seq_tiles,
                                  absorb=True)
                if l == N_A_LAYERS:
                    kv_new_t = jnp.pad(jnp.swapaxes(kv_new.reshape(b, s, KV_ROW), 1, 2),
                                       ((0, 0), (0, 0), (0, LANE - s)))
                    tok, kinv = _mla_sample(page_table, qabs, shared['w_ukt'], kv_new_t, cache_t,
                                            None, pps=16)
                else:
                    (tok,) = _mla_sample(page_table, qabs, shared['w_ukt'], kv_new_t, cache_t,
                                         kinv, pps=16)
                tok = tok.reshape(n, MLA_HEADS * KV_LORA)
                nb_o, ts_o = nb, ts
            xf = _out_b(xf, tok, qm, mem_k[l], mem_v[l], wl, shared, nb=nb_o, ts=ts_o,
                        latent=past is not None)
        xf = _ffn(xf, wl['g_ffn'], wl['w_ffn_in'], wl['w_ffn_out'], tm=min(tm, n))
        x = xf.reshape(b, s, D_MODEL)
    return x, kv_new.reshape(b, s, KV_ROW), jnp.stack(pool_new)


def kernel(x_prompt, x_sample, cache_kv, state_pool, cache_mem_k, cache_mem_v, page_table, mem_prompt,
           norm_mix, norm_ffn, w_out, w_ffn_in, w_ffn_out, norm_mem, w_mem_kv, g_mem_q, g_mem_k,
           w_in_a, w_pool_grp, pool_scale, w_in_b, g_q_lora, w_uq, g_q,
           norm_kv, w_dkv, g_kv_lora, w_uk, w_uv, g_k_nope, g_k_rope):
    layers, shared = _prep_weights(norm_mix, norm_ffn, w_out, w_ffn_in, w_ffn_out, g_mem_q, w_in_a,
                                   w_pool_grp, pool_scale, w_in_b, g_q_lora, w_uq, g_q, norm_kv,
                                   w_dkv, g_kv_lora, w_uk, w_uv, g_k_nope, g_k_rope)
    b_p, s_p, _ = x_prompt.shape
    n_seq, s_s, _ = x_sample.shape

    mk_p, mv_p = _memkv(mem_prompt, norm_mem[:, None, :],
                        jnp.swapaxes(w_mem_kv, 1, 2).astype(BF16), g_mem_k[:, None, :, None])
    pool0 = jnp.zeros((N_A_LAYERS, b_p, POOL_STATE, POOL_W), F32)
    y_p, kv_p, pool_p = _trunk(x_prompt, 0, pool0, mk_p, mv_p, layers, shared,
                               nb=1, ts=512, tm=512, past=None)

    past_len = page_table.shape[1] * cache_kv.shape[1]
    mk_s = jnp.transpose(cache_mem_k, (0, 1, 3, 4, 2)).reshape(DEPTH, n_seq, MEM_W, N_MEM)
    mv_s = jnp.transpose(cache_mem_v, (0, 1, 3, 4, 2)).reshape(DEPTH, n_seq, MEM_W, N_MEM)
    cache_t = jnp.swapaxes(cache_kv, 1, 2)
    y_s, kv_s, pool_s = _trunk(x_sample, past_len, state_pool, mk_s, mv_s, layers, shared,
                               nb=n_seq // 2, ts=s_s, tm=512, past=(cache_t, page_table))

    def mem_out(a):
        a = a.reshape(DEPTH, b_p, MEM_HEADS, MEM_HEAD_DIM, N_MEM)
        return jnp.transpose(a, (0, 1, 4, 2, 3))

    return (y_p, y_s, kv_p, kv_s, pool_p, pool_s, mem_out(mk_p), mem_out(mv_p))
```

```python
import functools

import numpy as np
import jax
import jax.numpy as jnp
from jax import lax
from jax.experimental import pallas as pl
from jax.experimental.pallas import tpu as pltpu

D_MODEL = 1024
DEPTH = 4
N_A_LAYERS = 2
MEM_HEADS = 4
MEM_HEAD_DIM = 64
MEM_W = MEM_HEADS * MEM_HEAD_DIM
N_MEM = 256
POOL_W = D_MODEL - MEM_W
POOL_WINDOWS = (2, 4, 8, 16)
POOL_GROUP_W = POOL_W // len(POOL_WINDOWS)
POOL_STATE = max(POOL_WINDOWS) - 1
POOL_PAD = POOL_STATE + 1
MLA_HEADS = 12
QK_NOPE = 64
QK_ROPE = 32
QK_DIM = QK_NOPE + QK_ROPE
V_HEAD = 64
KV_LORA = 256
Q_LORA = 384
KV_ROW = KV_LORA + QK_ROPE
ROPE_BASE = 10000.0
D_FF = 2816
EPS = 1e-6
NEG = -1e30
LOG2E = 1.4426950408889634

LANE = 128
SLOT = LANE
Q_SLOTS = MLA_HEADS * SLOT
V_SLOT = V_HEAD + 16
V_SLOTS = MLA_HEADS * V_SLOT
VMEM_LIMIT = 56 * 1024 * 1024

BF16 = jnp.bfloat16
F32 = jnp.float32
NT_DIMS = (((1,), (1,)), ((), ()))


def _params(n_grid, flags=None):
    return pltpu.CompilerParams(dimension_semantics=("arbitrary",) * n_grid,
                                vmem_limit_bytes=VMEM_LIMIT, flags=flags)


def _dot(a, b):
    return jnp.dot(a, b, preferred_element_type=F32)


def _dot_nt(a, b):
    return lax.dot_general(a, b, NT_DIMS, preferred_element_type=F32)


def _split_dot(x, w):
    hi = x.astype(BF16)
    lo = (x - hi.astype(F32)).astype(BF16)
    return _dot(hi, w) + _dot(lo, w)


def _rms(x, g):
    return x * lax.rsqrt(jnp.mean(x * x, axis=-1, keepdims=True) + EPS) * g


def _mem_attend(qm, mkt_ref, mvt_ref, gq, gsum, nb, ts):
    ssq = _split_dot(qm * qm, gsum)
    qn = qm * lax.rsqrt(ssq * (1.0 / MEM_HEAD_DIM) + EPS) * (gq * MEM_HEAD_DIM ** -0.5)
    q3 = qn.reshape(nb, ts, MEM_W)
    mkt = mkt_ref[...].astype(BF16)
    mvt = mvt_ref[...].astype(BF16)
    lane = lax.broadcasted_iota(jnp.int32, (1, 1, MEM_W), 2)
    heads = range(MEM_HEADS)
    in_head = [(lane >= h * MEM_HEAD_DIM) & (lane < (h + 1) * MEM_HEAD_DIM) for h in heads]
    qs = [jnp.where(in_head[h], q3, 0.0).astype(BF16) for h in heads]
    ss = [jnp.einsum('bqd,bdm->bqm', qs[h], mkt, preferred_element_type=F32) for h in heads]
    ps = [jnp.exp(s - jnp.max(s, axis=-1, keepdims=True)) for s in ss]
    ps = [(p / jnp.sum(p, axis=-1, keepdims=True)).astype(BF16) for p in ps]
    os_ = [jnp.einsum('bqm,bdm->bqd', p, mvt, preferred_element_type=F32) for p in ps]
    out = jnp.where(in_head[0], os_[0], 0.0)
    for h in range(1, MEM_HEADS):
        out = jnp.where(in_head[h], os_[h], out)
    return out.reshape(nb * ts, MEM_W)


def _memkv_kernel(mem_ref, gnorm_ref, wt_ref, gk_ref, kt_ref, vt_ref):
    x = mem_ref[0]
    xn = x * lax.rsqrt(jnp.mean(x * x, axis=-1, keepdims=True) + EPS)
    hm = (xn * gnorm_ref[0]).astype(BF16)
    kvt = _dot_nt(wt_ref[0], hm)
    kt = kvt[:MEM_W].reshape(MEM_HEADS, MEM_HEAD_DIM, N_MEM)
    inv = lax.rsqrt(jnp.mean(kt * kt, axis=1, keepdims=True) + EPS)
    kt_ref[0, 0] = (kt * inv * gk_ref[0]).reshape(MEM_W, N_MEM)
    vt_ref[0, 0] = kvt[MEM_W:]


def _memkv(mem, norm_mem, w_mem_kv_t, g_mem_k):
    b = mem.shape[0]
    return pl.pallas_call(
        _memkv_kernel,
        grid=(DEPTH, b),
        in_specs=[
            pl.BlockSpec((1, N_MEM, D_MODEL), lambda l, i: (i, 0, 0)),
            pl.BlockSpec((1, 1, D_MODEL), lambda l, i: (l, 0, 0)),
            pl.BlockSpec((1, 2 * MEM_W, D_MODEL), lambda l, i: (l, 0, 0)),
            pl.BlockSpec((1, 1, MEM_HEAD_DIM, 1), lambda l, i: (l, 0, 0, 0)),
        ],
        out_specs=[pl.BlockSpec((1, 1, MEM_W, N_MEM), lambda l, i: (l, i, 0, 0)),
                   pl.BlockSpec((1, 1, MEM_W, N_MEM), lambda l, i: (l, i, 0, 0))],
        out_shape=[jax.ShapeDtypeStruct((DEPTH, b, MEM_W, N_MEM), F32)] * 2,
        compiler_params=_params(2),
        name="memkv",
    )(mem, norm_mem, w_mem_kv_t, g_mem_k)


def _mixer_a_kernel(x_ref, g_ref, w_in_ref, prev_ref, wpool_ref, pscale_ref,
                    mk_ref, mv_ref, gq_ref, gsum_ref, w_out_ref,
                    xo_ref, pool_ref, ext_ref, *, nb, ts, pos0):
    j = pl.program_id(1)
    rows = nb * ts
    x = x_ref[...].reshape(rows, D_MODEL)
    h = _rms(x, g_ref[...]).astype(BF16)
    proj = _dot(h, w_in_ref[...])
    u = proj[:, :POOL_W]
    qm = proj[:, POOL_W:]

    @pl.when(j == 0)
    def _():
        ext_ref[:, 0:POOL_PAD, :] = jnp.zeros((nb, POOL_PAD, POOL_W), F32)
        ext_ref[:, 1:POOL_PAD, :] = prev_ref[...]

    @pl.when(j > 0)
    def _():
        ext_ref[:, 0:POOL_PAD, :] = ext_ref[:, ts:ts + POOL_PAD, :]

    ext_ref[:, POOL_PAD:, :] = u.reshape(nb, ts, POOL_W)
    e = ext_ref[...].reshape(nb * (POOL_PAD + ts), POOL_W)
    s2 = e + pltpu.roll(e, 1, 0)
    s4 = s2 + pltpu.roll(s2, 2, 0)
    s8 = s4 + pltpu.roll(s4, 4, 0)
    s16 = s8 + pltpu.roll(s8, 8, 0)

    def tile_rows(a):
        return a.reshape(nb, POOL_PAD + ts, POOL_W)[:, POOL_PAD:, :]

    t_idx = lax.broadcasted_iota(jnp.int32, (1, ts, 1), 1)
    pos1 = (pos0 + 1 + j * ts + t_idx).astype(F32)
    col = lax.broadcasted_iota(jnp.int32, (1, 1, POOL_W), 2)
    pooled = None
    for g, (w, sw) in enumerate(zip(POOL_WINDOWS, (s2, s4, s8, s16))):
        val = tile_rows(sw) / jnp.minimum(float(w), pos1)
        if pooled is None:
            pooled = val
        else:
            pooled = jnp.where(col >= g * POOL_GROUP_W, val, pooled)
    diff = (pooled.reshape(rows, POOL_W) - u).astype(BF16)
    mixed = _dot(diff, wpool_ref[...]) * pscale_ref[...]

    mo = _mem_attend(qm, mk_ref, mv_ref, gq_ref[...], gsum_ref[...], nb, ts)

    y = _dot(mixed.astype(BF16), w_out_ref[0:POOL_W, :])
    y = y + _dot(mo.astype(BF16), w_out_ref[POOL_W:, :])
    xo_ref[...] = (x + y).reshape(nb, ts, D_MODEL)

    @pl.when(j == pl.num_programs(1) - 1)
    def _():
        pool_ref[...] = ext_ref[:, ts + 1:ts + POOL_PAD, :]


def _mixer_a(x, prev, mk, mv, wl, *, nb, ts, pos0):
    b, s, _ = x.shape
    kern = functools.partial(_mixer_a_kernel, nb=nb, ts=ts, pos0=pos0)
    const = lambda i, j: (0, 0)
    return pl.pallas_call(
        kern,
        grid=(b // nb, s // ts),
        in_specs=[
            pl.BlockSpec((nb, ts, D_MODEL), lambda i, j: (i, j, 0)),
            pl.BlockSpec((1, D_MODEL), const),
            pl.BlockSpec((None, D_MODEL, D_MODEL), lambda i, j: (wl['l_in'], 0, 0)),
            pl.BlockSpec((nb, POOL_STATE, POOL_W), lambda i, j: (i, 0, 0)),
            pl.BlockSpec((POOL_W, POOL_W), const),
            pl.BlockSpec((1, POOL_W), const),
            pl.BlockSpec((None, nb, MEM_W, N_MEM), lambda i, j: (wl['l'], i, 0, 0)),
            pl.BlockSpec((None, nb, MEM_W, N_MEM), lambda i, j: (wl['l'], i, 0, 0)),
            pl.BlockSpec((1, MEM_W), const),
            pl.BlockSpec((MEM_W, MEM_W), const),
            pl.BlockSpec((None, D_MODEL, D_MODEL), lambda i, j: (wl['l'], 0, 0)),
        ],
        out_specs=[pl.BlockSpec((nb, ts, D_MODEL), lambda i, j: (i, j, 0)),
                   pl.BlockSpec((nb, POOL_STATE, POOL_W), lambda i, j: (i, 0, 0))],
        out_shape=[jax.ShapeDtypeStruct((b, s, D_MODEL), F32),
                   jax.ShapeDtypeStruct((b, POOL_STATE, POOL_W), F32)],
        scratch_shapes=[pltpu.VMEM((nb, POOL_PAD + ts, POOL_W), F32)],
        compiler_params=_params(2),
        name="mixer_a",
    )(x, wl['g_mix'], wl['w_in'], prev, wl['w_pool'], wl['pool_scale'],
      mk, mv, wl['g_mem_q'], wl['gsum'], wl['w_out'])


FFN_CHUNK = 1408


def _ffn_kernel(x_ref, g_ref, w_in_ref, w_out_ref, o_ref):
    x = x_ref[...]
    h = _rms(x, g_ref[...]).astype(BF16)
    acc = x
    for c0 in range(0, D_FF, FFN_CHUNK):
        gate = _dot(h, w_in_ref[:, c0:c0 + FFN_CHUNK])
        up = _dot(h, w_in_ref[:, D_FF + c0:D_FF + c0 + FFN_CHUNK])
        a = (gate * jax.nn.sigmoid(gate) * up).astype(BF16)
        acc = acc + _dot(a, w_out_ref[c0:c0 + FFN_CHUNK, :])
    o_ref[...] = acc


def _ffn(x, g, w_in, w_out, layer, *, tm):
    n = x.shape[0]
    const = lambda i: (0, 0)
    return pl.pallas_call(
        _ffn_kernel,
        grid=(n // tm,),
        in_specs=[
            pl.BlockSpec((tm, D_MODEL), lambda i: (i, 0)),
            pl.BlockSpec((1, D_MODEL), const),
            pl.BlockSpec((None, D_MODEL, 2 * D_FF), lambda i: (layer, 0, 0),
                         pipeline_mode=pl.Buffered(1)),
            pl.BlockSpec((None, D_FF, D_MODEL), lambda i: (layer, 0, 0),
                         pipeline_mode=pl.Buffered(1)),
        ],
        out_specs=pl.BlockSpec((tm, D_MODEL), lambda i: (i, 0)),
        out_shape=jax.ShapeDtypeStruct((n, D_MODEL), F32),
        compiler_params=_params(1),
        name="ffn",
    )(x, g, w_in, w_out)


def _kvrows_kernel(x_ref, g_ref, w_ref, gl_ref, cos_ref, sin_ref, *rest, with_kv):
    if with_kv:
        wk_ref, wv_ref, kv_ref, kn_ref, v_ref = rest
    else:
        (kv_ref,) = rest
    x = x_ref[...]
    h = _rms(x, g_ref[...]).astype(BF16)
    ck = _dot(h, w_ref[...])
    c = _rms(ck[:, :KV_LORA], gl_ref[...])
    kr = (ck[:, KV_LORA:KV_ROW] * cos_ref[...]
          + ck[:, KV_LORA + LANE:KV_LORA + LANE + QK_ROPE] * sin_ref[...])
    kv_ref[:, 0:KV_LORA] = c
    kv_ref[:, KV_LORA:KV_ROW] = kr
    if with_kv:
        cb = c.astype(BF16)
        kn = _dot(cb, wk_ref[0:KV_LORA, :]) + _dot(kr.astype(BF16), wk_ref[KV_LORA:KV_ROW, :])
        for hd in range(MLA_HEADS):
            sl = slice(hd * SLOT, (hd + 1) * SLOT)
            kh = kn[:, sl]
            inv = lax.rsqrt(jnp.sum(kh * kh, axis=-1, keepdims=True) * (1.0 / QK_DIM) + EPS)
            kn_ref[:, sl] = (kh * inv).astype(BF16)
        vt = _dot_nt(wv_ref[...], cb)
        row = lax.broadcasted_iota(jnp.int32, (V_SLOTS, 1), 0)
        is_one = functools.reduce(
            jnp.logical_or, [row == hd * V_SLOT + V_HEAD for hd in range(MLA_HEADS)])
        v_ref[0] = jnp.where(is_one, 1.0, vt).astype(BF16)


def _kvrows(x, wk, cos32, sin32, *, tm, seq_tiles, with_kv):
    n = x.shape[0]
    const = lambda i: (0, 0)
    in_specs = [
        pl.BlockSpec((tm, D_MODEL), lambda i: (i, 0)),
        pl.BlockSpec((1, D_MODEL), const),
        pl.BlockSpec((D_MODEL, 4 * LANE), const),
        pl.BlockSpec((1, KV_LORA), const),
        pl.BlockSpec((tm, QK_ROPE), lambda i: (i % seq_tiles, 0)),
        pl.BlockSpec((tm, QK_ROPE), lambda i: (i % seq_tiles, 0)),
    ]
    args = [x, wk['norm_kv'], wk['w_dkv_ext'], wk['g_kv_lora'], cos32, sin32]
    out_specs = [pl.BlockSpec((tm, KV_ROW), lambda i: (i, 0))]
    out_shape = [jax.ShapeDtypeStruct((n, KV_ROW), F32)]
    if with_kv:
        in_specs += [pl.BlockSpec((KV_ROW, Q_SLOTS), const),
                     pl.BlockSpec((V_SLOTS, KV_LORA), const)]
        args += [wk['w_kslots'], wk['w_uvt_slots']]
        out_specs += [pl.BlockSpec((tm, Q_SLOTS), lambda i: (i, 0)),
                      pl.BlockSpec((1, V_SLOTS, tm), lambda i: (i // seq_tiles, 0, i % seq_tiles))]
        out_shape += [jax.ShapeDtypeStruct((n, Q_SLOTS), BF16),
                      jax.ShapeDtypeStruct((n // (tm * seq_tiles), V_SLOTS, tm * seq_tiles), BF16)]
    return pl.pallas_call(
        functools.partial(_kvrows_kernel, with_kv=with_kv),
        grid=(n // tm,),
        in_specs=in_specs, out_specs=out_specs, out_shape=out_shape,
        compiler_params=_params(1),
        name="kvrows",
    )(*args)


def _qproj_kernel(x_ref, g_ref, w_in_ref, gql_ref, wm_ref, wr_ref, ga_ref, gb_ref,
                  cos_ref, sin_ref, *rest, absorb):
    if absorb:
        wabs_ref, qabs_ref, qm_ref = rest
    else:
        qf_ref, qm_ref = rest
    x = x_ref[...]
    h = _rms(x, g_ref[...]).astype(BF16)
    proj = _dot(h, w_in_ref[...])
    qm_ref[...] = proj[:, Q_LORA:]
    cq = _rms(proj[:, :Q_LORA], gql_ref[...]).astype(BF16)
    q_main = _dot(cq, wm_ref[...])
    q_rot = _dot(cq, wr_ref[...])
    a_tab = ga_ref[...] * cos_ref[...]
    b_tab = gb_ref[...] * sin_ref[...]
    for hd in range(MLA_HEADS):
        sl = slice(hd * SLOT, (hd + 1) * SLOT)
        qh = q_main[:, sl]
        inv = lax.rsqrt(jnp.sum(qh * qh, axis=-1, keepdims=True) * (1.0 / QK_DIM) + EPS)
        qf = ((qh * a_tab + q_rot[:, sl] * b_tab) * (inv * (QK_DIM ** -0.5 * LOG2E))).astype(BF16)
        if absorb:
            qabs_ref[hd] = _dot(qf, wabs_ref[hd])
        else:
            qf_ref[:, sl] = qf


def _qproj(x, wl, wk, cos_t, sin_t, *, tm, seq_tiles, absorb):
    n = x.shape[0]
    const = lambda i: (0, 0)
    in_specs = [
        pl.BlockSpec((tm, D_MODEL), lambda i: (i, 0)),
        pl.BlockSpec((1, D_MODEL), const),
        pl.BlockSpec((None, D_MODEL, Q_LORA + MEM_W), lambda i: (wl['l_in'], 0, 0)),
        pl.BlockSpec((1, Q_LORA), const),
        pl.BlockSpec((Q_LORA, Q_SLOTS), const),
        pl.BlockSpec((Q_LORA, Q_SLOTS), const),
        pl.BlockSpec((1, SLOT), const),
        pl.BlockSpec((1, SLOT), const),
        pl.BlockSpec((tm, SLOT), lambda i: (i % seq_tiles, 0)),
        pl.BlockSpec((tm, SLOT), lambda i: (i % seq_tiles, 0)),
    ]
    args = [x, wl['g_mix'], wl['w_in'], wl['g_q_lora'], wl['w_uq_main'], wl['w_uq_rot'],
            wl['g_a'], wl['g_b'], cos_t, sin_t]
    if absorb:
        in_specs.append(pl.BlockSpec((MLA_HEADS, SLOT, KV_ROW), lambda i: (0, 0, 0)))
        args.append(wk['w_abs'])
        out_specs = [pl.BlockSpec((MLA_HEADS, tm, KV_ROW), lambda i: (0, i, 0))]
        out_shape = [jax.ShapeDtypeStruct((MLA_HEADS, n, KV_ROW), F32)]
    else:
        out_specs = [pl.BlockSpec((tm, Q_SLOTS), lambda i: (i, 0))]
        out_shape = [jax.ShapeDtypeStruct((n, Q_SLOTS), BF16)]
    out_specs.append(pl.BlockSpec((tm, MEM_W), lambda i: (i, 0)))
    out_shape.append(jax.ShapeDtypeStruct((n, MEM_W), F32))
    return pl.pallas_call(
        functools.partial(_qproj_kernel, absorb=absorb),
        grid=(n // tm,),
        in_specs=in_specs, out_specs=out_specs, out_shape=out_shape,
        compiler_params=_params(1),
        name="qproj",
    )(*args)


def _mla_prompt_kernel(q_ref, k_ref, vt_ref, o_ref, *, tq, nh):
    qi = pl.program_id(2)
    q = q_ref[0]
    q_idx = qi * tq + lax.broadcasted_iota(jnp.int32, (1, tq), 1)
    heads = range(nh)
    slots = [slice(hd * SLOT, (hd + 1) * SLOT) for hd in heads]

    def block(kb, carry, masked):
        start = pl.multiple_of(kb * tq, tq)
        k = k_ref[0, pl.ds(start, tq), :]
        vt = vt_ref[0, :, pl.ds(start, tq)]
        sts = [_dot_nt(k[:, sl], q[:, sl]) for sl in slots]
        if masked:
            k_idx = start + lax.broadcasted_iota(jnp.int32, (tq, 1), 0)
            sts = [jnp.where(k_idx <= q_idx, st, NEG) for st in sts]
        ms = [jnp.maximum(carry[2 * hd], jnp.max(sts[hd], axis=0, keepdims=True))
              for hd in heads]
        ps = [jnp.exp2(sts[hd] - ms[hd]).astype(BF16) for hd in heads]
        pvs = [_dot(vt[hd * V_SLOT:(hd + 1) * V_SLOT, :], ps[hd]) for hd in heads]
        new = []
        for hd in heads:
            alpha = jnp.exp2(carry[2 * hd] - ms[hd])
            new += [ms[hd], carry[2 * hd + 1] * alpha + pvs[hd]]
        return tuple(new)

    init = (jnp.full((1, tq), NEG, F32), jnp.zeros((V_SLOT, tq), F32)) * nh
    carry = lax.fori_loop(0, qi, lambda kb, c: block(kb, c, False), init)
    carry = block(qi, carry, True)
    outs = [carry[2 * hd + 1][:V_HEAD] / carry[2 * hd + 1][V_HEAD:V_HEAD + 1] for hd in heads]
    o_ref[0] = jnp.concatenate(outs, axis=0).T.astype(BF16)


def _mla_prompt(qf, kn, vt, *, tq, nh):
    b, s, _ = qf.shape
    return pl.pallas_call(
        functools.partial(_mla_prompt_kernel, tq=tq, nh=nh),
        grid=(b, MLA_HEADS // nh, s // tq),
        in_specs=[
            pl.BlockSpec((1, tq, nh * SLOT), lambda i, hg, qi: (i, qi, hg)),
            pl.BlockSpec((1, s, nh * SLOT), lambda i, hg, qi: (i, 0, hg)),
            pl.BlockSpec((1, nh * V_SLOT, s), lambda i, hg, qi: (i, hg, 0)),
        ],
        out_specs=pl.BlockSpec((1, tq, nh * V_HEAD), lambda i, hg, qi: (i, qi, hg)),
        out_shape=jax.ShapeDtypeStruct((b, s, MLA_HEADS * V_HEAD), BF16),
        compiler_params=_params(3),
        name="mla_prompt",
    )(qf, kn, vt)


Q_ROWS = MLA_HEADS * 8
N_KN = MLA_HEADS * QK_NOPE
SAMPLE_SUB_PAGES = 4


def _mla_sample_kernel(pt_ref, qabs_ref, wukt_ref, kvnew_ref, *rest, pps, sub_pages, n_tok,
                       first_layer):
    page_refs = rest[:pps]
    rest = rest[pps:]
    if first_layer:
        o_ref, kinv_out_ref, m_ref, l_ref, acc_ref, lhs_ref = rest
    else:
        kinv_in_ref, o_ref, m_ref, l_ref, acc_ref, lhs_ref = rest
    step = pl.program_id(1)
    qabs = qabs_ref[...].reshape(Q_ROWS, KV_ROW)
    q_rope = qabs[:, KV_LORA:].astype(BF16)

    @pl.when(step == 0)
    def _():
        m_ref[...] = jnp.full(m_ref.shape, NEG, F32)
        l_ref[...] = jnp.zeros(l_ref.shape, F32)
        acc_ref[...] = jnp.zeros(acc_ref.shape, F32)
        lhs_ref[0:N_KN, :] = wukt_ref[...]
        lhs_ref[N_KN:, :] = qabs[:, :KV_LORA].astype(BF16)

    def scores(ct, krt, kinv, causal):
        t = ct.shape[1]
        if kinv is None:
            r1 = _dot(lhs_ref[...], ct)
            kn = r1[:N_KN]
            ss = jnp.sum((kn * kn).reshape(MLA_HEADS, QK_NOPE, t), axis=1)
            krsq = jnp.sum(krt * krt, axis=0, keepdims=True)
            kinv = lax.rsqrt((ss + krsq) * (1.0 / QK_DIM) + EPS)
            s_lat = r1[N_KN:]
        else:
            s_lat = _dot(lhs_ref[N_KN:, :], ct)
        sc = s_lat + _dot(q_rope, krt.astype(BF16))
        sc = sc.reshape(MLA_HEADS, n_tok, t) * kinv[:, None, :]
        if causal:
            tok = lax.broadcasted_iota(jnp.int32, (1, n_tok, t), 1)
            key = lax.broadcasted_iota(jnp.int32, (1, n_tok, t), 2)
            sc = jnp.where(key <= tok, sc, NEG)
        return sc.reshape(Q_ROWS, t), kinv

    def accumulate(state, sc, ct):
        m_prev, l_prev, acc = state
        m_new = jnp.maximum(m_prev, jnp.max(sc, axis=-1, keepdims=True))
        alpha = jnp.exp2(m_prev - m_new)
        p = jnp.exp2(sc - m_new)
        l_new = alpha * l_prev + jnp.sum(p, axis=-1, keepdims=True)
        return m_new, l_new, alpha * acc + _dot_nt(p.astype(BF16), ct)

    n_sub = pps // sub_pages
    t_sub = sub_pages * page_refs[0].shape[2]

    def sub_scores(j):
        refs = page_refs[j * sub_pages:(j + 1) * sub_pages]
        ct = jnp.concatenate([r[0, 0:KV_LORA, :].astype(BF16) for r in refs], axis=1)
        krt = jnp.concatenate([r[0, KV_LORA:, :] for r in refs], axis=1)
        if first_layer:
            sc, kinv = scores(ct, krt, None, False)
            kinv_out_ref[0, :, j * t_sub:(j + 1) * t_sub] = kinv
        else:
            sc, _ = scores(ct, krt, kinv_in_ref[0, :, j * t_sub:(j + 1) * t_sub], False)
        return sc, ct

    if first_layer:
        state = (m_ref[...], l_ref[...], acc_ref[...])
        pending = sub_scores(0)
        for j in range(1, n_sub):
            nxt = sub_scores(j)
            state = accumulate(state, *pending)
            pending = nxt
        m_ref[...], l_ref[...], acc_ref[...] = accumulate(state, *pending)
    else:
        subs = [sub_scores(j) for j in range(n_sub)]
        m_prev = m_ref[...]
        m_blk = functools.reduce(jnp.maximum, [sc for sc, _ in subs])
        m_new = jnp.maximum(m_prev, jnp.max(m_blk, axis=-1, keepdims=True))
        alpha = jnp.exp2(m_prev - m_new)
        ps = [jnp.exp2(sc - m_new) for sc, _ in subs]
        l_ref[...] = alpha * l_ref[...] + sum(jnp.sum(p, axis=-1, keepdims=True) for p in ps)
        pv = sum(_dot_nt(p.astype(BF16), ct) for p, (_, ct) in zip(ps, subs))
        acc_ref[...] = alpha * acc_ref[...] + pv
        m_ref[...] = m_new

    @pl.when(step == pl.num_programs(1) - 1)
    def _():
        kvt = kvnew_ref[0]
        ct = kvt[:KV_LORA].astype(BF16)
        sc, _ = scores(ct, kvt[KV_LORA:], None, True)
        _, l_fin, acc = accumulate((m_ref[...], l_ref[...], acc_ref[...]), sc, ct)
        o = acc / l_fin
        for hd in range(MLA_HEADS):
            o_ref[0, :, hd * KV_LORA:(hd + 1) * KV_LORA] = o[hd * n_tok:(hd + 1) * n_tok, :]


def _mla_sample(page_table, qabs, w_ukt, kv_new_t, cache_t, kinv, *, pps):
    n_seq, n_pages = page_table.shape
    n_tok = qabs.shape[1] // n_seq
    page = cache_t.shape[2]
    first_layer = kinv is None
    t_step = pps * page
    assert n_tok == 8 and n_pages % pps == 0

    def page_spec(i):
        return pl.BlockSpec((1, KV_ROW, page), lambda b, s, pt: (pt[b, s * pps + i], 0, 0))

    in_specs = [
        pl.BlockSpec((MLA_HEADS, 1, n_tok, KV_ROW), lambda b, s, pt: (0, b, 0, 0)),
        pl.BlockSpec((N_KN, KV_LORA), lambda b, s, pt: (0, 0)),
        pl.BlockSpec((1, KV_ROW, LANE), lambda b, s, pt: (b, 0, 0)),
    ] + [page_spec(i) for i in range(pps)]
    args = [qabs.reshape(MLA_HEADS, n_seq, n_tok, KV_ROW), w_ukt, kv_new_t] + [cache_t] * pps
    out_specs = [pl.BlockSpec((1, n_tok, MLA_HEADS * KV_LORA), lambda b, s, pt: (b, 0, 0))]
    out_shape = [jax.ShapeDtypeStruct((n_seq, n_tok, MLA_HEADS * KV_LORA), F32)]
    kinv_spec = pl.BlockSpec((1, MLA_HEADS, t_step), lambda b, s, pt: (b, 0, s))
    if first_layer:
        out_specs.append(kinv_spec)
        out_shape.append(jax.ShapeDtypeStruct((n_seq, MLA_HEADS, n_pages * page), F32))
    else:
        in_specs.append(kinv_spec)
        args.append(kinv)
    grid_spec = pltpu.PrefetchScalarGridSpec(
        num_scalar_prefetch=1,
        grid=(n_seq, n_pages // pps),
        in_specs=in_specs,
        out_specs=out_specs,
        scratch_shapes=[
            pltpu.VMEM((Q_ROWS, 1), F32),
            pltpu.VMEM((Q_ROWS, 1), F32),
            pltpu.VMEM((Q_ROWS, KV_LORA), F32),
            pltpu.VMEM((N_KN + Q_ROWS, KV_LORA), BF16),
        ],
    )
    return pl.pallas_call(
        functools.partial(_mla_sample_kernel, pps=pps, sub_pages=SAMPLE_SUB_PAGES, n_tok=n_tok,
                          first_layer=first_layer),
        grid_spec=grid_spec,
        out_shape=out_shape,
        compiler_params=_params(2),
        name="mla_sample",
    )(page_table, *args)


def _out_b_kernel(x_ref, tok_ref, qm_ref, mk_ref, mv_ref, gq_ref, gsum_ref, w_out_ref,
                  *rest, nb, ts, latent):
    if latent:
        wuv_ref, o_ref = rest
        tok = _dot(tok_ref[...].astype(BF16), wuv_ref[...]).astype(BF16)
    else:
        (o_ref,) = rest
        tok = tok_ref[...]
    mo = _mem_attend(qm_ref[...], mk_ref, mv_ref, gq_ref[...], gsum_ref[...], nb, ts)
    y = _dot(tok, w_out_ref[0:POOL_W, :]) + _dot(mo.astype(BF16), w_out_ref[POOL_W:, :])
    o_ref[...] = x_ref[...] + y


def _out_b(x, tok, qm, mk, mv, wl, wk, *, nb, ts, latent):
    n = x.shape[0]
    rows = nb * ts
    tok_w = tok.shape[1]
    const = lambda i: (0, 0)
    seq_len = n // mk.shape[1]
    mem_idx = lambda i: (wl['l'], i * rows // (seq_len * nb), 0, 0)
    in_specs = [
        pl.BlockSpec((rows, D_MODEL), lambda i: (i, 0)),
        pl.BlockSpec((rows, tok_w), lambda i: (i, 0)),
        pl.BlockSpec((rows, MEM_W), lambda i: (i, 0)),
        pl.BlockSpec((None, nb, MEM_W, N_MEM), mem_idx),
        pl.BlockSpec((None, nb, MEM_W, N_MEM), mem_idx),
        pl.BlockSpec((1, MEM_W), const),
        pl.BlockSpec((MEM_W, MEM_W), const),
        pl.BlockSpec((None, D_MODEL, D_MODEL), lambda i: (wl['l'], 0, 0)),
    ]
    args = [x, tok, qm, mk, mv, wl['g_mem_q'], wl['gsum'], wl['w_out']]
    if latent:
        in_specs.append(pl.BlockSpec((MLA_HEADS * KV_LORA, MLA_HEADS * V_HEAD), const))
        args.append(wk['w_uv_bd'])
    return pl.pallas_call(
        functools.partial(_out_b_kernel, nb=nb, ts=ts, latent=latent),
        grid=(n // rows,),
        in_specs=in_specs,
        out_specs=pl.BlockSpec((rows, D_MODEL), lambda i: (i, 0)),
        out_shape=jax.ShapeDtypeStruct((n, D_MODEL), F32),
        compiler_params=_params(1),
        name="out_b",
    )(*args)


def _rope_tables(pos):
    half = QK_ROPE // 2
    inv_freq = ROPE_BASE ** (-jnp.arange(half, dtype=F32) / half)
    ang = pos.astype(F32)[:, None] * inv_freq[None, :]
    cos32 = jnp.concatenate([jnp.cos(ang)] * 2, axis=-1)
    sin32 = jnp.concatenate([jnp.sin(ang)] * 2, axis=-1)
    n = pos.shape[0]
    cos_t = jnp.concatenate([jnp.ones((n, QK_NOPE), F32), cos32,
                             jnp.zeros((n, SLOT - QK_DIM), F32)], axis=-1)
    sin_t = jnp.concatenate([jnp.zeros((n, QK_NOPE), F32), sin32,
                             jnp.zeros((n, SLOT - QK_DIM), F32)], axis=-1)
    return cos32, sin32, cos_t, sin_t


def _rot_cols(w):
    half = QK_ROPE // 2
    return jnp.concatenate([-w[..., half:], w[..., :half]], axis=-1)


def _prep_weights(norm_mix, norm_ffn, w_out, w_ffn_in, w_ffn_out, g_mem_q, w_in_a, w_pool_grp,
                  pool_scale, w_in_b, g_q_lora, w_uq, g_q, norm_kv, w_dkv, g_kv_lora, w_uk, w_uv,
                  g_k_nope, g_k_rope):
    half = QK_ROPE // 2
    gsum = jnp.asarray(np.kron(np.eye(MEM_HEADS), np.ones((MEM_HEAD_DIM, MEM_HEAD_DIM))), BF16)
    w_out_b, w_ffn_in_b, w_ffn_out_b = (w.astype(BF16) for w in (w_out, w_ffn_in, w_ffn_out))
    w_in_a_b, w_in_b_b = w_in_a.astype(BF16), w_in_b.astype(BF16)
    layers = []
    for l in range(DEPTH):
        wl = dict(
            l=l, g_mix=norm_mix[l][None], g_ffn=norm_ffn[l][None],
            w_out=w_out_b, w_ffn_in=w_ffn_in_b, w_ffn_out=w_ffn_out_b,
            g_mem_q=jnp.tile(g_mem_q[l], MEM_HEADS)[None], gsum=gsum)
        if l < N_A_LAYERS:
            wp = jnp.zeros((POOL_W, POOL_W), F32)
            for g in range(len(POOL_WINDOWS)):
                sl = slice(g * POOL_GROUP_W, (g + 1) * POOL_GROUP_W)
                wp = wp.at[sl, sl].set(w_pool_grp[l, g])
            wl.update(w_in=w_in_a_b, l_in=l, w_pool=wp.astype(BF16),
                      pool_scale=pool_scale[l][None])
        else:
            j = l - N_A_LAYERS
            wq = w_uq[j].reshape(Q_LORA, MLA_HEADS, QK_DIM)
            zpad = jnp.zeros((Q_LORA, MLA_HEADS, SLOT - QK_DIM), F32)
            w_main = jnp.concatenate([wq, zpad], axis=-1).reshape(Q_LORA, Q_SLOTS)
            w_rot = jnp.concatenate([jnp.zeros((Q_LORA, MLA_HEADS, QK_NOPE), F32),
                                     _rot_cols(wq[..., QK_NOPE:]), zpad],
                                    axis=-1).reshape(Q_LORA, Q_SLOTS)
            gk2 = jnp.concatenate([g_k_rope, g_k_rope])
            gq = g_q[j]
            g_rot = jnp.concatenate([gq[QK_NOPE + half:], gq[QK_NOPE:QK_NOPE + half]])
            pad = jnp.zeros((SLOT - QK_DIM,), F32)
            g_a = jnp.concatenate([gq[:QK_NOPE] * g_k_nope, gq[QK_NOPE:] * gk2, pad])
            g_b = jnp.concatenate([jnp.zeros((QK_NOPE,), F32), g_rot * gk2, pad])
            wl.update(w_in=w_in_b_b, l_in=j, g_q_lora=g_q_lora[j][None],
                      w_uq_main=w_main.astype(BF16), w_uq_rot=w_rot.astype(BF16),
                      g_a=g_a[None], g_b=g_b[None])
        layers.append(wl)

    zcol = jnp.zeros((D_MODEL, LANE - QK_ROPE), F32)
    w_dkv_ext = jnp.concatenate([w_dkv, zcol, _rot_cols(w_dkv[:, KV_LORA:]), zcol], axis=-1)
    wuk_h = w_uk.reshape(KV_LORA, MLA_HEADS, QK_NOPE)
    top = jnp.concatenate([wuk_h, jnp.zeros((KV_LORA, MLA_HEADS, SLOT - QK_NOPE), F32)], axis=-1)
    eye = jnp.eye(QK_ROPE, dtype=F32)
    bot = jnp.concatenate([jnp.zeros((QK_ROPE, QK_NOPE), F32), eye,
                           jnp.zeros((QK_ROPE, SLOT - QK_DIM), F32)], axis=-1)
    bot = jnp.broadcast_to(bot[:, None, :], (QK_ROPE, MLA_HEADS, SLOT))
    w_kslots = jnp.concatenate([top, bot], axis=0).reshape(KV_ROW, Q_SLOTS)
    w_abs = jnp.zeros((MLA_HEADS, SLOT, KV_ROW), F32)
    w_abs = w_abs.at[:, :QK_NOPE, :KV_LORA].set(jnp.transpose(wuk_h, (1, 2, 0)))
    w_abs = w_abs.at[:, QK_NOPE:QK_DIM, KV_LORA:].set(jnp.broadcast_to(eye, (MLA_HEADS,) + eye.shape))
    wuv_h = w_uv.reshape(KV_LORA, MLA_HEADS, V_HEAD)
    w_uvt_slots = jnp.concatenate(
        [jnp.transpose(wuv_h, (1, 2, 0)), jnp.zeros((MLA_HEADS, V_SLOT - V_HEAD, KV_LORA), F32)],
        axis=1).reshape(V_SLOTS, KV_LORA)
    w_uv_bd = jnp.einsum('chd,hg->hcgd', wuv_h, jnp.eye(MLA_HEADS, dtype=F32))
    w_uv_bd = w_uv_bd.reshape(MLA_HEADS * KV_LORA, MLA_HEADS * V_HEAD)
    shared = dict(
        norm_kv=norm_kv[None], g_kv_lora=g_kv_lora[None],
        w_dkv_ext=w_dkv_ext.astype(BF16), w_kslots=w_kslots.astype(BF16),
        w_uvt_slots=w_uvt_slots.astype(BF16), w_abs=w_abs.astype(BF16),
        w_ukt=jnp.transpose(w_uk).astype(BF16), w_uv_bd=w_uv_bd.astype(BF16))
    return layers, shared


def _trunk(x, pos0, pool_prev, mem_k, mem_v, layers, shared, *, nb, ts, tm, past):
    b, s, _ = x.shape
    n = b * s
    pos = pos0 + jnp.arange(s)
    tabs = _rope_tables(pos)
    if s >= tm:
        tmk, seq_tiles = tm, s // tm
    else:
        reps = min(tm, n) // s
        tmk, seq_tiles = reps * s, 1
        tabs = tuple(jnp.tile(t, (reps, 1)) for t in tabs)
    cos32, sin32, cos_t, sin_t = tabs
    pool_new = []
    kv_new = None
    for l in range(DEPTH):
        wl = layers[l]
        if l < N_A_LAYERS:
            x, st = _mixer_a(x, pool_prev[l], mem_k, mem_v, wl, nb=nb, ts=ts, pos0=pos0)
            pool_new.append(st)
            xf = x.reshape(n, D_MODEL)
        else:
            xf = x.reshape(n, D_MODEL)
            if l == N_A_LAYERS:
                if past is None:
                    kv_new, kn, vt = _kvrows(xf, shared, cos32, sin32, tm=tmk,
                                             seq_tiles=seq_tiles, with_kv=True)
                else:
                    (kv_new,) = _kvrows(xf, shared, cos32, sin32, tm=tmk,
                                        seq_tiles=seq_tiles, with_kv=False)
            if past is None:
                qf, qm = _qproj(xf, wl, shared, cos_t, sin_t, tm=tmk, seq_tiles=seq_tiles,
                                absorb=False)
                tok = _mla_prompt(qf.reshape(b, s, Q_SLOTS), kn.reshape(b, s, Q_SLOTS), vt, tq=256,
                                  nh=12)
                tok = tok.reshape(n, MLA_HEADS * V_HEAD)
                nb_o, ts_o = 1, tmk
            else:
                cache_t, page_table = past
                qabs, qm = _qproj(xf, wl, shared, cos_t, sin_t, tm=tmk, seq_tiles=seq_tiles,
                                  absorb=True)
                if l == N_A_LAYERS:
                    kv_new_t = jnp.pad(jnp.swapaxes(kv_new.reshape(b, s, KV_ROW), 1, 2),
                                       ((0, 0), (0, 0), (0, LANE - s)))
                    tok, kinv = _mla_sample(page_table, qabs, shared['w_ukt'], kv_new_t, cache_t,
                                            None, pps=64)
                else:
                    (tok,) = _mla_sample(page_table, qabs, shared['w_ukt'], kv_new_t, cache_t,
                                         kinv, pps=64)
                tok = tok.reshape(n, MLA_HEADS * KV_LORA)
                nb_o, ts_o = nb, ts
            xf = _out_b(xf, tok, qm, mem_k, mem_v, wl, shared, nb=nb_o, ts=ts_o,
                        latent=past is not None)
        xf = _ffn(xf, wl['g_ffn'], wl['w_ffn_in'], wl['w_ffn_out'], wl['l'], tm=min(tm, n))
        x = xf.reshape(b, s, D_MODEL)
    return x, kv_new.reshape(b, s, KV_ROW), jnp.stack(pool_new)


def kernel(x_prompt, x_sample, cache_kv, state_pool, cache_mem_k, cache_mem_v, page_table, mem_prompt,
           norm_mix, norm_ffn, w_out, w_ffn_in, w_ffn_out, norm_mem, w_mem_kv, g_mem_q, g_mem_k,
           w_in_a, w_pool_grp, pool_scale, w_in_b, g_q_lora, w_uq, g_q,
           norm_kv, w_dkv, g_kv_lora, w_uk, w_uv, g_k_nope, g_k_rope):
    layers, shared = _prep_weights(norm_mix, norm_ffn, w_out, w_ffn_in, w_ffn_out, g_mem_q, w_in_a,
                                   w_pool_grp, pool_scale, w_in_b, g_q_lora, w_uq, g_q, norm_kv,
                                   w_dkv, g_kv_lora, w_uk, w_uv, g_k_nope, g_k_rope)
    b_p, s_p, _ = x_prompt.shape
    n_seq, s_s, _ = x_sample.shape

    mk_p, mv_p = _memkv(mem_prompt, norm_mem[:, None, :],
                        jnp.swapaxes(w_mem_kv, 1, 2).astype(BF16), g_mem_k[:, None, :, None])
    pool0 = jnp.zeros((N_A_LAYERS, b_p, POOL_STATE, POOL_W), F32)
    y_p, kv_p, pool_p = _trunk(x_prompt, 0, pool0, mk_p, mv_p, layers, shared,
                               nb=1, ts=512, tm=512, past=None)

    past_len = page_table.shape[1] * cache_kv.shape[1]
    mk_s = jnp.transpose(cache_mem_k, (0, 1, 3, 4, 2)).reshape(DEPTH, n_seq, MEM_W, N_MEM)
    mv_s = jnp.transpose(cache_mem_v, (0, 1, 3, 4, 2)).reshape(DEPTH, n_seq, MEM_W, N_MEM)
    cache_t = jnp.swapaxes(cache_kv, 1, 2)
    y_s, kv_s, pool_s = _trunk(x_sample, past_len, state_pool, mk_s, mv_s, layers, shared,
                               nb=n_seq // 2, ts=s_s, tm=512, past=(cache_t, page_table))

    def mem_out(a):
        a = a.reshape(DEPTH, b_p, MEM_HEADS, MEM_HEAD_DIM, N_MEM)
        return jnp.transpose(a, (0, 1, 4, 2, 3))

    return (y_p, y_s, kv_p, kv_s, pool_p, pool_s, mem_out(mk_p), mem_out(mv_p))
```

```python
import functools

import numpy as np
import jax
import jax.numpy as jnp
from jax import lax
from jax.experimental import pallas as pl
from jax.experimental.pallas import tpu as pltpu

D_MODEL = 1024
DEPTH = 4
N_A_LAYERS = 2
MEM_HEADS = 4
MEM_HEAD_DIM = 64
MEM_W = MEM_HEADS * MEM_HEAD_DIM
N_MEM = 256
POOL_W = D_MODEL - MEM_W
POOL_WINDOWS = (2, 4, 8, 16)
POOL_GROUP_W = POOL_W // len(POOL_WINDOWS)
POOL_STATE = max(POOL_WINDOWS) - 1
POOL_PAD = POOL_STATE + 1
MLA_HEADS = 12
QK_NOPE = 64
QK_ROPE = 32
QK_DIM = QK_NOPE + QK_ROPE
V_HEAD = 64
KV_LORA = 256
Q_LORA = 384
KV_ROW = KV_LORA + QK_ROPE
ROPE_BASE = 10000.0
D_FF = 2816
EPS = 1e-6
NEG = -1e30
LOG2E = 1.4426950408889634

LANE = 128
SLOT = LANE
Q_SLOTS = MLA_HEADS * SLOT
V_SLOT = V_HEAD + 16
V_SLOTS = MLA_HEADS * V_SLOT
VMEM_LIMIT = 56 * 1024 * 1024

BF16 = jnp.bfloat16
F32 = jnp.float32
NT_DIMS = (((1,), (1,)), ((), ()))


def _params(n_grid, flags=None):
    return pltpu.CompilerParams(dimension_semantics=("arbitrary",) * n_grid,
                                vmem_limit_bytes=VMEM_LIMIT, flags=flags)


def _dot(a, b):
    return jnp.dot(a, b, preferred_element_type=F32)


def _dot_nt(a, b):
    return lax.dot_general(a, b, NT_DIMS, preferred_element_type=F32)


def _split_dot(x, w):
    hi = x.astype(BF16)
    lo = (x - hi.astype(F32)).astype(BF16)
    return _dot(hi, w) + _dot(lo, w)


def _rms(x, g):
    return x * lax.rsqrt(jnp.mean(x * x, axis=-1, keepdims=True) + EPS) * g


def _mem_attend(qm, mkt_ref, mvt_ref, gq, gsum, nb, ts):
    ssq = _split_dot(qm * qm, gsum)
    qn = qm * lax.rsqrt(ssq * (1.0 / MEM_HEAD_DIM) + EPS) * (gq * MEM_HEAD_DIM ** -0.5)
    q3 = qn.reshape(nb, ts, MEM_W)
    mkt = mkt_ref[...].astype(BF16)
    mvt = mvt_ref[...].astype(BF16)
    lane = lax.broadcasted_iota(jnp.int32, (1, 1, MEM_W), 2)
    heads = range(MEM_HEADS)
    in_head = [(lane >= h * MEM_HEAD_DIM) & (lane < (h + 1) * MEM_HEAD_DIM) for h in heads]
    qs = [jnp.where(in_head[h], q3, 0.0).astype(BF16) for h in heads]
    ss = [jnp.einsum('bqd,bdm->bqm', qs[h], mkt, preferred_element_type=F32) for h in heads]
    ps = [jnp.exp(s - jnp.max(s, axis=-1, keepdims=True)) for s in ss]
    ps = [(p / jnp.sum(p, axis=-1, keepdims=True)).astype(BF16) for p in ps]
    os_ = [jnp.einsum('bqm,bdm->bqd', p, mvt, preferred_element_type=F32) for p in ps]
    out = jnp.where(in_head[0], os_[0], 0.0)
    for h in range(1, MEM_HEADS):
        out = jnp.where(in_head[h], os_[h], out)
    return out.reshape(nb * ts, MEM_W)


def _memkv_kernel(mem_ref, gnorm_ref, wt_ref, gk_ref, kt_ref, vt_ref):
    x = mem_ref[0]
    xn = x * lax.rsqrt(jnp.mean(x * x, axis=-1, keepdims=True) + EPS)
    hm = (xn * gnorm_ref[0]).astype(BF16)
    kvt = _dot_nt(wt_ref[0], hm)
    kt = kvt[:MEM_W].reshape(MEM_HEADS, MEM_HEAD_DIM, N_MEM)
    inv = lax.rsqrt(jnp.mean(kt * kt, axis=1, keepdims=True) + EPS)
    kt_ref[0, 0] = (kt * inv * gk_ref[0]).reshape(MEM_W, N_MEM)
    vt_ref[0, 0] = kvt[MEM_W:]


def _memkv(mem, norm_mem, w_mem_kv_t, g_mem_k):
    b = mem.shape[0]
    return pl.pallas_call(
        _memkv_kernel,
        grid=(DEPTH, b),
        in_specs=[
            pl.BlockSpec((1, N_MEM, D_MODEL), lambda l, i: (i, 0, 0)),
            pl.BlockSpec((1, 1, D_MODEL), lambda l, i: (l, 0, 0)),
            pl.BlockSpec((1, 2 * MEM_W, D_MODEL), lambda l, i: (l, 0, 0)),
            pl.BlockSpec((1, 1, MEM_HEAD_DIM, 1), lambda l, i: (l, 0, 0, 0)),
        ],
        out_specs=[pl.BlockSpec((1, 1, MEM_W, N_MEM), lambda l, i: (l, i, 0, 0)),
                   pl.BlockSpec((1, 1, MEM_W, N_MEM), lambda l, i: (l, i, 0, 0))],
        out_shape=[jax.ShapeDtypeStruct((DEPTH, b, MEM_W, N_MEM), F32)] * 2,
        compiler_params=_params(2),
        name="memkv",
    )(mem, norm_mem, w_mem_kv_t, g_mem_k)


def _pool_out_kernel(wp_ref, scale_ref, w_out_ref, o_ref):
    for g in range(len(POOL_WINDOWS)):
        rows = slice(g * POOL_GROUP_W, (g + 1) * POOL_GROUP_W)
        a = wp_ref[0, g]
        b = w_out_ref[0, rows, :] * scale_ref[0, rows, :]
        a_hi, b_hi = a.astype(BF16), b.astype(BF16)
        a_lo = (a - a_hi.astype(F32)).astype(BF16)
        b_lo = (b - b_hi.astype(F32)).astype(BF16)
        o_ref[0, rows, :] = (_dot(a_hi, b_hi) + _dot(a_hi, b_lo) + _dot(a_lo, b_hi)).astype(BF16)


def _pool_out_weights(w_pool_grp, pool_scale, w_out):
    n_groups = len(POOL_WINDOWS)
    return pl.pallas_call(
        _pool_out_kernel,
        grid=(N_A_LAYERS,),
        in_specs=[
            pl.BlockSpec((1, n_groups, POOL_GROUP_W, POOL_GROUP_W), lambda l: (l, 0, 0, 0)),
            pl.BlockSpec((1, POOL_W, 1), lambda l: (l, 0, 0)),
            pl.BlockSpec((1, POOL_W, D_MODEL), lambda l: (l, 0, 0)),
        ],
        out_specs=pl.BlockSpec((1, POOL_W, D_MODEL), lambda l: (l, 0, 0)),
        out_shape=jax.ShapeDtypeStruct((N_A_LAYERS, POOL_W, D_MODEL), BF16),
        compiler_params=_params(1),
        name="pool_out_weights",
    )(w_pool_grp, pool_scale[:, :, None], w_out)


def _mixer_a_kernel(x_ref, g_ref, w_in_ref, prev_ref, wpo_ref,
                    mk_ref, mv_ref, gq_ref, gsum_ref, w_out_ref,
                    xo_ref, pool_ref, ext_ref, *, nb, ts, pos0):
    j = pl.program_id(1)
    rows = nb * ts
    x = x_ref[...].reshape(rows, D_MODEL)
    h = _rms(x, g_ref[...]).astype(BF16)
    proj = _dot(h, w_in_ref[...])
    u = proj[:, :POOL_W]
    qm = proj[:, POOL_W:]

    @pl.when(j == 0)
    def _():
        ext_ref[:, 0:POOL_PAD, :] = jnp.zeros((nb, POOL_PAD, POOL_W), F32)
        ext_ref[:, 1:POOL_PAD, :] = prev_ref[...]

    @pl.when(j > 0)
    def _():
        ext_ref[:, 0:POOL_PAD, :] = ext_ref[:, ts:ts + POOL_PAD, :]

    ext_ref[:, POOL_PAD:, :] = u.reshape(nb, ts, POOL_W)
    e = ext_ref[...].reshape(nb * (POOL_PAD + ts), POOL_W)
    s2 = e + pltpu.roll(e, 1, 0)
    s4 = s2 + pltpu.roll(s2, 2, 0)
    s8 = s4 + pltpu.roll(s4, 4, 0)
    s16 = s8 + pltpu.roll(s8, 8, 0)

    def tile_rows(a):
        return a.reshape(nb, POOL_PAD + ts, POOL_W)[:, POOL_PAD:, :]

    t_idx = lax.broadcasted_iota(jnp.int32, (1, ts, 1), 1)
    pos1 = (pos0 + 1 + j * ts + t_idx).astype(F32)
    col = lax.broadcasted_iota(jnp.int32, (1, 1, POOL_W), 2)
    pooled = None
    for g, (w, sw) in enumerate(zip(POOL_WINDOWS, (s2, s4, s8, s16))):
        val = tile_rows(sw) / jnp.minimum(float(w), pos1)
        if pooled is None:
            pooled = val
        else:
            pooled = jnp.where(col >= g * POOL_GROUP_W, val, pooled)
    diff = (pooled.reshape(rows, POOL_W) - u).astype(BF16)

    mo = _mem_attend(qm, mk_ref, mv_ref, gq_ref[...], gsum_ref[...], nb, ts)

    y = _dot(diff, wpo_ref[...]) + _dot(mo.astype(BF16), w_out_ref[POOL_W:, :])
    xo_ref[...] = (x + y).reshape(nb, ts, D_MODEL)

    @pl.when(j == pl.num_programs(1) - 1)
    def _():
        pool_ref[...] = ext_ref[:, ts + 1:ts + POOL_PAD, :]


def _mixer_a(x, prev, mk, mv, wl, *, nb, ts, pos0):
    b, s, _ = x.shape
    kern = functools.partial(_mixer_a_kernel, nb=nb, ts=ts, pos0=pos0)
    const = lambda i, j: (0, 0)
    return pl.pallas_call(
        kern,
        grid=(b // nb, s // ts),
        in_specs=[
            pl.BlockSpec((nb, ts, D_MODEL), lambda i, j: (i, j, 0)),
            pl.BlockSpec((1, D_MODEL), const),
            pl.BlockSpec((None, D_MODEL, D_MODEL), lambda i, j: (wl['l_in'], 0, 0)),
            pl.BlockSpec((nb, POOL_STATE, POOL_W), lambda i, j: (i, 0, 0)),
            pl.BlockSpec((None, POOL_W, D_MODEL), lambda i, j: (wl['l_in'], 0, 0)),
            pl.BlockSpec((None, nb, MEM_W, N_MEM), lambda i, j: (wl['l'], i, 0, 0)),
            pl.BlockSpec((None, nb, MEM_W, N_MEM), lambda i, j: (wl['l'], i, 0, 0)),
            pl.BlockSpec((1, MEM_W), const),
            pl.BlockSpec((MEM_W, MEM_W), const),
            pl.BlockSpec((None, D_MODEL, D_MODEL), lambda i, j: (wl['l'], 0, 0)),
        ],
        out_specs=[pl.BlockSpec((nb, ts, D_MODEL), lambda i, j: (i, j, 0)),
                   pl.BlockSpec((nb, POOL_STATE, POOL_W), lambda i, j: (i, 0, 0))],
        out_shape=[jax.ShapeDtypeStruct((b, s, D_MODEL), F32),
                   jax.ShapeDtypeStruct((b, POOL_STATE, POOL_W), F32)],
        scratch_shapes=[pltpu.VMEM((nb, POOL_PAD + ts, POOL_W), F32)],
        compiler_params=_params(2),
        name="mixer_a",
    )(x, wl['g_mix'], wl['w_in'], prev, wl['w_pool_out'],
      mk, mv, wl['g_mem_q'], wl['gsum'], wl['w_out'])


FFN_CHUNK = 1408


def _ffn_kernel(x_ref, g_ref, w_in_ref, w_out_ref, o_ref):
    x = x_ref[...]
    h = _rms(x, g_ref[...]).astype(BF16)
    acc = x
    for c0 in range(0, D_FF, FFN_CHUNK):
        gate = _dot(h, w_in_ref[:, c0:c0 + FFN_CHUNK])
        up = _dot(h, w_in_ref[:, D_FF + c0:D_FF + c0 + FFN_CHUNK])
        a = (gate * jax.nn.sigmoid(gate) * up).astype(BF16)
        acc = acc + _dot(a, w_out_ref[c0:c0 + FFN_CHUNK, :])
    o_ref[...] = acc


def _ffn(x, g, w_in, w_out, layer, *, tm):
    n = x.shape[0]
    const = lambda i: (0, 0)
    return pl.pallas_call(
        _ffn_kernel,
        grid=(n // tm,),
        in_specs=[
            pl.BlockSpec((tm, D_MODEL), lambda i: (i, 0)),
            pl.BlockSpec((1, D_MODEL), const),
            pl.BlockSpec((None, D_MODEL, 2 * D_FF), lambda i: (layer, 0, 0),
                         pipeline_mode=pl.Buffered(1)),
            pl.BlockSpec((None, D_FF, D_MODEL), lambda i: (layer, 0, 0),
                         pipeline_mode=pl.Buffered(1)),
        ],
        out_specs=pl.BlockSpec((tm, D_MODEL), lambda i: (i, 0)),
        out_shape=jax.ShapeDtypeStruct((n, D_MODEL), F32),
        compiler_params=_params(1),
        name="ffn",
    )(x, g, w_in, w_out)


def _kvrows_kernel(x_ref, g_ref, w_ref, gl_ref, cos_ref, sin_ref, *rest, with_kv):
    if with_kv:
        wk_ref, wv_ref, kv_ref, kn_ref, v_ref = rest
    else:
        (kv_ref,) = rest
    x = x_ref[...]
    h = _rms(x, g_ref[...]).astype(BF16)
    ck = _dot(h, w_ref[...])
    c = _rms(ck[:, :KV_LORA], gl_ref[...])
    kr = (ck[:, KV_LORA:KV_ROW] * cos_ref[...]
          + ck[:, KV_LORA + LANE:KV_LORA + LANE + QK_ROPE] * sin_ref[...])
    kv_ref[:, 0:KV_LORA] = c
    kv_ref[:, KV_LORA:KV_ROW] = kr
    if with_kv:
        cb = c.astype(BF16)
        kn = _dot(cb, wk_ref[0:KV_LORA, :]) + _dot(kr.astype(BF16), wk_ref[KV_LORA:KV_ROW, :])
        is_k = lax.broadcasted_iota(jnp.int32, (1, SLOT), 1) < QK_DIM
        for hd in range(MLA_HEADS):
            sl = slice(hd * SLOT, (hd + 1) * SLOT)
            kh = kn[:, sl]
            ssq = jnp.sum(jnp.where(is_k, kh * kh, 0.0), axis=-1, keepdims=True)
            inv = lax.rsqrt(ssq * (1.0 / QK_DIM) + EPS)
            kn_ref[:, sl] = (kh * inv).astype(BF16)
        vt = _dot_nt(wv_ref[...], cb)
        row = lax.broadcasted_iota(jnp.int32, (V_SLOTS, 1), 0)
        is_one = functools.reduce(
            jnp.logical_or, [row == hd * V_SLOT + V_HEAD for hd in range(MLA_HEADS)])
        v_ref[0] = jnp.where(is_one, 1.0, vt).astype(BF16)


def _kvrows(x, wk, cos32, sin32, *, tm, seq_tiles, with_kv):
    n = x.shape[0]
    const = lambda i: (0, 0)
    in_specs = [
        pl.BlockSpec((tm, D_MODEL), lambda i: (i, 0)),
        pl.BlockSpec((1, D_MODEL), const),
        pl.BlockSpec((D_MODEL, 4 * LANE), const),
        pl.BlockSpec((1, KV_LORA), const),
        pl.BlockSpec((tm, QK_ROPE), lambda i: (i % seq_tiles, 0)),
        pl.BlockSpec((tm, QK_ROPE), lambda i: (i % seq_tiles, 0)),
    ]
    args = [x, wk['norm_kv'], wk['w_dkv_ext'], wk['g_kv_lora'], cos32, sin32]
    out_specs = [pl.BlockSpec((tm, KV_ROW), lambda i: (i, 0))]
    out_shape = [jax.ShapeDtypeStruct((n, KV_ROW), F32)]
    if with_kv:
        in_specs += [pl.BlockSpec((KV_ROW, Q_SLOTS), const),
                     pl.BlockSpec((V_SLOTS, KV_LORA), const)]
        args += [wk['w_kslots'], wk['w_uvt_slots']]
        out_specs += [pl.BlockSpec((tm, Q_SLOTS), lambda i: (i, 0)),
                      pl.BlockSpec((1, V_SLOTS, tm), lambda i: (i // seq_tiles, 0, i % seq_tiles))]
        out_shape += [jax.ShapeDtypeStruct((n, Q_SLOTS), BF16),
                      jax.ShapeDtypeStruct((n // (tm * seq_tiles), V_SLOTS, tm * seq_tiles), BF16)]
    return pl.pallas_call(
        functools.partial(_kvrows_kernel, with_kv=with_kv),
        grid=(n // tm,),
        in_specs=in_specs, out_specs=out_specs, out_shape=out_shape,
        compiler_params=_params(1),
        name="kvrows",
    )(*args)


def _qproj_kernel(x_ref, g_ref, w_in_ref, gql_ref, wq_ref, gq_ref, trig_ref, *rest, absorb):
    if absorb:
        wabs_ref, qabs_ref, qm_ref = rest
    else:
        qf_ref, qm_ref = rest
    x = x_ref[...]
    h = _rms(x, g_ref[...]).astype(BF16)
    proj = _dot(h, w_in_ref[...])
    qm_ref[...] = proj[:, Q_LORA:]
    cq = _rms(proj[:, :Q_LORA], gql_ref[...]).astype(BF16)
    q_all = _dot(cq, wq_ref[...])
    tab = gq_ref[...] * trig_ref[...]
    is_q = lax.broadcasted_iota(jnp.int32, (1, SLOT), 1) < QK_DIM
    for hd in range(MLA_HEADS):
        sl = slice(hd * SLOT, (hd + 1) * SLOT)
        qh = q_all[:, sl]
        ssq = jnp.sum(jnp.where(is_q, qh * qh, 0.0), axis=-1, keepdims=True)
        inv = lax.rsqrt(ssq * (1.0 / QK_DIM) + EPS)
        qf = (qh * tab * (inv * (QK_DIM ** -0.5 * LOG2E))).astype(BF16)
        if absorb:
            qabs_ref[hd] = _dot(qf, wabs_ref[hd])
        else:
            qf_ref[:, sl] = qf


def _qproj(x, wl, wk, trig_t, *, tm, seq_tiles, absorb):
    n = x.shape[0]
    const = lambda i: (0, 0)
    in_specs = [
        pl.BlockSpec((tm, D_MODEL), lambda i: (i, 0)),
        pl.BlockSpec((1, D_MODEL), const),
        pl.BlockSpec((None, D_MODEL, Q_LORA + MEM_W), lambda i: (wl['l_in'], 0, 0)),
        pl.BlockSpec((1, Q_LORA), const),
        pl.BlockSpec((Q_LORA, Q_SLOTS), const),
        pl.BlockSpec((1, SLOT), const),
        pl.BlockSpec((tm, SLOT), lambda i: (i % seq_tiles, 0)),
    ]
    args = [x, wl['g_mix'], wl['w_in'], wl['g_q_lora'], wl['w_uq_slots'], wl['g_q_slot'],
            trig_t]
    if absorb:
        in_specs.append(pl.BlockSpec((MLA_HEADS, SLOT, KV_ROW), lambda i: (0, 0, 0)))
        args.append(wk['w_abs'])
        out_specs = [pl.BlockSpec((MLA_HEADS, tm, KV_ROW), lambda i: (0, i, 0))]
        out_shape = [jax.ShapeDtypeStruct((MLA_HEADS, n, KV_ROW), F32)]
    else:
        out_specs = [pl.BlockSpec((tm, Q_SLOTS), lambda i: (i, 0))]
        out_shape = [jax.ShapeDtypeStruct((n, Q_SLOTS), BF16)]
    out_specs.append(pl.BlockSpec((tm, MEM_W), lambda i: (i, 0)))
    out_shape.append(jax.ShapeDtypeStruct((n, MEM_W), F32))
    return pl.pallas_call(
        functools.partial(_qproj_kernel, absorb=absorb),
        grid=(n // tm,),
        in_specs=in_specs, out_specs=out_specs, out_shape=out_shape,
        compiler_params=_params(1),
        name="qproj",
    )(*args)


def _mla_prompt_kernel(q_ref, k_ref, vt_ref, o_ref, *, tq, nh):
    qi = pl.program_id(2)
    q = q_ref[0]
    q_idx = qi * tq + lax.broadcasted_iota(jnp.int32, (1, tq), 1)
    heads = range(nh)
    slots = [slice(hd * SLOT, (hd + 1) * SLOT) for hd in heads]

    def block(kb, carry, masked):
        start = pl.multiple_of(kb * tq, tq)
        k = k_ref[0, pl.ds(start, tq), :]
        vt = vt_ref[0, :, pl.ds(start, tq)]
        sts = [_dot_nt(k[:, sl], q[:, sl]) for sl in slots]
        if masked:
            k_idx = start + lax.broadcasted_iota(jnp.int32, (tq, 1), 0)
            sts = [jnp.where(k_idx <= q_idx, st, NEG) for st in sts]
        ms = [jnp.maximum(carry[2 * hd], jnp.max(sts[hd], axis=0, keepdims=True))
              for hd in heads]
        ps = [jnp.exp2(sts[hd] - ms[hd]).astype(BF16) for hd in heads]
        pvs = [_dot(vt[hd * V_SLOT:(hd + 1) * V_SLOT, :], ps[hd]) for hd in heads]
        new = []
        for hd in heads:
            alpha = jnp.exp2(carry[2 * hd] - ms[hd])
            new += [ms[hd], carry[2 * hd + 1] * alpha + pvs[hd]]
        return tuple(new)

    init = (jnp.full((1, tq), NEG, F32), jnp.zeros((V_SLOT, tq), F32)) * nh
    carry = lax.fori_loop(0, qi, lambda kb, c: block(kb, c, False), init)
    carry = block(qi, carry, True)
    outs = [carry[2 * hd + 1][:V_HEAD] / carry[2 * hd + 1][V_HEAD:V_HEAD + 1] for hd in heads]
    o_ref[0] = jnp.concatenate(outs, axis=0).T.astype(BF16)


def _mla_prompt(qf, kn, vt, *, tq, nh):
    b, s, _ = qf.shape
    return pl.pallas_call(
        functools.partial(_mla_prompt_kernel, tq=tq, nh=nh),
        grid=(b, MLA_HEADS // nh, s // tq),
        in_specs=[
            pl.BlockSpec((1, tq, nh * SLOT), lambda i, hg, qi: (i, qi, hg)),
            pl.BlockSpec((1, s, nh * SLOT), lambda i, hg, qi: (i, 0, hg)),
            pl.BlockSpec((1, nh * V_SLOT, s), lambda i, hg, qi: (i, hg, 0)),
        ],
        out_specs=pl.BlockSpec((1, tq, nh * V_HEAD), lambda i, hg, qi: (i, qi, hg)),
        out_shape=jax.ShapeDtypeStruct((b, s, MLA_HEADS * V_HEAD), BF16),
        compiler_params=_params(3),
        name="mla_prompt",
    )(qf, kn, vt)


Q_ROWS = MLA_HEADS * 8
N_KN = MLA_HEADS * QK_NOPE
SAMPLE_SUB_PAGES = 4


def _mla_sample_kernel(pt_ref, qabs_ref, wukt_ref, kvnew_ref, *rest, pps, sub_pages, n_tok,
                       first_layer):
    page_refs = rest[:pps]
    rest = rest[pps:]
    if first_layer:
        o_ref, kinv_out_ref, m_ref, l_ref, acc_ref, lhs_ref = rest
    else:
        kinv_in_ref, o_ref, m_ref, l_ref, acc_ref, lhs_ref = rest
    step = pl.program_id(1)
    qabs = qabs_ref[...].reshape(Q_ROWS, KV_ROW)
    q_rope = qabs[:, KV_LORA:].astype(BF16)

    @pl.when(step == 0)
    def _():
        m_ref[...] = jnp.full(m_ref.shape, NEG, F32)
        l_ref[...] = jnp.zeros(l_ref.shape, F32)
        acc_ref[...] = jnp.zeros(acc_ref.shape, F32)
        lhs_ref[0:N_KN, :] = wukt_ref[...]
        lhs_ref[N_KN:, :] = qabs[:, :KV_LORA].astype(BF16)

    def scores(ct, krt, kinv, causal):
        t = ct.shape[1]
        if kinv is None:
            r1 = _dot(lhs_ref[...], ct)
            kn = r1[:N_KN]
            ss = jnp.sum((kn * kn).reshape(MLA_HEADS, QK_NOPE, t), axis=1)
            krsq = jnp.sum(krt * krt, axis=0, keepdims=True)
            kinv = lax.rsqrt((ss + krsq) * (1.0 / QK_DIM) + EPS)
            s_lat = r1[N_KN:]
        else:
            s_lat = _dot(lhs_ref[N_KN:, :], ct)
        sc = s_lat + _dot(q_rope, krt.astype(BF16))
        sc = sc.reshape(MLA_HEADS, n_tok, t) * kinv[:, None, :]
        if causal:
            tok = lax.broadcasted_iota(jnp.int32, (1, n_tok, t), 1)
            key = lax.broadcasted_iota(jnp.int32, (1, n_tok, t), 2)
            sc = jnp.where(key <= tok, sc, NEG)
        return sc.reshape(Q_ROWS, t), kinv

    def accumulate(state, sc, ct):
        m_prev, l_prev, acc = state
        m_new = jnp.maximum(m_prev, jnp.max(sc, axis=-1, keepdims=True))
        alpha = jnp.exp2(m_prev - m_new)
        p = jnp.exp2(sc - m_new)
        l_new = alpha * l_prev + jnp.sum(p, axis=-1, keepdims=True)
        return m_new, l_new, alpha * acc + _dot_nt(p.astype(BF16), ct)

    n_sub = pps // sub_pages
    t_sub = sub_pages * page_refs[0].shape[2]

    def sub_scores(j):
        refs = page_refs[j * sub_pages:(j + 1) * sub_pages]
        ct = jnp.concatenate([r[0, 0:KV_LORA, :].astype(BF16) for r in refs], axis=1)
        krt = jnp.concatenate([r[0, KV_LORA:, :] for r in refs], axis=1)
        if first_layer:
            sc, kinv = scores(ct, krt, None, False)
            kinv_out_ref[0, :, j * t_sub:(j + 1) * t_sub] = kinv
        else:
            sc, _ = scores(ct, krt, kinv_in_ref[0, :, j * t_sub:(j + 1) * t_sub], False)
        return sc, ct

    if first_layer:
        state = (m_ref[...], l_ref[...], acc_ref[...])
        pending = sub_scores(0)
        for j in range(1, n_sub):
            nxt = sub_scores(j)
            state = accumulate(state, *pending)
            pending = nxt
        m_ref[...], l_ref[...], acc_ref[...] = accumulate(state, *pending)
    else:
        subs = [sub_scores(j) for j in range(n_sub)]
        m_prev = m_ref[...]
        m_blk = functools.reduce(jnp.maximum, [sc for sc, _ in subs])
        m_new = jnp.maximum(m_prev, jnp.max(m_blk, axis=-1, keepdims=True))
        alpha = jnp.exp2(m_prev - m_new)
        ps = [jnp.exp2(sc - m_new) for sc, _ in subs]
        l_ref[...] = alpha * l_ref[...] + sum(jnp.sum(p, axis=-1, keepdims=True) for p in ps)
        pv = sum(_dot_nt(p.astype(BF16), ct) for p, (_, ct) in zip(ps, subs))
        acc_ref[...] = alpha * acc_ref[...] + pv
        m_ref[...] = m_new

    @pl.when(step == pl.num_programs(1) - 1)
    def _():
        kvt = kvnew_ref[0]
        ct = kvt[:KV_LORA].astype(BF16)
        sc, _ = scores(ct, kvt[KV_LORA:], None, True)
        _, l_fin, acc = accumulate((m_ref[...], l_ref[...], acc_ref[...]), sc, ct)
        o = acc / l_fin
        for hd in range(MLA_HEADS):
            o_ref[0, :, hd * KV_LORA:(hd + 1) * KV_LORA] = o[hd * n_tok:(hd + 1) * n_tok, :]


def _mla_sample(page_table, qabs, w_ukt, kv_new_t, cache_t, kinv, *, pps):
    n_seq, n_pages = page_table.shape
    n_tok = qabs.shape[1] // n_seq
    page = cache_t.shape[2]
    first_layer = kinv is None
    t_step = pps * page
    assert n_tok == 8 and n_pages % pps == 0

    def page_spec(i):
        return pl.BlockSpec((1, KV_ROW, page), lambda b, s, pt: (pt[b, s * pps + i], 0, 0))

    in_specs = [
        pl.BlockSpec((MLA_HEADS, 1, n_tok, KV_ROW), lambda b, s, pt: (0, b, 0, 0)),
        pl.BlockSpec((N_KN, KV_LORA), lambda b, s, pt: (0, 0)),
        pl.BlockSpec((1, KV_ROW, LANE), lambda b, s, pt: (b, 0, 0)),
    ] + [page_spec(i) for i in range(pps)]
    args = [qabs.reshape(MLA_HEADS, n_seq, n_tok, KV_ROW), w_ukt, kv_new_t] + [cache_t] * pps
    out_specs = [pl.BlockSpec((1, n_tok, MLA_HEADS * KV_LORA), lambda b, s, pt: (b, 0, 0))]
    out_shape = [jax.ShapeDtypeStruct((n_seq, n_tok, MLA_HEADS * KV_LORA), F32)]
    kinv_spec = pl.BlockSpec((1, MLA_HEADS, t_step), lambda b, s, pt: (b, 0, s))
    if first_layer:
        out_specs.append(kinv_spec)
        out_shape.append(jax.ShapeDtypeStruct((n_seq, MLA_HEADS, n_pages * page), F32))
    else:
        in_specs.append(kinv_spec)
        args.append(kinv)
    grid_spec = pltpu.PrefetchScalarGridSpec(
        num_scalar_prefetch=1,
        grid=(n_seq, n_pages // pps),
        in_specs=in_specs,
        out_specs=out_specs,
        scratch_shapes=[
            pltpu.VMEM((Q_ROWS, 1), F32),
            pltpu.VMEM((Q_ROWS, 1), F32),
            pltpu.VMEM((Q_ROWS, KV_LORA), F32),
            pltpu.VMEM((N_KN + Q_ROWS, KV_LORA), BF16),
        ],
    )
    return pl.pallas_call(
        functools.partial(_mla_sample_kernel, pps=pps, sub_pages=SAMPLE_SUB_PAGES, n_tok=n_tok,
                          first_layer=first_layer),
        grid_spec=grid_spec,
        out_shape=out_shape,
        compiler_params=_params(2),
        name="mla_sample",
    )(page_table, *args)


def _out_b_kernel(x_ref, tok_ref, qm_ref, mk_ref, mv_ref, gq_ref, gsum_ref, w_out_ref,
                  *rest, nb, ts, latent):
    if latent:
        wuv_ref, o_ref = rest
        tok = _dot(tok_ref[...].astype(BF16), wuv_ref[...]).astype(BF16)
    else:
        (o_ref,) = rest
        tok = tok_ref[...]
    mo = _mem_attend(qm_ref[...], mk_ref, mv_ref, gq_ref[...], gsum_ref[...], nb, ts)
    y = _dot(tok, w_out_ref[0:POOL_W, :]) + _dot(mo.astype(BF16), w_out_ref[POOL_W:, :])
    o_ref[...] = x_ref[...] + y


def _out_b(x, tok, qm, mk, mv, wl, wk, *, nb, ts, latent):
    n = x.shape[0]
    rows = nb * ts
    tok_w = tok.shape[1]
    const = lambda i: (0, 0)
    seq_len = n // mk.shape[1]
    mem_idx = lambda i: (wl['l'], i * rows // (seq_len * nb), 0, 0)
    in_specs = [
        pl.BlockSpec((rows, D_MODEL), lambda i: (i, 0)),
        pl.BlockSpec((rows, tok_w), lambda i: (i, 0)),
        pl.BlockSpec((rows, MEM_W), lambda i: (i, 0)),
        pl.BlockSpec((None, nb, MEM_W, N_MEM), mem_idx),
        pl.BlockSpec((None, nb, MEM_W, N_MEM), mem_idx),
        pl.BlockSpec((1, MEM_W), const),
        pl.BlockSpec((MEM_W, MEM_W), const),
        pl.BlockSpec((None, D_MODEL, D_MODEL), lambda i: (wl['l'], 0, 0)),
    ]
    args = [x, tok, qm, mk, mv, wl['g_mem_q'], wl['gsum'], wl['w_out']]
    if latent:
        in_specs.append(pl.BlockSpec((MLA_HEADS * KV_LORA, MLA_HEADS * V_HEAD), const))
        args.append(wk['w_uv_bd'])
    return pl.pallas_call(
        functools.partial(_out_b_kernel, nb=nb, ts=ts, latent=latent),
        grid=(n // rows,),
        in_specs=in_specs,
        out_specs=pl.BlockSpec((rows, D_MODEL), lambda i: (i, 0)),
        out_shape=jax.ShapeDtypeStruct((n, D_MODEL), F32),
        compiler_params=_params(1),
        name="out_b",
    )(*args)


def _rope_tables(pos):
    half = QK_ROPE // 2
    inv_freq = ROPE_BASE ** (-jnp.arange(half, dtype=F32) / half)
    ang = pos.astype(F32)[:, None] * inv_freq[None, :]
    cos32 = jnp.concatenate([jnp.cos(ang)] * 2, axis=-1)
    sin32 = jnp.concatenate([jnp.sin(ang)] * 2, axis=-1)
    n = pos.shape[0]
    trig_t = jnp.concatenate([jnp.ones((n, QK_NOPE), F32), cos32, sin32], axis=-1)
    return cos32, sin32, trig_t


def _rot_cols(w):
    half = QK_ROPE // 2
    return jnp.concatenate([-w[..., half:], w[..., :half]], axis=-1)


def _prep_weights(norm_mix, norm_ffn, w_out, w_ffn_in, w_ffn_out, g_mem_q, w_in_a, w_pool_grp,
                  pool_scale, w_in_b, g_q_lora, w_uq, g_q, norm_kv, w_dkv, g_kv_lora, w_uk, w_uv,
                  g_k_nope, g_k_rope):
    half = QK_ROPE // 2
    gsum = jnp.asarray(np.kron(np.eye(MEM_HEADS), np.ones((MEM_HEAD_DIM, MEM_HEAD_DIM))), BF16)
    w_out_b, w_ffn_in_b, w_ffn_out_b = (w.astype(BF16) for w in (w_out, w_ffn_in, w_ffn_out))
    w_in_a_b, w_in_b_b = w_in_a.astype(BF16), w_in_b.astype(BF16)
    w_pool_out = _pool_out_weights(w_pool_grp, pool_scale, w_out)
    layers = []
    for l in range(DEPTH):
        wl = dict(
            l=l, g_mix=norm_mix[l][None], g_ffn=norm_ffn[l][None],
            w_out=w_out_b, w_ffn_in=w_ffn_in_b, w_ffn_out=w_ffn_out_b,
            g_mem_q=jnp.tile(g_mem_q[l], MEM_HEADS)[None], gsum=gsum)
        if l < N_A_LAYERS:
            wl.update(w_in=w_in_a_b, l_in=l, w_pool_out=w_pool_out)
        else:
            j = l - N_A_LAYERS
            wq = w_uq[j].reshape(Q_LORA, MLA_HEADS, QK_DIM)
            w_slots = jnp.concatenate([wq, _rot_cols(wq[..., QK_NOPE:])],
                                      axis=-1).reshape(Q_LORA, Q_SLOTS)
            gk2 = jnp.concatenate([g_k_rope, g_k_rope])
            gq = g_q[j]
            g_rot = jnp.concatenate([gq[QK_NOPE + half:], gq[QK_NOPE:QK_NOPE + half]])
            g_slot = jnp.concatenate([gq[:QK_NOPE] * g_k_nope, gq[QK_NOPE:] * gk2, g_rot * gk2])
            wl.update(w_in=w_in_b_b, l_in=j, g_q_lora=g_q_lora[j][None],
                      w_uq_slots=w_slots.astype(BF16), g_q_slot=g_slot[None])
        layers.append(wl)

    zcol = jnp.zeros((D_MODEL, LANE - QK_ROPE), F32)
    w_dkv_ext = jnp.concatenate([w_dkv, zcol, _rot_cols(w_dkv[:, KV_LORA:]), zcol], axis=-1)
    wuk_h = w_uk.reshape(KV_LORA, MLA_HEADS, QK_NOPE)
    top = jnp.concatenate([wuk_h, jnp.zeros((KV_LORA, MLA_HEADS, SLOT - QK_NOPE), F32)], axis=-1)
    eye = jnp.eye(QK_ROPE, dtype=F32)
    bot = jnp.concatenate([jnp.zeros((QK_ROPE, QK_NOPE), F32), eye, eye], axis=-1)
    bot = jnp.broadcast_to(bot[:, None, :], (QK_ROPE, MLA_HEADS, SLOT))
    w_kslots = jnp.concatenate([top, bot], axis=0).reshape(KV_ROW, Q_SLOTS)
    w_abs = jnp.zeros((MLA_HEADS, SLOT, KV_ROW), F32)
    w_abs = w_abs.at[:, :QK_NOPE, :KV_LORA].set(jnp.transpose(wuk_h, (1, 2, 0)))
    eyes = jnp.broadcast_to(eye, (MLA_HEADS,) + eye.shape)
    w_abs = w_abs.at[:, QK_NOPE:QK_DIM, KV_LORA:].set(eyes).at[:, QK_DIM:, KV_LORA:].set(eyes)
    wuv_h = w_uv.reshape(KV_LORA, MLA_HEADS, V_HEAD)
    w_uvt_slots = jnp.concatenate(
        [jnp.transpose(wuv_h, (1, 2, 0)), jnp.zeros((MLA_HEADS, V_SLOT - V_HEAD, KV_LORA), F32)],
        axis=1).reshape(V_SLOTS, KV_LORA)
    w_uv_bd = jnp.einsum('chd,hg->hcgd', wuv_h, jnp.eye(MLA_HEADS, dtype=F32))
    w_uv_bd = w_uv_bd.reshape(MLA_HEADS * KV_LORA, MLA_HEADS * V_HEAD)
    shared = dict(
        norm_kv=norm_kv[None], g_kv_lora=g_kv_lora[None],
        w_dkv_ext=w_dkv_ext.astype(BF16), w_kslots=w_kslots.astype(BF16),
        w_uvt_slots=w_uvt_slots.astype(BF16), w_abs=w_abs.astype(BF16),
        w_ukt=jnp.transpose(w_uk).astype(BF16), w_uv_bd=w_uv_bd.astype(BF16))
    return layers, shared


def _trunk(x, pos0, pool_prev, mem_k, mem_v, layers, shared, *, nb, ts, tm, past):
    b, s, _ = x.shape
    n = b * s
    pos = pos0 + jnp.arange(s)
    tabs = _rope_tables(pos)
    if s >= tm:
        tmk, seq_tiles = tm, s // tm
    else:
        reps = min(tm, n) // s
        tmk, seq_tiles = reps * s, 1
        tabs = tuple(jnp.tile(t, (reps, 1)) for t in tabs)
    cos32, sin32, trig_t = tabs
    pool_new = []
    kv_new = None
    for l in range(DEPTH):
        wl = layers[l]
        if l < N_A_LAYERS:
            x, st = _mixer_a(x, pool_prev[l], mem_k, mem_v, wl, nb=nb, ts=ts, pos0=pos0)
            pool_new.append(st)
            xf = x.reshape(n, D_MODEL)
        else:
            xf = x.reshape(n, D_MODEL)
            if l == N_A_LAYERS:
                if past is None:
                    kv_new, kn, vt = _kvrows(xf, shared, cos32, sin32, tm=tmk,
                                             seq_tiles=seq_tiles, with_kv=True)
                else:
                    (kv_new,) = _kvrows(xf, shared, cos32, sin32, tm=tmk,
                                        seq_tiles=seq_tiles, with_kv=False)
            if past is None:
                qf, qm = _qproj(xf, wl, shared, trig_t, tm=tmk, seq_tiles=seq_tiles,
                                absorb=False)
                tok = _mla_prompt(qf.reshape(b, s, Q_SLOTS), kn.reshape(b, s, Q_SLOTS), vt, tq=256,
                                  nh=12)
                tok = tok.reshape(n, MLA_HEADS * V_HEAD)
                nb_o, ts_o = 1, tmk
            else:
                cache_t, page_table = past
                qabs, qm = _qproj(xf, wl, shared, trig_t, tm=tmk, seq_tiles=seq_tiles,
                                  absorb=True)
                if l == N_A_LAYERS:
                    kv_new_t = jnp.pad(jnp.swapaxes(kv_new.reshape(b, s, KV_ROW), 1, 2),
                                       ((0, 0), (0, 0), (0, LANE - s)))
                    tok, kinv = _mla_sample(page_table, qabs, shared['w_ukt'], kv_new_t, cache_t,
                                            None, pps=64)
                else:
                    (tok,) = _mla_sample(page_table, qabs, shared['w_ukt'], kv_new_t, cache_t,
                                         kinv, pps=64)
                tok = tok.reshape(n, MLA_HEADS * KV_LORA)
                nb_o, ts_o = nb, ts
            xf = _out_b(xf, tok, qm, mem_k, mem_v, wl, shared, nb=nb_o, ts=ts_o,
                        latent=past is not None)
        xf = _ffn(xf, wl['g_ffn'], wl['w_ffn_in'], wl['w_ffn_out'], wl['l'], tm=min(tm, n))
        x = xf.reshape(b, s, D_MODEL)
    return x, kv_new.reshape(b, s, KV_ROW), jnp.stack(pool_new)


def kernel(x_prompt, x_sample, cache_kv, state_pool, cache_mem_k, cache_mem_v, page_table, mem_prompt,
           norm_mix, norm_ffn, w_out, w_ffn_in, w_ffn_out, norm_mem, w_mem_kv, g_mem_q, g_mem_k,
           w_in_a, w_pool_grp, pool_scale, w_in_b, g_q_lora, w_uq, g_q,
           norm_kv, w_dkv, g_kv_lora, w_uk, w_uv, g_k_nope, g_k_rope):
    layers, shared = _prep_weights(norm_mix, norm_ffn, w_out, w_ffn_in, w_ffn_out, g_mem_q, w_in_a,
                                   w_pool_grp, pool_scale, w_in_b, g_q_lora, w_uq, g_q, norm_kv,
                                   w_dkv, g_kv_lora, w_uk, w_uv, g_k_nope, g_k_rope)
    b_p, s_p, _ = x_prompt.shape
    n_seq, s_s, _ = x_sample.shape

    mk_p, mv_p = _memkv(mem_prompt, norm_mem[:, None, :],
                        jnp.swapaxes(w_mem_kv, 1, 2).astype(BF16), g_mem_k[:, None, :, None])
    pool0 = jnp.zeros((N_A_LAYERS, b_p, POOL_STATE, POOL_W), F32)
    y_p, kv_p, pool_p = _trunk(x_prompt, 0, pool0, mk_p, mv_p, layers, shared,
                               nb=1, ts=512, tm=512, past=None)

    past_len = page_table.shape[1] * cache_kv.shape[1]
    mk_s = jnp.transpose(cache_mem_k, (0, 1, 3, 4, 2)).reshape(DEPTH, n_seq, MEM_W, N_MEM)
    mv_s = jnp.transpose(cache_mem_v, (0, 1, 3, 4, 2)).reshape(DEPTH, n_seq, MEM_W, N_MEM)
    cache_t = jnp.swapaxes(cache_kv, 1, 2)
    y_s, kv_s, pool_s = _trunk(x_sample, past_len, state_pool, mk_s, mv_s, layers, shared,
                               nb=n_seq // 2, ts=s_s, tm=512, past=(cache_t, page_table))

    def mem_out(a):
        a = a.reshape(DEPTH, b_p, MEM_HEADS, MEM_HEAD_DIM, N_MEM)
        return jnp.transpose(a, (0, 1, 4, 2, 3))

    return (y_p, y_s, kv_p, kv_s, pool_p, pool_s, mem_out(mk_p), mem_out(mv_p))
```

```python
import functools

import numpy as np
import jax
import jax.numpy as jnp
from jax import lax
from jax.experimental import pallas as pl
from jax.experimental.pallas import tpu as pltpu

D_MODEL = 1024
DEPTH = 4
N_A_LAYERS = 2
MEM_HEADS = 4
MEM_HEAD_DIM = 64
MEM_W = MEM_HEADS * MEM_HEAD_DIM
N_MEM = 256
POOL_W = D_MODEL - MEM_W
POOL_WINDOWS = (2, 4, 8, 16)
POOL_GROUP_W = POOL_W // len(POOL_WINDOWS)
POOL_STATE = max(POOL_WINDOWS) - 1
POOL_PAD = POOL_STATE + 1
MLA_HEADS = 12
QK_NOPE = 64
QK_ROPE = 32
QK_DIM = QK_NOPE + QK_ROPE
V_HEAD = 64
KV_LORA = 256
Q_LORA = 384
KV_ROW = KV_LORA + QK_ROPE
ROPE_BASE = 10000.0
D_FF = 2816
EPS = 1e-6
NEG = -1e30
LOG2E = 1.4426950408889634

LANE = 128
SLOT = LANE
Q_SLOTS = MLA_HEADS * SLOT
V_SLOT = V_HEAD + 16
V_SLOTS = MLA_HEADS * V_SLOT
VMEM_LIMIT = 56 * 1024 * 1024

BF16 = jnp.bfloat16
F32 = jnp.float32
NT_DIMS = (((1,), (1,)), ((), ()))


def _params(n_grid, flags=None):
    return pltpu.CompilerParams(dimension_semantics=("arbitrary",) * n_grid,
                                vmem_limit_bytes=VMEM_LIMIT, flags=flags)


def _dot(a, b):
    return jnp.dot(a, b, preferred_element_type=F32)


def _dot_nt(a, b):
    return lax.dot_general(a, b, NT_DIMS, preferred_element_type=F32)


def _split_dot(x, w):
    hi = x.astype(BF16)
    lo = (x - hi.astype(F32)).astype(BF16)
    return _dot(hi, w) + _dot(lo, w)


def _rms(x, g):
    return x * lax.rsqrt(jnp.mean(x * x, axis=-1, keepdims=True) + EPS) * g


def _mem_attend(qm, mkt_ref, mvt_ref, gq, gsum, nb, ts):
    ssq = _split_dot(qm * qm, gsum)
    qn = qm * lax.rsqrt(ssq * (1.0 / MEM_HEAD_DIM) + EPS) * (gq * MEM_HEAD_DIM ** -0.5)
    q3 = qn.reshape(nb, ts, MEM_W)
    mkt = mkt_ref[...].astype(BF16)
    mvt = mvt_ref[...].astype(BF16)
    lane = lax.broadcasted_iota(jnp.int32, (1, 1, MEM_W), 2)
    heads = range(MEM_HEADS)
    in_head = [(lane >= h * MEM_HEAD_DIM) & (lane < (h + 1) * MEM_HEAD_DIM) for h in heads]
    qs = [jnp.where(in_head[h], q3, 0.0).astype(BF16) for h in heads]
    ss = [jnp.einsum('bqd,bdm->bqm', qs[h], mkt, preferred_element_type=F32) for h in heads]
    ps = [jnp.exp(s - jnp.max(s, axis=-1, keepdims=True)) for s in ss]
    ps = [(p / jnp.sum(p, axis=-1, keepdims=True)).astype(BF16) for p in ps]
    os_ = [jnp.einsum('bqm,bdm->bqd', p, mvt, preferred_element_type=F32) for p in ps]
    out = jnp.where(in_head[0], os_[0], 0.0)
    for h in range(1, MEM_HEADS):
        out = jnp.where(in_head[h], os_[h], out)
    return out.reshape(nb * ts, MEM_W)


def _memkv_kernel(mem_ref, gnorm_ref, wt_ref, gk_ref, kt_ref, vt_ref):
    x = mem_ref[0]
    xn = x * lax.rsqrt(jnp.mean(x * x, axis=-1, keepdims=True) + EPS)
    hm = (xn * gnorm_ref[0]).astype(BF16)
    kvt = _dot_nt(wt_ref[0], hm)
    kt = kvt[:MEM_W].reshape(MEM_HEADS, MEM_HEAD_DIM, N_MEM)
    inv = lax.rsqrt(jnp.mean(kt * kt, axis=1, keepdims=True) + EPS)
    kt_ref[0, 0] = (kt * inv * gk_ref[0]).reshape(MEM_W, N_MEM)
    vt_ref[0, 0] = kvt[MEM_W:]


def _memkv(mem, norm_mem, w_mem_kv_t, g_mem_k):
    b = mem.shape[0]
    return pl.pallas_call(
        _memkv_kernel,
        grid=(DEPTH, b),
        in_specs=[
            pl.BlockSpec((1, N_MEM, D_MODEL), lambda l, i: (i, 0, 0)),
            pl.BlockSpec((1, 1, D_MODEL), lambda l, i: (l, 0, 0)),
            pl.BlockSpec((1, 2 * MEM_W, D_MODEL), lambda l, i: (l, 0, 0)),
            pl.BlockSpec((1, 1, MEM_HEAD_DIM, 1), lambda l, i: (l, 0, 0, 0)),
        ],
        out_specs=[pl.BlockSpec((1, 1, MEM_W, N_MEM), lambda l, i: (l, i, 0, 0)),
                   pl.BlockSpec((1, 1, MEM_W, N_MEM), lambda l, i: (l, i, 0, 0))],
        out_shape=[jax.ShapeDtypeStruct((DEPTH, b, MEM_W, N_MEM), F32)] * 2,
        compiler_params=_params(2),
        name="memkv",
    )(mem, norm_mem, w_mem_kv_t, g_mem_k)


def _pool_out_kernel(wp_ref, scale_ref, w_out_ref, o_ref):
    for g in range(len(POOL_WINDOWS)):
        rows = slice(g * POOL_GROUP_W, (g + 1) * POOL_GROUP_W)
        a = wp_ref[0, g]
        b = w_out_ref[0, rows, :] * scale_ref[0, rows, :]
        a_hi, b_hi = a.astype(BF16), b.astype(BF16)
        a_lo = (a - a_hi.astype(F32)).astype(BF16)
        b_lo = (b - b_hi.astype(F32)).astype(BF16)
        o_ref[0, rows, :] = (_dot(a_hi, b_hi) + _dot(a_hi, b_lo) + _dot(a_lo, b_hi)).astype(BF16)


def _pool_out_weights(w_pool_grp, pool_scale, w_out):
    n_groups = len(POOL_WINDOWS)
    return pl.pallas_call(
        _pool_out_kernel,
        grid=(N_A_LAYERS,),
        in_specs=[
            pl.BlockSpec((1, n_groups, POOL_GROUP_W, POOL_GROUP_W), lambda l: (l, 0, 0, 0)),
            pl.BlockSpec((1, POOL_W, 1), lambda l: (l, 0, 0)),
            pl.BlockSpec((1, POOL_W, D_MODEL), lambda l: (l, 0, 0)),
        ],
        out_specs=pl.BlockSpec((1, POOL_W, D_MODEL), lambda l: (l, 0, 0)),
        out_shape=jax.ShapeDtypeStruct((N_A_LAYERS, POOL_W, D_MODEL), BF16),
        compiler_params=_params(1),
        name="pool_out_weights",
    )(w_pool_grp, pool_scale[:, :, None], w_out)


def _mixer_a_kernel(x_ref, g_ref, w_in_ref, prev_ref, wpo_ref,
                    mk_ref, mv_ref, gq_ref, gsum_ref, w_out_ref,
                    xo_ref, pool_ref, ext_ref, *, nb, ts, pos0):
    j = pl.program_id(1)
    rows = nb * ts
    x = x_ref[...].reshape(rows, D_MODEL)
    h = _rms(x, g_ref[...]).astype(BF16)
    proj = _dot(h, w_in_ref[...])
    u = proj[:, :POOL_W]
    qm = proj[:, POOL_W:]

    @pl.when(j == 0)
    def _():
        ext_ref[:, 0:POOL_PAD, :] = jnp.zeros((nb, POOL_PAD, POOL_W), F32)
        ext_ref[:, 1:POOL_PAD, :] = prev_ref[...]

    @pl.when(j > 0)
    def _():
        ext_ref[:, 0:POOL_PAD, :] = ext_ref[:, ts:ts + POOL_PAD, :]

    ext_ref[:, POOL_PAD:, :] = u.reshape(nb, ts, POOL_W)
    e = ext_ref[...].reshape(nb * (POOL_PAD + ts), POOL_W)
    s2 = e + pltpu.roll(e, 1, 0)
    s4 = s2 + pltpu.roll(s2, 2, 0)
    s8 = s4 + pltpu.roll(s4, 4, 0)
    s16 = s8 + pltpu.roll(s8, 8, 0)

    def tile_rows(a):
        return a.reshape(nb, POOL_PAD + ts, POOL_W)[:, POOL_PAD:, :]

    t_idx = lax.broadcasted_iota(jnp.int32, (1, ts, 1), 1)
    pos1 = (pos0 + 1 + j * ts + t_idx).astype(F32)
    col = lax.broadcasted_iota(jnp.int32, (1, 1, POOL_W), 2)
    pooled = None
    for g, (w, sw) in enumerate(zip(POOL_WINDOWS, (s2, s4, s8, s16))):
        val = tile_rows(sw) / jnp.minimum(float(w), pos1)
        if pooled is None:
            pooled = val
        else:
            pooled = jnp.where(col >= g * POOL_GROUP_W, val, pooled)
    diff = (pooled.reshape(rows, POOL_W) - u).astype(BF16)

    mo = _mem_attend(qm, mk_ref, mv_ref, gq_ref[...], gsum_ref[...], nb, ts)

    y = _dot(diff, wpo_ref[...]) + _dot(mo.astype(BF16), w_out_ref[POOL_W:, :])
    xo_ref[...] = (x + y).reshape(nb, ts, D_MODEL)

    @pl.when(j == pl.num_programs(1) - 1)
    def _():
        pool_ref[...] = ext_ref[:, ts + 1:ts + POOL_PAD, :]


def _mixer_a(x, prev, mk, mv, wl, *, nb, ts, pos0):
    b, s, _ = x.shape
    kern = functools.partial(_mixer_a_kernel, nb=nb, ts=ts, pos0=pos0)
    const = lambda i, j: (0, 0)
    return pl.pallas_call(
        kern,
        grid=(b // nb, s // ts),
        in_specs=[
            pl.BlockSpec((nb, ts, D_MODEL), lambda i, j: (i, j, 0)),
            pl.BlockSpec((1, D_MODEL), const),
            pl.BlockSpec((None, D_MODEL, D_MODEL), lambda i, j: (wl['l_in'], 0, 0)),
            pl.BlockSpec((nb, POOL_STATE, POOL_W), lambda i, j: (i, 0, 0)),
            pl.BlockSpec((None, POOL_W, D_MODEL), lambda i, j: (wl['l_in'], 0, 0)),
            pl.BlockSpec((None, nb, MEM_W, N_MEM), lambda i, j: (wl['l'], i, 0, 0)),
            pl.BlockSpec((None, nb, MEM_W, N_MEM), lambda i, j: (wl['l'], i, 0, 0)),
            pl.BlockSpec((1, MEM_W), const),
            pl.BlockSpec((MEM_W, MEM_W), const),
            pl.BlockSpec((None, D_MODEL, D_MODEL), lambda i, j: (wl['l'], 0, 0)),
        ],
        out_specs=[pl.BlockSpec((nb, ts, D_MODEL), lambda i, j: (i, j, 0)),
                   pl.BlockSpec((nb, POOL_STATE, POOL_W), lambda i, j: (i, 0, 0))],
        out_shape=[jax.ShapeDtypeStruct((b, s, D_MODEL), F32),
                   jax.ShapeDtypeStruct((b, POOL_STATE, POOL_W), F32)],
        scratch_shapes=[pltpu.VMEM((nb, POOL_PAD + ts, POOL_W), F32)],
        compiler_params=_params(2),
        name="mixer_a",
    )(x, wl['g_mix'], wl['w_in'], prev, wl['w_pool_out'],
      mk, mv, wl['g_mem_q'], wl['gsum'], wl['w_out'])


FFN_CHUNK = 1408


def _ffn_kernel(x_ref, g_ref, w_in_ref, w_out_ref, o_ref):
    x = x_ref[...]
    h = _rms(x, g_ref[...]).astype(BF16)
    acc = x
    for c0 in range(0, D_FF, FFN_CHUNK):
        gate = _dot(h, w_in_ref[:, c0:c0 + FFN_CHUNK])
        up = _dot(h, w_in_ref[:, D_FF + c0:D_FF + c0 + FFN_CHUNK])
        a = (gate * jax.nn.sigmoid(gate) * up).astype(BF16)
        acc = acc + _dot(a, w_out_ref[c0:c0 + FFN_CHUNK, :])
    o_ref[...] = acc


def _ffn(x, g, w_in, w_out, layer, *, tm):
    n = x.shape[0]
    const = lambda i: (0, 0)
    return pl.pallas_call(
        _ffn_kernel,
        grid=(n // tm,),
        in_specs=[
            pl.BlockSpec((tm, D_MODEL), lambda i: (i, 0)),
            pl.BlockSpec((1, D_MODEL), const),
            pl.BlockSpec((None, D_MODEL, 2 * D_FF), lambda i: (layer, 0, 0),
                         pipeline_mode=pl.Buffered(1)),
            pl.BlockSpec((None, D_FF, D_MODEL), lambda i: (layer, 0, 0),
                         pipeline_mode=pl.Buffered(1)),
        ],
        out_specs=pl.BlockSpec((tm, D_MODEL), lambda i: (i, 0)),
        out_shape=jax.ShapeDtypeStruct((n, D_MODEL), F32),
        compiler_params=_params(1),
        name="ffn",
    )(x, g, w_in, w_out)


def _kvrows_kernel(x_ref, g_ref, w_ref, gl_ref, cos_ref, sin_ref, *rest, with_kv):
    if with_kv:
        wk_ref, wv_ref, kv_ref, kn_ref, v_ref = rest
    else:
        (kv_ref,) = rest
    x = x_ref[...]
    h = _rms(x, g_ref[...]).astype(BF16)
    ck = _dot(h, w_ref[...])
    c = _rms(ck[:, :KV_LORA], gl_ref[...])
    kr = (ck[:, KV_LORA:KV_ROW] * cos_ref[...]
          + ck[:, KV_LORA + LANE:KV_LORA + LANE + QK_ROPE] * sin_ref[...])
    kv_ref[:, 0:KV_LORA] = c
    kv_ref[:, KV_LORA:KV_ROW] = kr
    if with_kv:
        cb = c.astype(BF16)
        kn = _dot(cb, wk_ref[0:KV_LORA, :]) + _dot(kr.astype(BF16), wk_ref[KV_LORA:KV_ROW, :])
        is_k = lax.broadcasted_iota(jnp.int32, (1, SLOT), 1) < QK_DIM
        for hd in range(MLA_HEADS):
            sl = slice(hd * SLOT, (hd + 1) * SLOT)
            kh = kn[:, sl]
            ssq = jnp.sum(jnp.where(is_k, kh * kh, 0.0), axis=-1, keepdims=True)
            inv = lax.rsqrt(ssq * (1.0 / QK_DIM) + EPS)
            kn_ref[:, sl] = (kh * inv).astype(BF16)
        vt = _dot_nt(wv_ref[...], cb)
        row = lax.broadcasted_iota(jnp.int32, (V_SLOTS, 1), 0)
        is_one = functools.reduce(
            jnp.logical_or, [row == hd * V_SLOT + V_HEAD for hd in range(MLA_HEADS)])
        v_ref[0] = jnp.where(is_one, 1.0, vt).astype(BF16)


def _kvrows(x, wk, cos32, sin32, *, tm, seq_tiles, with_kv):
    n = x.shape[0]
    const = lambda i: (0, 0)
    in_specs = [
        pl.BlockSpec((tm, D_MODEL), lambda i: (i, 0)),
        pl.BlockSpec((1, D_MODEL), const),
        pl.BlockSpec((D_MODEL, 4 * LANE), const),
        pl.BlockSpec((1, KV_LORA), const),
        pl.BlockSpec((tm, QK_ROPE), lambda i: (i % seq_tiles, 0)),
        pl.BlockSpec((tm, QK_ROPE), lambda i: (i % seq_tiles, 0)),
    ]
    args = [x, wk['norm_kv'], wk['w_dkv_ext'], wk['g_kv_lora'], cos32, sin32]
    out_specs = [pl.BlockSpec((tm, KV_ROW), lambda i: (i, 0))]
    out_shape = [jax.ShapeDtypeStruct((n, KV_ROW), F32)]
    if with_kv:
        in_specs += [pl.BlockSpec((KV_ROW, Q_SLOTS), const),
                     pl.BlockSpec((V_SLOTS, KV_LORA), const)]
        args += [wk['w_kslots'], wk['w_uvt_slots']]
        out_specs += [pl.BlockSpec((tm, Q_SLOTS), lambda i: (i, 0)),
                      pl.BlockSpec((1, V_SLOTS, tm), lambda i: (i // seq_tiles, 0, i % seq_tiles))]
        out_shape += [jax.ShapeDtypeStruct((n, Q_SLOTS), BF16),
                      jax.ShapeDtypeStruct((n // (tm * seq_tiles), V_SLOTS, tm * seq_tiles), BF16)]
    return pl.pallas_call(
        functools.partial(_kvrows_kernel, with_kv=with_kv),
        grid=(n // tm,),
        in_specs=in_specs, out_specs=out_specs, out_shape=out_shape,
        compiler_params=_params(1),
        name="kvrows",
    )(*args)


def _qproj_kernel(x_ref, g_ref, w_in_ref, gql_ref, wq_ref, gq_ref, trig_ref, *rest, absorb):
    if absorb:
        wabs_ref, qabs_ref, qm_ref = rest
    else:
        qf_ref, qm_ref = rest
    x = x_ref[...]
    h = _rms(x, g_ref[...]).astype(BF16)
    proj = _dot(h, w_in_ref[...])
    qm_ref[...] = proj[:, Q_LORA:]
    cq = _rms(proj[:, :Q_LORA], gql_ref[...]).astype(BF16)
    q_all = _dot(cq, wq_ref[...])
    tab = gq_ref[...] * trig_ref[...]
    is_q = lax.broadcasted_iota(jnp.int32, (1, SLOT), 1) < QK_DIM
    for hd in range(MLA_HEADS):
        sl = slice(hd * SLOT, (hd + 1) * SLOT)
        qh = q_all[:, sl]
        ssq = jnp.sum(jnp.where(is_q, qh * qh, 0.0), axis=-1, keepdims=True)
        inv = lax.rsqrt(ssq * (1.0 / QK_DIM) + EPS)
        qf = (qh * tab * (inv * (QK_DIM ** -0.5 * LOG2E))).astype(BF16)
        if absorb:
            qabs_ref[hd] = _dot(qf, wabs_ref[hd])
        else:
            qf_ref[:, sl] = qf


def _qproj(x, wl, wk, trig_t, *, tm, seq_tiles, absorb):
    n = x.shape[0]
    const = lambda i: (0, 0)
    in_specs = [
        pl.BlockSpec((tm, D_MODEL), lambda i: (i, 0)),
        pl.BlockSpec((1, D_MODEL), const),
        pl.BlockSpec((None, D_MODEL, Q_LORA + MEM_W), lambda i: (wl['l_in'], 0, 0)),
        pl.BlockSpec((1, Q_LORA), const),
        pl.BlockSpec((Q_LORA, Q_SLOTS), const),
        pl.BlockSpec((1, SLOT), const),
        pl.BlockSpec((tm, SLOT), lambda i: (i % seq_tiles, 0)),
    ]
    args = [x, wl['g_mix'], wl['w_in'], wl['g_q_lora'], wl['w_uq_slots'], wl['g_q_slot'],
            trig_t]
    if absorb:
        in_specs.append(pl.BlockSpec((MLA_HEADS, SLOT, KV_ROW), lambda i: (0, 0, 0)))
        args.append(wk['w_abs'])
        out_specs = [pl.BlockSpec((MLA_HEADS, tm, KV_ROW), lambda i: (0, i, 0))]
        out_shape = [jax.ShapeDtypeStruct((MLA_HEADS, n, KV_ROW), F32)]
    else:
        out_specs = [pl.BlockSpec((tm, Q_SLOTS), lambda i: (i, 0))]
        out_shape = [jax.ShapeDtypeStruct((n, Q_SLOTS), BF16)]
    out_specs.append(pl.BlockSpec((tm, MEM_W), lambda i: (i, 0)))
    out_shape.append(jax.ShapeDtypeStruct((n, MEM_W), F32))
    return pl.pallas_call(
        functools.partial(_qproj_kernel, absorb=absorb),
        grid=(n // tm,),
        in_specs=in_specs, out_specs=out_specs, out_shape=out_shape,
        compiler_params=_params(1),
        name="qproj",
    )(*args)


def _mla_prompt_kernel(q_ref, k_ref, vt_ref, o_ref, *, tq, nh):
    qi = pl.program_id(2)
    q = q_ref[0]
    q_idx = qi * tq + lax.broadcasted_iota(jnp.int32, (1, tq), 1)
    heads = range(nh)
    slots = [slice(hd * SLOT, (hd + 1) * SLOT) for hd in heads]

    def block(kb, carry, masked):
        start = pl.multiple_of(kb * tq, tq)
        k = k_ref[0, pl.ds(start, tq), :]
        vt = vt_ref[0, :, pl.ds(start, tq)]
        sts = [_dot_nt(k[:, sl], q[:, sl]) for sl in slots]
        if masked:
            k_idx = start + lax.broadcasted_iota(jnp.int32, (tq, 1), 0)
            sts = [jnp.where(k_idx <= q_idx, st, NEG) for st in sts]
        ms = [jnp.maximum(carry[2 * hd], jnp.max(sts[hd], axis=0, keepdims=True))
              for hd in heads]
        ps = [jnp.exp2(sts[hd] - ms[hd]).astype(BF16) for hd in heads]
        pvs = [_dot(vt[hd * V_SLOT:(hd + 1) * V_SLOT, :], ps[hd]) for hd in heads]
        new = []
        for hd in heads:
            alpha = jnp.exp2(carry[2 * hd] - ms[hd])
            new += [ms[hd], carry[2 * hd + 1] * alpha + pvs[hd]]
        return tuple(new)

    init = (jnp.full((1, tq), NEG, F32), jnp.zeros((V_SLOT, tq), F32)) * nh
    carry = lax.fori_loop(0, qi, lambda kb, c: block(kb, c, False), init)
    carry = block(qi, carry, True)
    outs = [carry[2 * hd + 1][:V_HEAD] / carry[2 * hd + 1][V_HEAD:V_HEAD + 1] for hd in heads]
    o_ref[0] = jnp.concatenate(outs, axis=0).T.astype(BF16)


def _mla_prompt(qf, kn, vt, *, tq, nh):
    b, s, _ = qf.shape
    return pl.pallas_call(
        functools.partial(_mla_prompt_kernel, tq=tq, nh=nh),
        grid=(b, MLA_HEADS // nh, s // tq),
        in_specs=[
            pl.BlockSpec((1, tq, nh * SLOT), lambda i, hg, qi: (i, qi, hg)),
            pl.BlockSpec((1, s, nh * SLOT), lambda i, hg, qi: (i, 0, hg)),
            pl.BlockSpec((1, nh * V_SLOT, s), lambda i, hg, qi: (i, hg, 0)),
        ],
        out_specs=pl.BlockSpec((1, tq, nh * V_HEAD), lambda i, hg, qi: (i, qi, hg)),
        out_shape=jax.ShapeDtypeStruct((b, s, MLA_HEADS * V_HEAD), BF16),
        compiler_params=_params(3),
        name="mla_prompt",
    )(qf, kn, vt)


Q_ROWS = MLA_HEADS * 8
N_KN = MLA_HEADS * QK_NOPE
SAMPLE_SUB_PAGES = 4


def _mla_sample_kernel(pt_ref, qabs_ref, wukt_ref, kvnew_ref, cache_ref, *rest, pps, sub_pages,
                       n_tok, first_layer):
    if first_layer:
        o_ref, kinv_out_ref, m_ref, l_ref, acc_ref, lhs_ref, buf_ref, sem = rest
    else:
        kinv_in_ref, o_ref, m_ref, l_ref, acc_ref, lhs_ref, buf_ref, sem = rest
    seq = pl.program_id(0)
    step = pl.program_id(1)
    n_steps = pl.num_programs(1)
    flat = seq * n_steps + step
    slot = lax.rem(flat, 2)

    def page_copies(seq_i, step_i, slot_i):
        return [pltpu.make_async_copy(cache_ref.at[pt_ref[seq_i, step_i * pps + i]],
                                      buf_ref.at[slot_i, i], sem.at[slot_i])
                for i in range(pps)]

    @pl.when(flat == 0)
    def _():
        for cp in page_copies(0, 0, 0):
            cp.start()

    @pl.when(flat + 1 < pl.num_programs(0) * n_steps)
    def _():
        wrap = step + 1 == n_steps
        nxt_seq = jnp.where(wrap, seq + 1, seq)
        nxt_step = jnp.where(wrap, 0, step + 1)
        for cp in page_copies(nxt_seq, nxt_step, 1 - slot):
            cp.start()

    for cp in page_copies(seq, step, slot):
        cp.wait()

    qabs = qabs_ref[...].reshape(Q_ROWS, KV_ROW)
    q_rope = qabs[:, KV_LORA:].astype(BF16)

    @pl.when(step == 0)
    def _():
        m_ref[...] = jnp.full(m_ref.shape, NEG, F32)
        l_ref[...] = jnp.zeros(l_ref.shape, F32)
        acc_ref[...] = jnp.zeros(acc_ref.shape, F32)
        lhs_ref[0:N_KN, :] = wukt_ref[...]
        lhs_ref[N_KN:, :] = qabs[:, :KV_LORA].astype(BF16)

    def scores(ct, krt, kinv, causal):
        t = ct.shape[1]
        if kinv is None:
            r1 = _dot(lhs_ref[...], ct)
            kn = r1[:N_KN]
            ss = jnp.sum((kn * kn).reshape(MLA_HEADS, QK_NOPE, t), axis=1)
            krsq = jnp.sum(krt * krt, axis=0, keepdims=True)
            kinv = lax.rsqrt((ss + krsq) * (1.0 / QK_DIM) + EPS)
            s_lat = r1[N_KN:]
        else:
            s_lat = _dot(lhs_ref[N_KN:, :], ct)
        sc = s_lat + _dot(q_rope, krt.astype(BF16))
        sc = sc.reshape(MLA_HEADS, n_tok, t) * kinv[:, None, :]
        if causal:
            tok = lax.broadcasted_iota(jnp.int32, (1, n_tok, t), 1)
            key = lax.broadcasted_iota(jnp.int32, (1, n_tok, t), 2)
            sc = jnp.where(key <= tok, sc, NEG)
        return sc.reshape(Q_ROWS, t), kinv

    def accumulate(state, sc, ct):
        m_prev, l_prev, acc = state
        m_new = jnp.maximum(m_prev, jnp.max(sc, axis=-1, keepdims=True))
        alpha = jnp.exp2(m_prev - m_new)
        p = jnp.exp2(sc - m_new)
        l_new = alpha * l_prev + jnp.sum(p, axis=-1, keepdims=True)
        return m_new, l_new, alpha * acc + _dot_nt(p.astype(BF16), ct)

    n_sub = pps // sub_pages
    t_sub = sub_pages * buf_ref.shape[3]

    def sub_scores(j):
        pages = range(j * sub_pages, (j + 1) * sub_pages)
        ct = jnp.concatenate([buf_ref[slot, i, 0:KV_LORA, :].astype(BF16) for i in pages], axis=1)
        krt = jnp.concatenate([buf_ref[slot, i, KV_LORA:, :] for i in pages], axis=1)
        if first_layer:
            sc, kinv = scores(ct, krt, None, False)
            kinv_out_ref[0, :, j * t_sub:(j + 1) * t_sub] = kinv
        else:
            sc, _ = scores(ct, krt, kinv_in_ref[0, :, j * t_sub:(j + 1) * t_sub], False)
        return sc, ct

    if first_layer:
        state = (m_ref[...], l_ref[...], acc_ref[...])
        pending = sub_scores(0)
        for j in range(1, n_sub):
            nxt = sub_scores(j)
            state = accumulate(state, *pending)
            pending = nxt
        m_ref[...], l_ref[...], acc_ref[...] = accumulate(state, *pending)
    else:
        subs = [sub_scores(j) for j in range(n_sub)]
        m_prev = m_ref[...]
        m_blk = functools.reduce(jnp.maximum, [sc for sc, _ in subs])
        m_new = jnp.maximum(m_prev, jnp.max(m_blk, axis=-1, keepdims=True))
        alpha = jnp.exp2(m_prev - m_new)
        ps = [jnp.exp2(sc - m_new) for sc, _ in subs]
        l_ref[...] = alpha * l_ref[...] + sum(jnp.sum(p, axis=-1, keepdims=True) for p in ps)
        pv = sum(_dot_nt(p.astype(BF16), ct) for p, (_, ct) in zip(ps, subs))
        acc_ref[...] = alpha * acc_ref[...] + pv
        m_ref[...] = m_new

    @pl.when(step == pl.num_programs(1) - 1)
    def _():
        kvt = kvnew_ref[0]
        ct = kvt[:KV_LORA].astype(BF16)
        sc, _ = scores(ct, kvt[KV_LORA:], None, True)
        _, l_fin, acc = accumulate((m_ref[...], l_ref[...], acc_ref[...]), sc, ct)
        o = acc / l_fin
        for hd in range(MLA_HEADS):
            o_ref[0, :, hd * KV_LORA:(hd + 1) * KV_LORA] = o[hd * n_tok:(hd + 1) * n_tok, :]


def _mla_sample(page_table, qabs, w_ukt, kv_new_t, cache_t, kinv, *, pps):
    n_seq, n_pages = page_table.shape
    n_tok = qabs.shape[1] // n_seq
    page = cache_t.shape[2]
    first_layer = kinv is None
    t_step = pps * page
    assert n_tok == 8 and n_pages % pps == 0

    in_specs = [
        pl.BlockSpec((MLA_HEADS, 1, n_tok, KV_ROW), lambda b, s, pt: (0, b, 0, 0)),
        pl.BlockSpec((N_KN, KV_LORA), lambda b, s, pt: (0, 0)),
        pl.BlockSpec((1, KV_ROW, LANE), lambda b, s, pt: (b, 0, 0)),
        pl.BlockSpec(memory_space=pl.ANY),
    ]
    args = [qabs.reshape(MLA_HEADS, n_seq, n_tok, KV_ROW), w_ukt, kv_new_t, cache_t]
    out_specs = [pl.BlockSpec((1, n_tok, MLA_HEADS * KV_LORA), lambda b, s, pt: (b, 0, 0))]
    out_shape = [jax.ShapeDtypeStruct((n_seq, n_tok, MLA_HEADS * KV_LORA), F32)]
    kinv_spec = pl.BlockSpec((1, MLA_HEADS, t_step), lambda b, s, pt: (b, 0, s))
    if first_layer:
        out_specs.append(kinv_spec)
        out_shape.append(jax.ShapeDtypeStruct((n_seq, MLA_HEADS, n_pages * page), F32))
    else:
        in_specs.append(kinv_spec)
        args.append(kinv)
    grid_spec = pltpu.PrefetchScalarGridSpec(
        num_scalar_prefetch=1,
        grid=(n_seq, n_pages // pps),
        in_specs=in_specs,
        out_specs=out_specs,
        scratch_shapes=[
            pltpu.VMEM((Q_ROWS, 1), F32),
            pltpu.VMEM((Q_ROWS, 1), F32),
            pltpu.VMEM((Q_ROWS, KV_LORA), F32),
            pltpu.VMEM((N_KN + Q_ROWS, KV_LORA), BF16),
            pltpu.VMEM((2, pps, KV_ROW, page), F32),
            pltpu.SemaphoreType.DMA((2,)),
        ],
    )
    return pl.pallas_call(
        functools.partial(_mla_sample_kernel, pps=pps, sub_pages=SAMPLE_SUB_PAGES, n_tok=n_tok,
                          first_layer=first_layer),
        grid_spec=grid_spec,
        out_shape=out_shape,
        compiler_params=_params(2),
        name="mla_sample",
    )(page_table, *args)


def _out_b_kernel(x_ref, tok_ref, qm_ref, mk_ref, mv_ref, gq_ref, gsum_ref, w_out_ref,
                  *rest, nb, ts, latent):
    if latent:
        wuv_ref, o_ref = rest
        tok = _dot(tok_ref[...].astype(BF16), wuv_ref[...]).astype(BF16)
    else:
        (o_ref,) = rest
        tok = tok_ref[...]
    mo = _mem_attend(qm_ref[...], mk_ref, mv_ref, gq_ref[...], gsum_ref[...], nb, ts)
    y = _dot(tok, w_out_ref[0:POOL_W, :]) + _dot(mo.astype(BF16), w_out_ref[POOL_W:, :])
    o_ref[...] = x_ref[...] + y


def _out_b(x, tok, qm, mk, mv, wl, wk, *, nb, ts, latent):
    n = x.shape[0]
    rows = nb * ts
    tok_w = tok.shape[1]
    const = lambda i: (0, 0)
    seq_len = n // mk.shape[1]
    mem_idx = lambda i: (wl['l'], i * rows // (seq_len * nb), 0, 0)
    in_specs = [
        pl.BlockSpec((rows, D_MODEL), lambda i: (i, 0)),
        pl.BlockSpec((rows, tok_w), lambda i: (i, 0)),
        pl.BlockSpec((rows, MEM_W), lambda i: (i, 0)),
        pl.BlockSpec((None, nb, MEM_W, N_MEM), mem_idx),
        pl.BlockSpec((None, nb, MEM_W, N_MEM), mem_idx),
        pl.BlockSpec((1, MEM_W), const),
        pl.BlockSpec((MEM_W, MEM_W), const),
        pl.BlockSpec((None, D_MODEL, D_MODEL), lambda i: (wl['l'], 0, 0)),
    ]
    args = [x, tok, qm, mk, mv, wl['g_mem_q'], wl['gsum'], wl['w_out']]
    if latent:
        in_specs.append(pl.BlockSpec((MLA_HEADS * KV_LORA, MLA_HEADS * V_HEAD), const))
        args.append(wk['w_uv_bd'])
    return pl.pallas_call(
        functools.partial(_out_b_kernel, nb=nb, ts=ts, latent=latent),
        grid=(n // rows,),
        in_specs=in_specs,
        out_specs=pl.BlockSpec((rows, D_MODEL), lambda i: (i, 0)),
        out_shape=jax.ShapeDtypeStruct((n, D_MODEL), F32),
        compiler_params=_params(1),
        name="out_b",
    )(*args)


def _rope_tables(pos):
    half = QK_ROPE // 2
    inv_freq = ROPE_BASE ** (-jnp.arange(half, dtype=F32) / half)
    ang = pos.astype(F32)[:, None] * inv_freq[None, :]
    cos32 = jnp.concatenate([jnp.cos(ang)] * 2, axis=-1)
    sin32 = jnp.concatenate([jnp.sin(ang)] * 2, axis=-1)
    n = pos.shape[0]
    trig_t = jnp.concatenate([jnp.ones((n, QK_NOPE), F32), cos32, sin32], axis=-1)
    return cos32, sin32, trig_t


def _rot_cols(w):
    half = QK_ROPE // 2
    return jnp.concatenate([-w[..., half:], w[..., :half]], axis=-1)


def _prep_weights(norm_mix, norm_ffn, w_out, w_ffn_in, w_ffn_out, g_mem_q, w_in_a, w_pool_grp,
                  pool_scale, w_in_b, g_q_lora, w_uq, g_q, norm_kv, w_dkv, g_kv_lora, w_uk, w_uv,
                  g_k_nope, g_k_rope):
    half = QK_ROPE // 2
    gsum = jnp.asarray(np.kron(np.eye(MEM_HEADS), np.ones((MEM_HEAD_DIM, MEM_HEAD_DIM))), BF16)
    w_out_b, w_ffn_in_b, w_ffn_out_b = (w.astype(BF16) for w in (w_out, w_ffn_in, w_ffn_out))
    w_in_a_b, w_in_b_b = w_in_a.astype(BF16), w_in_b.astype(BF16)
    w_pool_out = _pool_out_weights(w_pool_grp, pool_scale, w_out)
    layers = []
    for l in range(DEPTH):
        wl = dict(
            l=l, g_mix=norm_mix[l][None], g_ffn=norm_ffn[l][None],
            w_out=w_out_b, w_ffn_in=w_ffn_in_b, w_ffn_out=w_ffn_out_b,
            g_mem_q=jnp.tile(g_mem_q[l], MEM_HEADS)[None], gsum=gsum)
        if l < N_A_LAYERS:
            wl.update(w_in=w_in_a_b, l_in=l, w_pool_out=w_pool_out)
        else:
            j = l - N_A_LAYERS
            wq = w_uq[j].reshape(Q_LORA, MLA_HEADS, QK_DIM)
            w_slots = jnp.concatenate([wq, _rot_cols(wq[..., QK_NOPE:])],
                                      axis=-1).reshape(Q_LORA, Q_SLOTS)
            gk2 = jnp.concatenate([g_k_rope, g_k_rope])
            gq = g_q[j]
            g_rot = jnp.concatenate([gq[QK_NOPE + half:], gq[QK_NOPE:QK_NOPE + half]])
            g_slot = jnp.concatenate([gq[:QK_NOPE] * g_k_nope, gq[QK_NOPE:] * gk2, g_rot * gk2])
            wl.update(w_in=w_in_b_b, l_in=j, g_q_lora=g_q_lora[j][None],
                      w_uq_slots=w_slots.astype(BF16), g_q_slot=g_slot[None])
        layers.append(wl)

    zcol = jnp.zeros((D_MODEL, LANE - QK_ROPE), F32)
    w_dkv_ext = jnp.concatenate([w_dkv, zcol, _rot_cols(w_dkv[:, KV_LORA:]), zcol], axis=-1)
    wuk_h = w_uk.reshape(KV_LORA, MLA_HEADS, QK_NOPE)
    top = jnp.concatenate([wuk_h, jnp.zeros((KV_LORA, MLA_HEADS, SLOT - QK_NOPE), F32)], axis=-1)
    eye = jnp.eye(QK_ROPE, dtype=F32)
    bot = jnp.concatenate([jnp.zeros((QK_ROPE, QK_NOPE), F32), eye, eye], axis=-1)
    bot = jnp.broadcast_to(bot[:, None, :], (QK_ROPE, MLA_HEADS, SLOT))
    w_kslots = jnp.concatenate([top, bot], axis=0).reshape(KV_ROW, Q_SLOTS)
    w_abs = jnp.zeros((MLA_HEADS, SLOT, KV_ROW), F32)
    w_abs = w_abs.at[:, :QK_NOPE, :KV_LORA].set(jnp.transpose(wuk_h, (1, 2, 0)))
    eyes = jnp.broadcast_to(eye, (MLA_HEADS,) + eye.shape)
    w_abs = w_abs.at[:, QK_NOPE:QK_DIM, KV_LORA:].set(eyes).at[:, QK_DIM:, KV_LORA:].set(eyes)
    wuv_h = w_uv.reshape(KV_LORA, MLA_HEADS, V_HEAD)
    w_uvt_slots = jnp.concatenate(
        [jnp.transpose(wuv_h, (1, 2, 0)), jnp.zeros((MLA_HEADS, V_SLOT - V_HEAD, KV_LORA), F32)],
        axis=1).reshape(V_SLOTS, KV_LORA)
    w_uv_bd = jnp.einsum('chd,hg->hcgd', wuv_h, jnp.eye(MLA_HEADS, dtype=F32))
    w_uv_bd = w_uv_bd.reshape(MLA_HEADS * KV_LORA, MLA_HEADS * V_HEAD)
    shared = dict(
        norm_kv=norm_kv[None], g_kv_lora=g_kv_lora[None],
        w_dkv_ext=w_dkv_ext.astype(BF16), w_kslots=w_kslots.astype(BF16),
        w_uvt_slots=w_uvt_slots.astype(BF16), w_abs=w_abs.astype(BF16),
        w_ukt=jnp.transpose(w_uk).astype(BF16), w_uv_bd=w_uv_bd.astype(BF16))
    return layers, shared


def _trunk(x, pos0, pool_prev, mem_k, mem_v, layers, shared, *, nb, ts, tm, past):
    b, s, _ = x.shape
    n = b * s
    pos = pos0 + jnp.arange(s)
    tabs = _rope_tables(pos)
    if s >= tm:
        tmk, seq_tiles = tm, s // tm
    else:
        reps = min(tm, n) // s
        tmk, seq_tiles = reps * s, 1
        tabs = tuple(jnp.tile(t, (reps, 1)) for t in tabs)
    cos32, sin32, trig_t = tabs
    pool_new = []
    kv_new = None
    for l in range(DEPTH):
        wl = layers[l]
        if l < N_A_LAYERS:
            x, st = _mixer_a(x, pool_prev[l], mem_k, mem_v, wl, nb=nb, ts=ts, pos0=pos0)
            pool_new.append(st)
            xf = x.reshape(n, D_MODEL)
        else:
            xf = x.reshape(n, D_MODEL)
            if l == N_A_LAYERS:
                if past is None:
                    kv_new, kn, vt = _kvrows(xf, shared, cos32, sin32, tm=tmk,
                                             seq_tiles=seq_tiles, with_kv=True)
                else:
                    (kv_new,) = _kvrows(xf, shared, cos32, sin32, tm=tmk,
                                        seq_tiles=seq_tiles, with_kv=False)
            if past is None:
                qf, qm = _qproj(xf, wl, shared, trig_t, tm=tmk, seq_tiles=seq_tiles,
                                absorb=False)
                tok = _mla_prompt(qf.reshape(b, s, Q_SLOTS), kn.reshape(b, s, Q_SLOTS), vt, tq=256,
                                  nh=12)
                tok = tok.reshape(n, MLA_HEADS * V_HEAD)
                nb_o, ts_o = 1, tmk
            else:
                cache_t, page_table = past
                qabs, qm = _qproj(xf, wl, shared, trig_t, tm=tmk, seq_tiles=seq_tiles,
                                  absorb=True)
                if l == N_A_LAYERS:
                    kv_new_t = jnp.pad(jnp.swapaxes(kv_new.reshape(b, s, KV_ROW), 1, 2),
                                       ((0, 0), (0, 0), (0, LANE - s)))
                    tok, kinv = _mla_sample(page_table, qabs, shared['w_ukt'], kv_new_t, cache_t,
                                            None, pps=64)
                else:
                    (tok,) = _mla_sample(page_table, qabs, shared['w_ukt'], kv_new_t, cache_t,
                                         kinv, pps=64)
                tok = tok.reshape(n, MLA_HEADS * KV_LORA)
                nb_o, ts_o = nb, ts
            xf = _out_b(xf, tok, qm, mem_k, mem_v, wl, shared, nb=nb_o, ts=ts_o,
                        latent=past is not None)
        xf = _ffn(xf, wl['g_ffn'], wl['w_ffn_in'], wl['w_ffn_out'], wl['l'], tm=min(tm, n))
        x = xf.reshape(b, s, D_MODEL)
    return x, kv_new.reshape(b, s, KV_ROW), jnp.stack(pool_new)


def kernel(x_prompt, x_sample, cache_kv, state_pool, cache_mem_k, cache_mem_v, page_table, mem_prompt,
           norm_mix, norm_ffn, w_out, w_ffn_in, w_ffn_out, norm_mem, w_mem_kv, g_mem_q, g_mem_k,
           w_in_a, w_pool_grp, pool_scale, w_in_b, g_q_lora, w_uq, g_q,
           norm_kv, w_dkv, g_kv_lora, w_uk, w_uv, g_k_nope, g_k_rope):
    layers, shared = _prep_weights(norm_mix, norm_ffn, w_out, w_ffn_in, w_ffn_out, g_mem_q, w_in_a,
                                   w_pool_grp, pool_scale, w_in_b, g_q_lora, w_uq, g_q, norm_kv,
                                   w_dkv, g_kv_lora, w_uk, w_uv, g_k_nope, g_k_rope)
    b_p, s_p, _ = x_prompt.shape
    n_seq, s_s, _ = x_sample.shape

    mk_p, mv_p = _memkv(mem_prompt, norm_mem[:, None, :],
                        jnp.swapaxes(w_mem_kv, 1, 2).astype(BF16), g_mem_k[:, None, :, None])
    pool0 = jnp.zeros((N_A_LAYERS, b_p, POOL_STATE, POOL_W), F32)
    y_p, kv_p, pool_p = _trunk(x_prompt, 0, pool0, mk_p, mv_p, layers, shared,
                               nb=1, ts=512, tm=512, past=None)

    past_len = page_table.shape[1] * cache_kv.shape[1]
    mk_s = jnp.transpose(cache_mem_k, (0, 1, 3, 4, 2)).reshape(DEPTH, n_seq, MEM_W, N_MEM)
    mv_s = jnp.transpose(cache_mem_v, (0, 1, 3, 4, 2)).reshape(DEPTH, n_seq, MEM_W, N_MEM)
    cache_t = jnp.swapaxes(cache_kv, 1, 2)
    y_s, kv_s, pool_s = _trunk(x_sample, past_len, state_pool, mk_s, mv_s, layers, shared,
                               nb=n_seq // 2, ts=s_s, tm=512, past=(cache_t, page_table))

    def mem_out(a):
        a = a.reshape(DEPTH, b_p, MEM_HEADS, MEM_HEAD_DIM, N_MEM)
        return jnp.transpose(a, (0, 1, 4, 2, 3))

    return (y_p, y_s, kv_p, kv_s, pool_p, pool_s, mem_out(mk_p), mem_out(mv_p))
```

```python
import functools

import numpy as np
import jax
import jax.numpy as jnp
from jax import lax
from jax.experimental import pallas as pl
from jax.experimental.pallas import tpu as pltpu

D_MODEL = 1024
DEPTH = 4
N_A_LAYERS = 2
MEM_HEADS = 4
MEM_HEAD_DIM = 64
MEM_W = MEM_HEADS * MEM_HEAD_DIM
N_MEM = 256
POOL_W = D_MODEL - MEM_W
POOL_WINDOWS = (2, 4, 8, 16)
POOL_GROUP_W = POOL_W // len(POOL_WINDOWS)
POOL_STATE = max(POOL_WINDOWS) - 1
POOL_PAD = POOL_STATE + 1
MLA_HEADS = 12
QK_NOPE = 64
QK_ROPE = 32
QK_DIM = QK_NOPE + QK_ROPE
V_HEAD = 64
KV_LORA = 256
Q_LORA = 384
KV_ROW = KV_LORA + QK_ROPE
ROPE_BASE = 10000.0
D_FF = 2816
EPS = 1e-6
NEG = -1e30
LOG2E = 1.4426950408889634

LANE = 128
BF16_SUBLANES = 16
SLOT = LANE
Q_SLOTS = MLA_HEADS * SLOT
V_SLOT = V_HEAD + BF16_SUBLANES
V_SLOTS = MLA_HEADS * V_SLOT
VMEM_LIMIT = 56 * 1024 * 1024

MIXER_SEQ_TILE = 512
TOKEN_ROWS = 1024
FFN_ROWS = 512
FFN_CHUNK = 1408
ATTN_TILE = 256
ATTN_HEADS_PER_STEP = 12
SAMPLE_SEQS_PER_STEP = 16
SAMPLE_PAGES_PER_STEP = 64

BF16 = jnp.bfloat16
F32 = jnp.float32
NT_DIMS = (((1,), (1,)), ((), ()))


def _params(n_grid):
    return pltpu.CompilerParams(dimension_semantics=("arbitrary",) * n_grid,
                                vmem_limit_bytes=VMEM_LIMIT)


def _dot(a, b):
    return jnp.dot(a, b, preferred_element_type=F32)


def _dot_nt(a, b):
    return lax.dot_general(a, b, NT_DIMS, preferred_element_type=F32)


def _split_dot(x, w):
    hi = x.astype(BF16)
    lo = (x - hi.astype(F32)).astype(BF16)
    return _dot(hi, w) + _dot(lo, w)


def _rms(x, g):
    return x * lax.rsqrt(jnp.mean(x * x, axis=-1, keepdims=True) + EPS) * g


def _mem_attend(qm, mkt_ref, mvt_ref, gq, gsum, nb, ts):
    ssq = _split_dot(qm * qm, gsum)
    qn = qm * lax.rsqrt(ssq * (1.0 / MEM_HEAD_DIM) + EPS) * (gq * MEM_HEAD_DIM ** -0.5)
    q3 = qn.reshape(nb, ts, MEM_W)
    mkt = mkt_ref[...].astype(BF16)
    mvt = mvt_ref[...].astype(BF16)
    lane = lax.broadcasted_iota(jnp.int32, (1, 1, MEM_W), 2)
    heads = range(MEM_HEADS)
    in_head = [(lane >= h * MEM_HEAD_DIM) & (lane < (h + 1) * MEM_HEAD_DIM) for h in heads]
    qs = [jnp.where(in_head[h], q3, 0.0).astype(BF16) for h in heads]
    ss = [jnp.einsum('bqd,bdm->bqm', qs[h], mkt, preferred_element_type=F32) for h in heads]
    ps = [jnp.exp(s - jnp.max(s, axis=-1, keepdims=True)) for s in ss]
    ps = [(p / jnp.sum(p, axis=-1, keepdims=True)).astype(BF16) for p in ps]
    os_ = [jnp.einsum('bqm,bdm->bqd', p, mvt, preferred_element_type=F32) for p in ps]
    out = jnp.where(in_head[0], os_[0], 0.0)
    for h in range(1, MEM_HEADS):
        out = jnp.where(in_head[h], os_[h], out)
    return out.reshape(nb * ts, MEM_W)


def _memkv_kernel(mem_ref, gnorm_ref, wt_ref, gk_ref, kt_ref, vt_ref):
    x = mem_ref[0]
    xn = x * lax.rsqrt(jnp.mean(x * x, axis=-1, keepdims=True) + EPS)
    hm = (xn * gnorm_ref[0]).astype(BF16)
    kvt = _dot_nt(wt_ref[0], hm)
    kt = kvt[:MEM_W].reshape(MEM_HEADS, MEM_HEAD_DIM, N_MEM)
    inv = lax.rsqrt(jnp.mean(kt * kt, axis=1, keepdims=True) + EPS)
    kt_ref[0, 0] = (kt * inv * gk_ref[0]).reshape(MEM_W, N_MEM)
    vt_ref[0, 0] = kvt[MEM_W:]


def _memkv(mem, norm_mem, w_mem_kv_t, g_mem_k):
    b = mem.shape[0]
    return pl.pallas_call(
        _memkv_kernel,
        grid=(DEPTH, b),
        in_specs=[
            pl.BlockSpec((1, N_MEM, D_MODEL), lambda l, i: (i, 0, 0)),
            pl.BlockSpec((1, 1, D_MODEL), lambda l, i: (l, 0, 0)),
            pl.BlockSpec((1, 2 * MEM_W, D_MODEL), lambda l, i: (l, 0, 0)),
            pl.BlockSpec((1, 1, MEM_HEAD_DIM, 1), lambda l, i: (l, 0, 0, 0)),
        ],
        out_specs=[pl.BlockSpec((1, 1, MEM_W, N_MEM), lambda l, i: (l, i, 0, 0)),
                   pl.BlockSpec((1, 1, MEM_W, N_MEM), lambda l, i: (l, i, 0, 0))],
        out_shape=[jax.ShapeDtypeStruct((DEPTH, b, MEM_W, N_MEM), F32)] * 2,
        compiler_params=_params(2),
        name="memkv",
    )(mem, norm_mem, w_mem_kv_t, g_mem_k)


def _pool_out_kernel(wp_ref, scale_ref, w_out_ref, o_ref):
    for g in range(len(POOL_WINDOWS)):
        rows = slice(g * POOL_GROUP_W, (g + 1) * POOL_GROUP_W)
        a = wp_ref[0, g]
        b = w_out_ref[0, rows, :] * scale_ref[0, rows, :]
        a_hi, b_hi = a.astype(BF16), b.astype(BF16)
        a_lo = (a - a_hi.astype(F32)).astype(BF16)
        b_lo = (b - b_hi.astype(F32)).astype(BF16)
        o_ref[0, rows, :] = (_dot(a_hi, b_hi) + _dot(a_hi, b_lo) + _dot(a_lo, b_hi)).astype(BF16)


def _pool_out_weights(w_pool_grp, pool_scale, w_out):
    n_groups = len(POOL_WINDOWS)
    return pl.pallas_call(
        _pool_out_kernel,
        grid=(N_A_LAYERS,),
        in_specs=[
            pl.BlockSpec((1, n_groups, POOL_GROUP_W, POOL_GROUP_W), lambda l: (l, 0, 0, 0)),
            pl.BlockSpec((1, POOL_W, 1), lambda l: (l, 0, 0)),
            pl.BlockSpec((1, POOL_W, D_MODEL), lambda l: (l, 0, 0)),
        ],
        out_specs=pl.BlockSpec((1, POOL_W, D_MODEL), lambda l: (l, 0, 0)),
        out_shape=jax.ShapeDtypeStruct((N_A_LAYERS, POOL_W, D_MODEL), BF16),
        compiler_params=_params(1),
        name="pool_out_weights",
    )(w_pool_grp, pool_scale[:, :, None], w_out)


def _mixer_a_kernel(x_ref, g_ref, w_in_ref, prev_ref, wpo_ref,
                    mk_ref, mv_ref, gq_ref, gsum_ref, w_out_ref,
                    xo_ref, pool_ref, ext_ref, *, nb, ts, pos0):
    j = pl.program_id(1)
    rows = nb * ts
    x = x_ref[...].reshape(rows, D_MODEL)
    h = _rms(x, g_ref[...]).astype(BF16)
    proj = _dot(h, w_in_ref[...])
    u = proj[:, :POOL_W]
    qm = proj[:, POOL_W:]

    @pl.when(j == 0)
    def _():
        ext_ref[:, 0:POOL_PAD, :] = jnp.zeros((nb, POOL_PAD, POOL_W), F32)
        ext_ref[:, 1:POOL_PAD, :] = prev_ref[...]

    @pl.when(j > 0)
    def _():
        ext_ref[:, 0:POOL_PAD, :] = ext_ref[:, ts:ts + POOL_PAD, :]

    ext_ref[:, POOL_PAD:, :] = u.reshape(nb, ts, POOL_W)
    e = ext_ref[...].reshape(nb * (POOL_PAD + ts), POOL_W)
    s2 = e + pltpu.roll(e, 1, 0)
    s4 = s2 + pltpu.roll(s2, 2, 0)
    s8 = s4 + pltpu.roll(s4, 4, 0)
    s16 = s8 + pltpu.roll(s8, 8, 0)

    def tile_rows(a):
        return a.reshape(nb, POOL_PAD + ts, POOL_W)[:, POOL_PAD:, :]

    t_idx = lax.broadcasted_iota(jnp.int32, (1, ts, 1), 1)
    pos1 = (pos0 + 1 + j * ts + t_idx).astype(F32)
    col = lax.broadcasted_iota(jnp.int32, (1, 1, POOL_W), 2)
    pooled = None
    for g, (w, sw) in enumerate(zip(POOL_WINDOWS, (s2, s4, s8, s16))):
        val = tile_rows(sw) * (1.0 / jnp.minimum(float(w), pos1))
        if pooled is None:
            pooled = val
        else:
            pooled = jnp.where(col >= g * POOL_GROUP_W, val, pooled)
    diff = (pooled.reshape(rows, POOL_W) - u).astype(BF16)

    mo = _mem_attend(qm, mk_ref, mv_ref, gq_ref[...], gsum_ref[...], nb, ts)

    y = _dot(diff, wpo_ref[...]) + _dot(mo.astype(BF16), w_out_ref[POOL_W:, :])
    xo_ref[...] = (x + y).reshape(nb, ts, D_MODEL)

    @pl.when(j == pl.num_programs(1) - 1)
    def _():
        pool_ref[...] = ext_ref[:, ts + 1:ts + POOL_PAD, :]


def _mixer_a(x, prev, mk, mv, wl, *, nb, ts, pos0):
    b, s, _ = x.shape
    assert b % nb == 0 and s % ts == 0 and (ts >= POOL_PAD or s == ts)
    kern = functools.partial(_mixer_a_kernel, nb=nb, ts=ts, pos0=pos0)
    const = lambda i, j: (0, 0)
    return pl.pallas_call(
        kern,
        grid=(b // nb, s // ts),
        in_specs=[
            pl.BlockSpec((nb, ts, D_MODEL), lambda i, j: (i, j, 0)),
            pl.BlockSpec((1, D_MODEL), const),
            pl.BlockSpec((None, D_MODEL, D_MODEL), lambda i, j: (wl['l_in'], 0, 0)),
            pl.BlockSpec((nb, POOL_STATE, POOL_W), lambda i, j: (i, 0, 0)),
            pl.BlockSpec((None, POOL_W, D_MODEL), lambda i, j: (wl['l_in'], 0, 0)),
            pl.BlockSpec((None, nb, MEM_W, N_MEM), lambda i, j: (wl['l'], i, 0, 0)),
            pl.BlockSpec((None, nb, MEM_W, N_MEM), lambda i, j: (wl['l'], i, 0, 0)),
            pl.BlockSpec((1, MEM_W), const),
            pl.BlockSpec((MEM_W, MEM_W), const),
            pl.BlockSpec((None, D_MODEL, D_MODEL), lambda i, j: (wl['l'], 0, 0)),
        ],
        out_specs=[pl.BlockSpec((nb, ts, D_MODEL), lambda i, j: (i, j, 0)),
                   pl.BlockSpec((nb, POOL_STATE, POOL_W), lambda i, j: (i, 0, 0))],
        out_shape=[jax.ShapeDtypeStruct((b, s, D_MODEL), F32),
                   jax.ShapeDtypeStruct((b, POOL_STATE, POOL_W), F32)],
        scratch_shapes=[pltpu.VMEM((nb, POOL_PAD + ts, POOL_W), F32)],
        compiler_params=_params(2),
        name="mixer_a",
    )(x, wl['g_mix'], wl['w_in'], prev, wl['w_pool_out'],
      mk, mv, wl['g_mem_q'], wl['gsum'], wl['w_out'])


def _ffn_kernel(x_ref, g_ref, w_in_ref, w_out_ref, o_ref):
    x = x_ref[...]
    h = _rms(x, g_ref[...]).astype(BF16)
    acc = x
    for c0 in range(0, D_FF, FFN_CHUNK):
        gate = _dot(h, w_in_ref[:, c0:c0 + FFN_CHUNK])
        up = _dot(h, w_in_ref[:, D_FF + c0:D_FF + c0 + FFN_CHUNK])
        a = (gate * jax.nn.sigmoid(gate) * up).astype(BF16)
        acc = acc + _dot(a, w_out_ref[c0:c0 + FFN_CHUNK, :])
    o_ref[...] = acc


def _ffn(x, g, w_in, w_out, layer, *, tm):
    n = x.shape[0]
    assert n % tm == 0 and D_FF % FFN_CHUNK == 0 and FFN_CHUNK % LANE == 0
    const = lambda i: (0, 0)
    return pl.pallas_call(
        _ffn_kernel,
        grid=(n // tm,),
        in_specs=[
            pl.BlockSpec((tm, D_MODEL), lambda i: (i, 0)),
            pl.BlockSpec((1, D_MODEL), const),
            pl.BlockSpec((None, D_MODEL, 2 * D_FF), lambda i: (layer, 0, 0),
                         pipeline_mode=pl.Buffered(1)),
            pl.BlockSpec((None, D_FF, D_MODEL), lambda i: (layer, 0, 0),
                         pipeline_mode=pl.Buffered(1)),
        ],
        out_specs=pl.BlockSpec((tm, D_MODEL), lambda i: (i, 0)),
        out_shape=jax.ShapeDtypeStruct((n, D_MODEL), F32),
        compiler_params=_params(1),
        name="ffn",
    )(x, g, w_in, w_out)


def _kvrows_kernel(x_ref, g_ref, w_ref, gl_ref, cos_ref, sin_ref, *rest, with_kv):
    if with_kv:
        wk_ref, wv_ref, kv_ref, kn_ref, v_ref = rest
    else:
        (kv_ref,) = rest
    x = x_ref[...]
    h = _rms(x, g_ref[...]).astype(BF16)
    ck = _dot(h, w_ref[...])
    c = _rms(ck[:, :KV_LORA], gl_ref[...])
    kr = (ck[:, KV_LORA:KV_ROW] * cos_ref[...]
          + ck[:, KV_LORA + LANE:KV_LORA + LANE + QK_ROPE] * sin_ref[...])
    kv_ref[:, 0:KV_LORA] = c
    kv_ref[:, KV_LORA:KV_ROW] = kr
    if with_kv:
        cb = c.astype(BF16)
        kn = _dot(cb, wk_ref[0:KV_LORA, :]) + _dot(kr.astype(BF16), wk_ref[KV_LORA:KV_ROW, :])
        is_k = lax.broadcasted_iota(jnp.int32, (1, SLOT), 1) < QK_DIM
        for hd in range(MLA_HEADS):
            sl = slice(hd * SLOT, (hd + 1) * SLOT)
            kh = kn[:, sl]
            ssq = jnp.sum(jnp.where(is_k, kh * kh, 0.0), axis=-1, keepdims=True)
            inv = lax.rsqrt(ssq * (1.0 / QK_DIM) + EPS)
            kn_ref[:, sl] = (kh * inv).astype(BF16)
        vt = _dot_nt(wv_ref[...], cb)
        row = lax.broadcasted_iota(jnp.int32, (V_SLOTS, 1), 0)
        is_one = functools.reduce(
            jnp.logical_or, [row == hd * V_SLOT + V_HEAD for hd in range(MLA_HEADS)])
        v_ref[0] = jnp.where(is_one, 1.0, vt).astype(BF16)


def _kvrows(x, wk, cos32, sin32, *, tm, seq_tiles, with_kv):
    n = x.shape[0]
    const = lambda i: (0, 0)
    in_specs = [
        pl.BlockSpec((tm, D_MODEL), lambda i: (i, 0)),
        pl.BlockSpec((1, D_MODEL), const),
        pl.BlockSpec((D_MODEL, 4 * LANE), const),
        pl.BlockSpec((1, KV_LORA), const),
        pl.BlockSpec((tm, QK_ROPE), lambda i: (i % seq_tiles, 0)),
        pl.BlockSpec((tm, QK_ROPE), lambda i: (i % seq_tiles, 0)),
    ]
    args = [x, wk['norm_kv'], wk['w_dkv_ext'], wk['g_kv_lora'], cos32, sin32]
    out_specs = [pl.BlockSpec((tm, KV_ROW), lambda i: (i, 0))]
    out_shape = [jax.ShapeDtypeStruct((n, KV_ROW), F32)]
    if with_kv:
        in_specs += [pl.BlockSpec((KV_ROW, Q_SLOTS), const),
                     pl.BlockSpec((V_SLOTS, KV_LORA), const)]
        args += [wk['w_kslots'], wk['w_uvt_slots']]
        out_specs += [pl.BlockSpec((tm, Q_SLOTS), lambda i: (i, 0)),
                      pl.BlockSpec((1, V_SLOTS, tm), lambda i: (i // seq_tiles, 0, i % seq_tiles))]
        out_shape += [jax.ShapeDtypeStruct((n, Q_SLOTS), BF16),
                      jax.ShapeDtypeStruct((n // (tm * seq_tiles), V_SLOTS, tm * seq_tiles), BF16)]
    return pl.pallas_call(
        functools.partial(_kvrows_kernel, with_kv=with_kv),
        grid=(n // tm,),
        in_specs=in_specs, out_specs=out_specs, out_shape=out_shape,
        compiler_params=_params(1),
        name="kvrows",
    )(*args)


def _qproj_kernel(x_ref, g_ref, w_in_ref, gql_ref, wq_ref, gq_ref, trig_ref, *rest, absorb):
    if absorb:
        wabs_ref, qabs_ref, qm_ref = rest
    else:
        qf_ref, qm_ref = rest
    x = x_ref[...]
    h = _rms(x, g_ref[...]).astype(BF16)
    proj = _dot(h, w_in_ref[...])
    qm_ref[...] = proj[:, Q_LORA:]
    cq = _rms(proj[:, :Q_LORA], gql_ref[...]).astype(BF16)
    q_all = _dot(cq, wq_ref[...])
    tab = gq_ref[...] * trig_ref[...]
    is_q = lax.broadcasted_iota(jnp.int32, (1, SLOT), 1) < QK_DIM
    for hd in range(MLA_HEADS):
        sl = slice(hd * SLOT, (hd + 1) * SLOT)
        qh = q_all[:, sl]
        ssq = jnp.sum(jnp.where(is_q, qh * qh, 0.0), axis=-1, keepdims=True)
        inv = lax.rsqrt(ssq * (1.0 / QK_DIM) + EPS)
        qf = (qh * tab * (inv * (QK_DIM ** -0.5 * LOG2E))).astype(BF16)
        if absorb:
            qabs_ref[hd] = _dot(qf, wabs_ref[hd])
        else:
            qf_ref[:, sl] = qf


def _qproj(x, wl, wk, trig_t, *, tm, seq_tiles, absorb):
    n = x.shape[0]
    const = lambda i: (0, 0)
    in_specs = [
        pl.BlockSpec((tm, D_MODEL), lambda i: (i, 0)),
        pl.BlockSpec((1, D_MODEL), const),
        pl.BlockSpec((None, D_MODEL, Q_LORA + MEM_W), lambda i: (wl['l_in'], 0, 0)),
        pl.BlockSpec((1, Q_LORA), const),
        pl.BlockSpec((Q_LORA, Q_SLOTS), const),
        pl.BlockSpec((1, SLOT), const),
        pl.BlockSpec((tm, SLOT), lambda i: (i % seq_tiles, 0)),
    ]
    args = [x, wl['g_mix'], wl['w_in'], wl['g_q_lora'], wl['w_uq_slots'], wl['g_q_slot'],
            trig_t]
    if absorb:
        in_specs.append(pl.BlockSpec((MLA_HEADS, SLOT, KV_ROW), lambda i: (0, 0, 0)))
        args.append(wk['w_abs'])
        out_specs = [pl.BlockSpec((MLA_HEADS, tm, KV_ROW), lambda i: (0, i, 0))]
        out_shape = [jax.ShapeDtypeStruct((MLA_HEADS, n, KV_ROW), F32)]
    else:
        out_specs = [pl.BlockSpec((tm, Q_SLOTS), lambda i: (i, 0))]
        out_shape = [jax.ShapeDtypeStruct((n, Q_SLOTS), BF16)]
    out_specs.append(pl.BlockSpec((tm, MEM_W), lambda i: (i, 0)))
    out_shape.append(jax.ShapeDtypeStruct((n, MEM_W), F32))
    return pl.pallas_call(
        functools.partial(_qproj_kernel, absorb=absorb),
        grid=(n // tm,),
        in_specs=in_specs, out_specs=out_specs, out_shape=out_shape,
        compiler_params=_params(1),
        name="qproj",
    )(*args)


def _mla_prompt_kernel(q_ref, k_ref, vt_ref, o_ref, *, tq, nh):
    qi = pl.program_id(2)
    q = q_ref[0]
    q_idx = qi * tq + lax.broadcasted_iota(jnp.int32, (1, tq), 1)
    heads = range(nh)
    slots = [slice(hd * SLOT, (hd + 1) * SLOT) for hd in heads]

    def block(kb, carry, masked):
        start = pl.multiple_of(kb * tq, tq)
        k = k_ref[0, pl.ds(start, tq), :]
        vt = vt_ref[0, :, pl.ds(start, tq)]
        sts = [_dot_nt(k[:, sl], q[:, sl]) for sl in slots]
        if masked:
            k_idx = start + lax.broadcasted_iota(jnp.int32, (tq, 1), 0)
            sts = [jnp.where(k_idx <= q_idx, st, NEG) for st in sts]
        ms = [jnp.maximum(carry[2 * hd], jnp.max(sts[hd], axis=0, keepdims=True))
              for hd in heads]
        ps = [jnp.exp2(sts[hd] - ms[hd]).astype(BF16) for hd in heads]
        pvs = [_dot(vt[hd * V_SLOT:(hd + 1) * V_SLOT, :], ps[hd]) for hd in heads]
        new = []
        for hd in heads:
            alpha = jnp.exp2(carry[2 * hd] - ms[hd])
            new += [ms[hd], carry[2 * hd + 1] * alpha + pvs[hd]]
        return tuple(new)

    init = (jnp.full((1, tq), NEG, F32), jnp.zeros((V_SLOT, tq), F32)) * nh
    carry = lax.fori_loop(0, qi, lambda kb, c: block(kb, c, False), init)
    carry = block(qi, carry, True)
    outs = [carry[2 * hd + 1][:V_HEAD] / carry[2 * hd + 1][V_HEAD:V_HEAD + 1] for hd in heads]
    o_ref[0] = jnp.concatenate(outs, axis=0).T.astype(BF16)


def _mla_prompt(qf, kn, vt, *, tq, nh):
    b, s, _ = qf.shape
    assert s % tq == 0 and MLA_HEADS % nh == 0
    return pl.pallas_call(
        functools.partial(_mla_prompt_kernel, tq=tq, nh=nh),
        grid=(b, MLA_HEADS // nh, s // tq),
        in_specs=[
            pl.BlockSpec((1, tq, nh * SLOT), lambda i, hg, qi: (i, qi, hg)),
            pl.BlockSpec((1, s, nh * SLOT), lambda i, hg, qi: (i, 0, hg)),
            pl.BlockSpec((1, nh * V_SLOT, s), lambda i, hg, qi: (i, hg, 0)),
        ],
        out_specs=pl.BlockSpec((1, tq, nh * V_HEAD), lambda i, hg, qi: (i, qi, hg)),
        out_shape=jax.ShapeDtypeStruct((b, s, MLA_HEADS * V_HEAD), BF16),
        compiler_params=_params(3),
        name="mla_prompt",
    )(qf, kn, vt)


Q_ROWS = MLA_HEADS * 8
N_KN = MLA_HEADS * QK_NOPE
SAMPLE_SUB_PAGES = (8, 2)


def _mla_sample_kernel(pt_ref, qabs_ref, wukt_ref, kvnew_ref, cache_ref, *rest, pps, sub_pages,
                       n_tok, first_layer):
    if first_layer:
        o_ref, kinv_out_ref, m_ref, l_ref, acc_ref, lhs_ref, buf_ref, sem = rest
    else:
        kinv_in_ref, o_ref, m_ref, l_ref, acc_ref, lhs_ref, buf_ref, sem = rest
    seq = pl.program_id(0)
    step = pl.program_id(1)
    n_steps = pl.num_programs(1)
    flat = seq * n_steps + step
    slot = lax.rem(flat, 2)

    def page_copies(seq_i, step_i, slot_i):
        return [pltpu.make_async_copy(cache_ref.at[pt_ref[seq_i, step_i * pps + i]],
                                      buf_ref.at[slot_i, i], sem.at[slot_i])
                for i in range(pps)]

    @pl.when(flat == 0)
    def _():
        for cp in page_copies(0, 0, 0):
            cp.start()

    @pl.when(flat + 1 < pl.num_programs(0) * n_steps)
    def _():
        wrap = step + 1 == n_steps
        nxt_seq = jnp.where(wrap, seq + 1, seq)
        nxt_step = jnp.where(wrap, 0, step + 1)
        for cp in page_copies(nxt_seq, nxt_step, 1 - slot):
            cp.start()

    for cp in page_copies(seq, step, slot):
        cp.wait()

    qabs = qabs_ref[...].reshape(Q_ROWS, KV_ROW)
    q_rope = qabs[:, KV_LORA:].astype(BF16)

    @pl.when(step == 0)
    def _():
        m_ref[...] = jnp.full(m_ref.shape, NEG, F32)
        l_ref[...] = jnp.zeros(l_ref.shape, F32)
        acc_ref[...] = jnp.zeros(acc_ref.shape, F32)
        lhs_ref[0:N_KN, :] = wukt_ref[...]
        lhs_ref[N_KN:, :] = qabs[:, :KV_LORA].astype(BF16)

    def scores(ct, krt, kinv, causal):
        t = ct.shape[1]
        if kinv is None:
            r1 = _dot(lhs_ref[...], ct)
            kn = r1[:N_KN]
            ss = jnp.sum((kn * kn).reshape(MLA_HEADS, QK_NOPE, t), axis=1)
            krsq = jnp.sum(krt * krt, axis=0, keepdims=True)
            kinv = lax.rsqrt((ss + krsq) * (1.0 / QK_DIM) + EPS)
            s_lat = r1[N_KN:]
        else:
            s_lat = _dot(lhs_ref[N_KN:, :], ct)
        sc = s_lat + _dot(q_rope, krt.astype(BF16))
        sc = sc.reshape(MLA_HEADS, n_tok, t) * kinv[:, None, :]
        if causal:
            tok = lax.broadcasted_iota(jnp.int32, (1, n_tok, t), 1)
            key = lax.broadcasted_iota(jnp.int32, (1, n_tok, t), 2)
            sc = jnp.where(key <= tok, sc, NEG)
        return sc.reshape(Q_ROWS, t), kinv

    def accumulate(state, sc, ct):
        m_prev, l_prev, acc = state
        m_new = jnp.maximum(m_prev, jnp.max(sc, axis=-1, keepdims=True))
        alpha = jnp.exp2(m_prev - m_new)
        p = jnp.exp2(sc - m_new)
        l_new = alpha * l_prev + jnp.sum(p, axis=-1, keepdims=True)
        return m_new, l_new, alpha * acc + _dot_nt(p.astype(BF16), ct)

    n_sub = pps // sub_pages
    t_sub = sub_pages * buf_ref.shape[3]

    def sub_scores(j):
        pages = range(j * sub_pages, (j + 1) * sub_pages)
        ct = jnp.concatenate([buf_ref[slot, i, 0:KV_LORA, :].astype(BF16) for i in pages], axis=1)
        krt = jnp.concatenate([buf_ref[slot, i, KV_LORA:, :] for i in pages], axis=1)
        if first_layer:
            sc, kinv = scores(ct, krt, None, False)
            kinv_out_ref[0, :, j * t_sub:(j + 1) * t_sub] = kinv
        else:
            sc, _ = scores(ct, krt, kinv_in_ref[0, :, j * t_sub:(j + 1) * t_sub], False)
        return sc, ct

    if first_layer:
        state = (m_ref[...], l_ref[...], acc_ref[...])
        pending = sub_scores(0)
        for j in range(1, n_sub):
            nxt = sub_scores(j)
            state = accumulate(state, *pending)
            pending = nxt
        m_ref[...], l_ref[...], acc_ref[...] = accumulate(state, *pending)
    else:
        subs = [sub_scores(j) for j in range(n_sub)]
        m_prev = m_ref[...]
        m_blk = functools.reduce(jnp.maximum, [sc for sc, _ in subs])
        m_new = jnp.maximum(m_prev, jnp.max(m_blk, axis=-1, keepdims=True))
        alpha = jnp.exp2(m_prev - m_new)
        ps = [jnp.exp2(sc - m_new) for sc, _ in subs]
        l_ref[...] = alpha * l_ref[...] + sum(jnp.sum(p, axis=-1, keepdims=True) for p in ps)
        pv = sum(_dot_nt(p.astype(BF16), ct) for p, (_, ct) in zip(ps, subs))
        acc_ref[...] = alpha * acc_ref[...] + pv
        m_ref[...] = m_new

    @pl.when(step == pl.num_programs(1) - 1)
    def _():
        kvt = kvnew_ref[0]
        ct = kvt[:KV_LORA].astype(BF16)
        sc, _ = scores(ct, kvt[KV_LORA:], None, True)
        _, l_fin, acc = accumulate((m_ref[...], l_ref[...], acc_ref[...]), sc, ct)
        o = acc / l_fin
        for hd in range(MLA_HEADS):
            o_ref[0, :, hd * KV_LORA:(hd + 1) * KV_LORA] = o[hd * n_tok:(hd + 1) * n_tok, :]


def _mla_sample(page_table, qabs, w_ukt, kv_new_t, cache_t, kinv, *, pps):
    n_seq, n_pages = page_table.shape
    n_tok = qabs.shape[1] // n_seq
    page = cache_t.shape[2]
    first_layer = kinv is None
    t_step = pps * page
    assert n_tok == 8 and n_pages % pps == 0

    in_specs = [
        pl.BlockSpec((MLA_HEADS, 1, n_tok, KV_ROW), lambda b, s, pt: (0, b, 0, 0)),
        pl.BlockSpec((N_KN, KV_LORA), lambda b, s, pt: (0, 0)),
        pl.BlockSpec((1, KV_ROW, LANE), lambda b, s, pt: (b, 0, 0)),
        pl.BlockSpec(memory_space=pl.ANY),
    ]
    args = [qabs.reshape(MLA_HEADS, n_seq, n_tok, KV_ROW), w_ukt, kv_new_t, cache_t]
    out_specs = [pl.BlockSpec((1, n_tok, MLA_HEADS * KV_LORA), lambda b, s, pt: (b, 0, 0))]
    out_shape = [jax.ShapeDtypeStruct((n_seq, n_tok, MLA_HEADS * KV_LORA), F32)]
    kinv_spec = pl.BlockSpec((1, MLA_HEADS, t_step), lambda b, s, pt: (b, 0, s))
    if first_layer:
        out_specs.append(kinv_spec)
        out_shape.append(jax.ShapeDtypeStruct((n_seq, MLA_HEADS, n_pages * page), F32))
    else:
        in_specs.append(kinv_spec)
        args.append(kinv)
    grid_spec = pltpu.PrefetchScalarGridSpec(
        num_scalar_prefetch=1,
        grid=(n_seq, n_pages // pps),
        in_specs=in_specs,
        out_specs=out_specs,
        scratch_shapes=[
            pltpu.VMEM((Q_ROWS, 1), F32),
            pltpu.VMEM((Q_ROWS, 1), F32),
            pltpu.VMEM((Q_ROWS, KV_LORA), F32),
            pltpu.VMEM((N_KN + Q_ROWS, KV_LORA), BF16),
            pltpu.VMEM((2, pps, KV_ROW, page), F32),
            pltpu.SemaphoreType.DMA((2,)),
        ],
    )
    return pl.pallas_call(
        functools.partial(_mla_sample_kernel, pps=pps,
                          sub_pages=SAMPLE_SUB_PAGES[0 if first_layer else 1], n_tok=n_tok,
                          first_layer=first_layer),
        grid_spec=grid_spec,
        out_shape=out_shape,
        compiler_params=_params(2),
        name="mla_sample",
    )(page_table, *args)


def _out_b_kernel(x_ref, tok_ref, qm_ref, mk_ref, mv_ref, gq_ref, gsum_ref, w_out_ref,
                  *rest, nb, ts, latent):
    if latent:
        wuv_ref, o_ref = rest
        tok = _dot(tok_ref[...].astype(BF16), wuv_ref[...]).astype(BF16)
    else:
        (o_ref,) = rest
        tok = tok_ref[...]
    mo = _mem_attend(qm_ref[...], mk_ref, mv_ref, gq_ref[...], gsum_ref[...], nb, ts)
    y = _dot(tok, w_out_ref[0:POOL_W, :]) + _dot(mo.astype(BF16), w_out_ref[POOL_W:, :])
    o_ref[...] = x_ref[...] + y


def _out_b(x, tok, qm, mk, mv, wl, wk, *, nb, ts, latent):
    n = x.shape[0]
    rows = nb * ts
    tok_w = tok.shape[1]
    const = lambda i: (0, 0)
    seq_len = n // mk.shape[1]
    mem_idx = lambda i: (wl['l'], i * rows // (seq_len * nb), 0, 0)
    in_specs = [
        pl.BlockSpec((rows, D_MODEL), lambda i: (i, 0)),
        pl.BlockSpec((rows, tok_w), lambda i: (i, 0)),
        pl.BlockSpec((rows, MEM_W), lambda i: (i, 0)),
        pl.BlockSpec((None, nb, MEM_W, N_MEM), mem_idx),
        pl.BlockSpec((None, nb, MEM_W, N_MEM), mem_idx),
        pl.BlockSpec((1, MEM_W), const),
        pl.BlockSpec((MEM_W, MEM_W), const),
        pl.BlockSpec((None, D_MODEL, D_MODEL), lambda i: (wl['l'], 0, 0)),
    ]
    args = [x, tok, qm, mk, mv, wl['g_mem_q'], wl['gsum'], wl['w_out']]
    if latent:
        in_specs.append(pl.BlockSpec((MLA_HEADS * KV_LORA, MLA_HEADS * V_HEAD), const))
        args.append(wk['w_uv_bd'])
    return pl.pallas_call(
        functools.partial(_out_b_kernel, nb=nb, ts=ts, latent=latent),
        grid=(n // rows,),
        in_specs=in_specs,
        out_specs=pl.BlockSpec((rows, D_MODEL), lambda i: (i, 0)),
        out_shape=jax.ShapeDtypeStruct((n, D_MODEL), F32),
        compiler_params=_params(1),
        name="out_b",
    )(*args)


def _rope_tables(pos):
    half = QK_ROPE // 2
    inv_freq = ROPE_BASE ** (-jnp.arange(half, dtype=F32) / half)
    ang = pos.astype(F32)[:, None] * inv_freq[None, :]
    cos32 = jnp.concatenate([jnp.cos(ang)] * 2, axis=-1)
    sin32 = jnp.concatenate([jnp.sin(ang)] * 2, axis=-1)
    n = pos.shape[0]
    trig_t = jnp.concatenate([jnp.ones((n, QK_NOPE), F32), cos32, sin32], axis=-1)
    return cos32, sin32, trig_t


def _rot_cols(w):
    half = QK_ROPE // 2
    return jnp.concatenate([-w[..., half:], w[..., :half]], axis=-1)


def _prep_weights(norm_mix, norm_ffn, w_out, w_ffn_in, w_ffn_out, g_mem_q, w_in_a, w_pool_grp,
                  pool_scale, w_in_b, g_q_lora, w_uq, g_q, norm_kv, w_dkv, g_kv_lora, w_uk, w_uv,
                  g_k_nope, g_k_rope):
    half = QK_ROPE // 2
    gsum = jnp.asarray(np.kron(np.eye(MEM_HEADS), np.ones((MEM_HEAD_DIM, MEM_HEAD_DIM))), BF16)
    w_out_b, w_ffn_in_b, w_ffn_out_b = (w.astype(BF16) for w in (w_out, w_ffn_in, w_ffn_out))
    w_in_a_b, w_in_b_b = w_in_a.astype(BF16), w_in_b.astype(BF16)
    w_pool_out = _pool_out_weights(w_pool_grp, pool_scale, w_out)
    layers = []
    for l in range(DEPTH):
        wl = dict(
            l=l, g_mix=norm_mix[l][None], g_ffn=norm_ffn[l][None],
            w_out=w_out_b, w_ffn_in=w_ffn_in_b, w_ffn_out=w_ffn_out_b,
            g_mem_q=jnp.tile(g_mem_q[l], MEM_HEADS)[None], gsum=gsum)
        if l < N_A_LAYERS:
            wl.update(w_in=w_in_a_b, l_in=l, w_pool_out=w_pool_out)
        else:
            j = l - N_A_LAYERS
            wq = w_uq[j].reshape(Q_LORA, MLA_HEADS, QK_DIM)
            w_slots = jnp.concatenate([wq, _rot_cols(wq[..., QK_NOPE:])],
                                      axis=-1).reshape(Q_LORA, Q_SLOTS)
            gk2 = jnp.concatenate([g_k_rope, g_k_rope])
            gq = g_q[j]
            g_rot = jnp.concatenate([gq[QK_NOPE + half:], gq[QK_NOPE:QK_NOPE + half]])
            g_slot = jnp.concatenate([gq[:QK_NOPE] * g_k_nope, gq[QK_NOPE:] * gk2, g_rot * gk2])
            wl.update(w_in=w_in_b_b, l_in=j, g_q_lora=g_q_lora[j][None],
                      w_uq_slots=w_slots.astype(BF16), g_q_slot=g_slot[None])
        layers.append(wl)

    zcol = jnp.zeros((D_MODEL, LANE - QK_ROPE), F32)
    w_dkv_ext = jnp.concatenate([w_dkv, zcol, _rot_cols(w_dkv[:, KV_LORA:]), zcol], axis=-1)
    wuk_h = w_uk.reshape(KV_LORA, MLA_HEADS, QK_NOPE)
    top = jnp.concatenate([wuk_h, jnp.zeros((KV_LORA, MLA_HEADS, SLOT - QK_NOPE), F32)], axis=-1)
    eye = jnp.eye(QK_ROPE, dtype=F32)
    bot = jnp.concatenate([jnp.zeros((QK_ROPE, QK_NOPE), F32), eye, eye], axis=-1)
    bot = jnp.broadcast_to(bot[:, None, :], (QK_ROPE, MLA_HEADS, SLOT))
    w_kslots = jnp.concatenate([top, bot], axis=0).reshape(KV_ROW, Q_SLOTS)
    w_abs = jnp.zeros((MLA_HEADS, SLOT, KV_ROW), F32)
    w_abs = w_abs.at[:, :QK_NOPE, :KV_LORA].set(jnp.transpose(wuk_h, (1, 2, 0)))
    eyes = jnp.broadcast_to(eye, (MLA_HEADS,) + eye.shape)
    w_abs = w_abs.at[:, QK_NOPE:QK_DIM, KV_LORA:].set(eyes).at[:, QK_DIM:, KV_LORA:].set(eyes)
    wuv_h = w_uv.reshape(KV_LORA, MLA_HEADS, V_HEAD)
    w_uvt_slots = jnp.concatenate(
        [jnp.transpose(wuv_h, (1, 2, 0)), jnp.zeros((MLA_HEADS, V_SLOT - V_HEAD, KV_LORA), F32)],
        axis=1).reshape(V_SLOTS, KV_LORA)
    w_uv_bd = jnp.einsum('chd,hg->hcgd', wuv_h, jnp.eye(MLA_HEADS, dtype=F32))
    w_uv_bd = w_uv_bd.reshape(MLA_HEADS * KV_LORA, MLA_HEADS * V_HEAD)
    shared = dict(
        norm_kv=norm_kv[None], g_kv_lora=g_kv_lora[None],
        w_dkv_ext=w_dkv_ext.astype(BF16), w_kslots=w_kslots.astype(BF16),
        w_uvt_slots=w_uvt_slots.astype(BF16), w_abs=w_abs.astype(BF16),
        w_ukt=jnp.transpose(w_uk).astype(BF16), w_uv_bd=w_uv_bd.astype(BF16))
    return layers, shared


def _trunk(x, pos0, pool_prev, mem_k, mem_v, layers, shared, *, nb, ts, tm, past):
    b, s, _ = x.shape
    n = b * s
    pos = pos0 + jnp.arange(s)
    tabs = _rope_tables(pos)
    if s >= tm:
        tmk, seq_tiles = tm, s // tm
    else:
        reps = min(tm, n) // s
        tmk, seq_tiles = reps * s, 1
        tabs = tuple(jnp.tile(t, (reps, 1)) for t in tabs)
    cos32, sin32, trig_t = tabs
    pool_new = []
    kv_new = None
    for l in range(DEPTH):
        wl = layers[l]
        if l < N_A_LAYERS:
            x, st = _mixer_a(x, pool_prev[l], mem_k, mem_v, wl, nb=nb, ts=ts, pos0=pos0)
            pool_new.append(st)
            xf = x.reshape(n, D_MODEL)
        else:
            xf = x.reshape(n, D_MODEL)
            if l == N_A_LAYERS:
                if past is None:
                    kv_new, kn, vt = _kvrows(xf, shared, cos32, sin32, tm=tmk,
                                             seq_tiles=seq_tiles, with_kv=True)
                else:
                    (kv_new,) = _kvrows(xf, shared, cos32, sin32, tm=tmk,
                                        seq_tiles=seq_tiles, with_kv=False)
            if past is None:
                qf, qm = _qproj(xf, wl, shared, trig_t, tm=tmk, seq_tiles=seq_tiles,
                                absorb=False)
                tok = _mla_prompt(qf.reshape(b, s, Q_SLOTS), kn.reshape(b, s, Q_SLOTS), vt,
                                  tq=ATTN_TILE, nh=ATTN_HEADS_PER_STEP)
                tok = tok.reshape(n, MLA_HEADS * V_HEAD)
                nb_o, ts_o = 1, tmk
            else:
                cache_t, page_table = past
                qabs, qm = _qproj(xf, wl, shared, trig_t, tm=tmk, seq_tiles=seq_tiles,
                                  absorb=True)
                if l == N_A_LAYERS:
                    kv_new_t = jnp.pad(jnp.swapaxes(kv_new.reshape(b, s, KV_ROW), 1, 2),
                                       ((0, 0), (0, 0), (0, LANE - s)))
                    tok, kinv = _mla_sample(page_table, qabs, shared['w_ukt'], kv_new_t, cache_t,
                                            None, pps=SAMPLE_PAGES_PER_STEP)
                else:
                    (tok,) = _mla_sample(page_table, qabs, shared['w_ukt'], kv_new_t, cache_t,
                                         kinv, pps=SAMPLE_PAGES_PER_STEP)
                tok = tok.reshape(n, MLA_HEADS * KV_LORA)
                nb_o, ts_o = nb, ts
            xf = _out_b(xf, tok, qm, mem_k, mem_v, wl, shared, nb=nb_o, ts=ts_o,
                        latent=past is not None)
        xf = _ffn(xf, wl['g_ffn'], wl['w_ffn_in'], wl['w_ffn_out'], wl['l'],
                  tm=min(FFN_ROWS, n))
        x = xf.reshape(b, s, D_MODEL)
    return x, kv_new.reshape(b, s, KV_ROW), jnp.stack(pool_new)


def kernel(x_prompt, x_sample, cache_kv, state_pool, cache_mem_k, cache_mem_v, page_table, mem_prompt,
           norm_mix, norm_ffn, w_out, w_ffn_in, w_ffn_out, norm_mem, w_mem_kv, g_mem_q, g_mem_k,
           w_in_a, w_pool_grp, pool_scale, w_in_b, g_q_lora, w_uq, g_q,
           norm_kv, w_dkv, g_kv_lora, w_uk, w_uv, g_k_nope, g_k_rope):
    layers, shared = _prep_weights(norm_mix, norm_ffn, w_out, w_ffn_in, w_ffn_out, g_mem_q, w_in_a,
                                   w_pool_grp, pool_scale, w_in_b, g_q_lora, w_uq, g_q, norm_kv,
                                   w_dkv, g_kv_lora, w_uk, w_uv, g_k_nope, g_k_rope)
    b_p, s_p, _ = x_prompt.shape
    n_seq, s_s, _ = x_sample.shape

    mk_p, mv_p = _memkv(mem_prompt, norm_mem[:, None, :],
                        jnp.swapaxes(w_mem_kv, 1, 2).astype(BF16), g_mem_k[:, None, :, None])
    pool0 = jnp.zeros((N_A_LAYERS, b_p, POOL_STATE, POOL_W), F32)
    y_p, kv_p, pool_p = _trunk(x_prompt, 0, pool0, mk_p, mv_p, layers, shared,
                               nb=1, ts=MIXER_SEQ_TILE, tm=TOKEN_ROWS, past=None)

    past_len = page_table.shape[1] * cache_kv.shape[1]
    mk_s = jnp.transpose(cache_mem_k, (0, 1, 3, 4, 2)).reshape(DEPTH, n_seq, MEM_W, N_MEM)
    mv_s = jnp.transpose(cache_mem_v, (0, 1, 3, 4, 2)).reshape(DEPTH, n_seq, MEM_W, N_MEM)
    cache_t = jnp.swapaxes(cache_kv, 1, 2)
    y_s, kv_s, pool_s = _trunk(x_sample, past_len, state_pool, mk_s, mv_s, layers, shared,
                               nb=SAMPLE_SEQS_PER_STEP, ts=s_s, tm=TOKEN_ROWS,
                               past=(cache_t, page_table))

    def mem_out(a):
        a = a.reshape(DEPTH, b_p, MEM_HEADS, MEM_HEAD_DIM, N_MEM)
        return jnp.transpose(a, (0, 1, 4, 2, 3))

    return (y_p, y_s, kv_p, kv_s, pool_p, pool_s, mem_out(mk_p), mem_out(mv_p))
```

```python
import functools

import numpy as np
import jax
import jax.numpy as jnp
from jax import lax
from jax.experimental import pallas as pl
from jax.experimental.pallas import tpu as pltpu

D_MODEL = 1024
DEPTH = 4
N_A_LAYERS = 2
MEM_HEADS = 4
MEM_HEAD_DIM = 64
MEM_W = MEM_HEADS * MEM_HEAD_DIM
N_MEM = 256
POOL_W = D_MODEL - MEM_W
POOL_WINDOWS = (2, 4, 8, 16)
POOL_GROUP_W = POOL_W // len(POOL_WINDOWS)
POOL_STATE = max(POOL_WINDOWS) - 1
POOL_PAD = POOL_STATE + 1
MLA_HEADS = 12
QK_NOPE = 64
QK_ROPE = 32
QK_DIM = QK_NOPE + QK_ROPE
V_HEAD = 64
KV_LORA = 256
Q_LORA = 384
KV_ROW = KV_LORA + QK_ROPE
ROPE_BASE = 10000.0
D_FF = 2816
EPS = 1e-6
NEG = -1e30
LOG2E = 1.4426950408889634

LANE = 128
BF16_SUBLANES = 16
SLOT = LANE
Q_SLOTS = MLA_HEADS * SLOT
V_SLOT = V_HEAD + BF16_SUBLANES
V_SLOTS = MLA_HEADS * V_SLOT
VMEM_LIMIT = 56 * 1024 * 1024

MIXER_SEQ_TILE = 512
TOKEN_ROWS = 1024
FFN_ROWS = 512
FFN_CHUNK = 1408
ATTN_TILE = 256
ATTN_HEADS_PER_STEP = 12
SAMPLE_SEQS_PER_STEP = 16
SAMPLE_PAGES_PER_STEP = 64
SAMPLE_PAGE_SLOTS = 3

BF16 = jnp.bfloat16
F32 = jnp.float32
NT_DIMS = (((1,), (1,)), ((), ()))


def _params(n_grid):
    return pltpu.CompilerParams(dimension_semantics=("arbitrary",) * n_grid,
                                vmem_limit_bytes=VMEM_LIMIT)


def _dot(a, b):
    return jnp.dot(a, b, preferred_element_type=F32)


def _dot_nt(a, b):
    return lax.dot_general(a, b, NT_DIMS, preferred_element_type=F32)


def _split_dot(x, w):
    hi = x.astype(BF16)
    lo = (x - hi.astype(F32)).astype(BF16)
    return _dot(hi, w) + _dot(lo, w)


def _rms(x, g):
    return x * lax.rsqrt(jnp.mean(x * x, axis=-1, keepdims=True) + EPS) * g


def _mem_attend(qm, mkt_ref, mvt_ref, gq, gsum, nb, ts):
    ssq = _split_dot(qm * qm, gsum)
    qn = qm * lax.rsqrt(ssq * (1.0 / MEM_HEAD_DIM) + EPS) * (gq * MEM_HEAD_DIM ** -0.5)
    q3 = qn.reshape(nb, ts, MEM_W)
    mkt = mkt_ref[...].astype(BF16)
    mvt = mvt_ref[...].astype(BF16)
    lane = lax.broadcasted_iota(jnp.int32, (1, 1, MEM_W), 2)
    heads = range(MEM_HEADS)
    in_head = [(lane >= h * MEM_HEAD_DIM) & (lane < (h + 1) * MEM_HEAD_DIM) for h in heads]
    qs = [jnp.where(in_head[h], q3, 0.0).astype(BF16) for h in heads]
    ss = [jnp.einsum('bqd,bdm->bqm', qs[h], mkt, preferred_element_type=F32) for h in heads]
    ps = [jnp.exp(s - jnp.max(s, axis=-1, keepdims=True)) for s in ss]
    ps = [(p / jnp.sum(p, axis=-1, keepdims=True)).astype(BF16) for p in ps]
    os_ = [jnp.einsum('bqm,bdm->bqd', p, mvt, preferred_element_type=F32) for p in ps]
    out = jnp.where(in_head[0], os_[0], 0.0)
    for h in range(1, MEM_HEADS):
        out = jnp.where(in_head[h], os_[h], out)
    return out.reshape(nb * ts, MEM_W)


def _memkv_kernel(mem_ref, gnorm_ref, wt_ref, gk_ref, kt_ref, vt_ref):
    x = mem_ref[0]
    xn = x * lax.rsqrt(jnp.mean(x * x, axis=-1, keepdims=True) + EPS)
    hm = (xn * gnorm_ref[0]).astype(BF16)
    kvt = _dot_nt(wt_ref[0], hm)
    kt = kvt[:MEM_W].reshape(MEM_HEADS, MEM_HEAD_DIM, N_MEM)
    inv = lax.rsqrt(jnp.mean(kt * kt, axis=1, keepdims=True) + EPS)
    kt_ref[0, 0] = (kt * inv * gk_ref[0]).reshape(MEM_W, N_MEM)
    vt_ref[0, 0] = kvt[MEM_W:]


def _memkv(mem, norm_mem, w_mem_kv_t, g_mem_k):
    b = mem.shape[0]
    return pl.pallas_call(
        _memkv_kernel,
        grid=(DEPTH, b),
        in_specs=[
            pl.BlockSpec((1, N_MEM, D_MODEL), lambda l, i: (i, 0, 0)),
            pl.BlockSpec((1, 1, D_MODEL), lambda l, i: (l, 0, 0)),
            pl.BlockSpec((1, 2 * MEM_W, D_MODEL), lambda l, i: (l, 0, 0)),
            pl.BlockSpec((1, 1, MEM_HEAD_DIM, 1), lambda l, i: (l, 0, 0, 0)),
        ],
        out_specs=[pl.BlockSpec((1, 1, MEM_W, N_MEM), lambda l, i: (l, i, 0, 0)),
                   pl.BlockSpec((1, 1, MEM_W, N_MEM), lambda l, i: (l, i, 0, 0))],
        out_shape=[jax.ShapeDtypeStruct((DEPTH, b, MEM_W, N_MEM), F32)] * 2,
        compiler_params=_params(2),
        name="memkv",
    )(mem, norm_mem, w_mem_kv_t, g_mem_k)


def _pool_out_kernel(wp_ref, scale_ref, w_out_ref, o_ref):
    for g in range(len(POOL_WINDOWS)):
        rows = slice(g * POOL_GROUP_W, (g + 1) * POOL_GROUP_W)
        a = wp_ref[0, g]
        b = w_out_ref[0, rows, :] * scale_ref[0, rows, :]
        a_hi, b_hi = a.astype(BF16), b.astype(BF16)
        a_lo = (a - a_hi.astype(F32)).astype(BF16)
        b_lo = (b - b_hi.astype(F32)).astype(BF16)
        o_ref[0, rows, :] = (_dot(a_hi, b_hi) + _dot(a_hi, b_lo) + _dot(a_lo, b_hi)).astype(BF16)


def _pool_out_weights(w_pool_grp, pool_scale, w_out):
    n_groups = len(POOL_WINDOWS)
    return pl.pallas_call(
        _pool_out_kernel,
        grid=(N_A_LAYERS,),
        in_specs=[
            pl.BlockSpec((1, n_groups, POOL_GROUP_W, POOL_GROUP_W), lambda l: (l, 0, 0, 0)),
            pl.BlockSpec((1, POOL_W, 1), lambda l: (l, 0, 0)),
            pl.BlockSpec((1, POOL_W, D_MODEL), lambda l: (l, 0, 0)),
        ],
        out_specs=pl.BlockSpec((1, POOL_W, D_MODEL), lambda l: (l, 0, 0)),
        out_shape=jax.ShapeDtypeStruct((N_A_LAYERS, POOL_W, D_MODEL), BF16),
        compiler_params=_params(1),
        name="pool_out_weights",
    )(w_pool_grp, pool_scale[:, :, None], w_out)


def _mixer_a_kernel(x_ref, g_ref, w_in_ref, prev_ref, wpo_ref,
                    mk_ref, mv_ref, gq_ref, gsum_ref, w_out_ref,
                    xo_ref, pool_ref, ext_ref, *, nb, ts, pos0):
    j = pl.program_id(1)
    rows = nb * ts
    x = x_ref[...].reshape(rows, D_MODEL)
    h = _rms(x, g_ref[...]).astype(BF16)
    proj = _dot(h, w_in_ref[...])
    u = proj[:, :POOL_W]
    qm = proj[:, POOL_W:]

    @pl.when(j == 0)
    def _():
        ext_ref[:, 0:POOL_PAD, :] = jnp.zeros((nb, POOL_PAD, POOL_W), F32)
        ext_ref[:, 1:POOL_PAD, :] = prev_ref[...]

    @pl.when(j > 0)
    def _():
        ext_ref[:, 0:POOL_PAD, :] = ext_ref[:, ts:ts + POOL_PAD, :]

    ext_ref[:, POOL_PAD:, :] = u.reshape(nb, ts, POOL_W)
    e = ext_ref[...].reshape(nb * (POOL_PAD + ts), POOL_W)
    s2 = e + pltpu.roll(e, 1, 0)
    s4 = s2 + pltpu.roll(s2, 2, 0)
    s8 = s4 + pltpu.roll(s4, 4, 0)
    s16 = s8 + pltpu.roll(s8, 8, 0)

    def tile_rows(a):
        return a.reshape(nb, POOL_PAD + ts, POOL_W)[:, POOL_PAD:, :]

    t_idx = lax.broadcasted_iota(jnp.int32, (1, ts, 1), 1)
    pos1 = (pos0 + 1 + j * ts + t_idx).astype(F32)
    col = lax.broadcasted_iota(jnp.int32, (1, 1, POOL_W), 2)
    pooled = None
    for g, (w, sw) in enumerate(zip(POOL_WINDOWS, (s2, s4, s8, s16))):
        val = tile_rows(sw) * (1.0 / jnp.minimum(float(w), pos1))
        if pooled is None:
            pooled = val
        else:
            pooled = jnp.where(col >= g * POOL_GROUP_W, val, pooled)
    diff = (pooled.reshape(rows, POOL_W) - u).astype(BF16)

    mo = _mem_attend(qm, mk_ref, mv_ref, gq_ref[...], gsum_ref[...], nb, ts)

    y = _dot(diff, wpo_ref[...]) + _dot(mo.astype(BF16), w_out_ref[POOL_W:, :])
    xo_ref[...] = (x + y).reshape(nb, ts, D_MODEL)

    @pl.when(j == pl.num_programs(1) - 1)
    def _():
        pool_ref[...] = ext_ref[:, ts + 1:ts + POOL_PAD, :]


def _mixer_a(x, prev, mk, mv, wl, *, nb, ts, pos0):
    b, s, _ = x.shape
    assert b % nb == 0 and s % ts == 0 and (ts >= POOL_PAD or s == ts)
    kern = functools.partial(_mixer_a_kernel, nb=nb, ts=ts, pos0=pos0)
    const = lambda i, j: (0, 0)
    return pl.pallas_call(
        kern,
        grid=(b // nb, s // ts),
        in_specs=[
            pl.BlockSpec((nb, ts, D_MODEL), lambda i, j: (i, j, 0)),
            pl.BlockSpec((1, D_MODEL), const),
            pl.BlockSpec((None, D_MODEL, D_MODEL), lambda i, j: (wl['l_in'], 0, 0)),
            pl.BlockSpec((nb, POOL_STATE, POOL_W), lambda i, j: (i, 0, 0)),
            pl.BlockSpec((None, POOL_W, D_MODEL), lambda i, j: (wl['l_in'], 0, 0)),
            pl.BlockSpec((None, nb, MEM_W, N_MEM), lambda i, j: (wl['l'], i, 0, 0)),
            pl.BlockSpec((None, nb, MEM_W, N_MEM), lambda i, j: (wl['l'], i, 0, 0)),
            pl.BlockSpec((1, MEM_W), const),
            pl.BlockSpec((MEM_W, MEM_W), const),
            pl.BlockSpec((None, D_MODEL, D_MODEL), lambda i, j: (wl['l'], 0, 0)),
        ],
        out_specs=[pl.BlockSpec((nb, ts, D_MODEL), lambda i, j: (i, j, 0)),
                   pl.BlockSpec((nb, POOL_STATE, POOL_W), lambda i, j: (i, 0, 0))],
        out_shape=[jax.ShapeDtypeStruct((b, s, D_MODEL), F32),
                   jax.ShapeDtypeStruct((b, POOL_STATE, POOL_W), F32)],
        scratch_shapes=[pltpu.VMEM((nb, POOL_PAD + ts, POOL_W), F32)],
        compiler_params=_params(2),
        name="mixer_a",
    )(x, wl['g_mix'], wl['w_in'], prev, wl['w_pool_out'],
      mk, mv, wl['g_mem_q'], wl['gsum'], wl['w_out'])


def _ffn_kernel(x_ref, g_ref, w_in_ref, w_out_ref, o_ref):
    x = x_ref[...]
    h = _rms(x, g_ref[...]).astype(BF16)
    acc = x
    for c0 in range(0, D_FF, FFN_CHUNK):
        gate = _dot(h, w_in_ref[:, c0:c0 + FFN_CHUNK])
        up = _dot(h, w_in_ref[:, D_FF + c0:D_FF + c0 + FFN_CHUNK])
        a = (gate * jax.nn.sigmoid(gate) * up).astype(BF16)
        acc = acc + _dot(a, w_out_ref[c0:c0 + FFN_CHUNK, :])
    o_ref[...] = acc


def _ffn(x, g, w_in, w_out, layer, *, tm):
    n = x.shape[0]
    assert n % tm == 0 and D_FF % FFN_CHUNK == 0 and FFN_CHUNK % LANE == 0
    const = lambda i: (0, 0)
    return pl.pallas_call(
        _ffn_kernel,
        grid=(n // tm,),
        in_specs=[
            pl.BlockSpec((tm, D_MODEL), lambda i: (i, 0)),
            pl.BlockSpec((1, D_MODEL), const),
            pl.BlockSpec((None, D_MODEL, 2 * D_FF), lambda i: (layer, 0, 0),
                         pipeline_mode=pl.Buffered(1)),
            pl.BlockSpec((None, D_FF, D_MODEL), lambda i: (layer, 0, 0),
                         pipeline_mode=pl.Buffered(1)),
        ],
        out_specs=pl.BlockSpec((tm, D_MODEL), lambda i: (i, 0)),
        out_shape=jax.ShapeDtypeStruct((n, D_MODEL), F32),
        compiler_params=_params(1),
        name="ffn",
    )(x, g, w_in, w_out)


def _kvrows_kernel(x_ref, g_ref, w_ref, gl_ref, cos_ref, sin_ref, *rest, with_kv):
    if with_kv:
        wk_ref, wv_ref, kv_ref, kn_ref, v_ref = rest
    else:
        (kv_ref,) = rest
    x = x_ref[...]
    h = _rms(x, g_ref[...]).astype(BF16)
    ck = _dot(h, w_ref[...])
    c = _rms(ck[:, :KV_LORA], gl_ref[...])
    kr = (ck[:, KV_LORA:KV_ROW] * cos_ref[...]
          + ck[:, KV_LORA + LANE:KV_LORA + LANE + QK_ROPE] * sin_ref[...])
    kv_ref[:, 0:KV_LORA] = c
    kv_ref[:, KV_LORA:KV_ROW] = kr
    if with_kv:
        cb = c.astype(BF16)
        kn = _dot(cb, wk_ref[0:KV_LORA, :]) + _dot(kr.astype(BF16), wk_ref[KV_LORA:KV_ROW, :])
        is_k = lax.broadcasted_iota(jnp.int32, (1, SLOT), 1) < QK_DIM
        for hd in range(MLA_HEADS):
            sl = slice(hd * SLOT, (hd + 1) * SLOT)
            kh = kn[:, sl]
            ssq = jnp.sum(jnp.where(is_k, kh * kh, 0.0), axis=-1, keepdims=True)
            inv = lax.rsqrt(ssq * (1.0 / QK_DIM) + EPS)
            kn_ref[:, sl] = (kh * inv).astype(BF16)
        vt = _dot_nt(wv_ref[...], cb)
        row = lax.broadcasted_iota(jnp.int32, (V_SLOTS, 1), 0)
        is_one = functools.reduce(
            jnp.logical_or, [row == hd * V_SLOT + V_HEAD for hd in range(MLA_HEADS)])
        v_ref[0] = jnp.where(is_one, 1.0, vt).astype(BF16)


def _kvrows(x, wk, cos32, sin32, *, tm, seq_tiles, with_kv):
    n = x.shape[0]
    const = lambda i: (0, 0)
    in_specs = [
        pl.BlockSpec((tm, D_MODEL), lambda i: (i, 0)),
        pl.BlockSpec((1, D_MODEL), const),
        pl.BlockSpec((D_MODEL, 4 * LANE), const),
        pl.BlockSpec((1, KV_LORA), const),
        pl.BlockSpec((tm, QK_ROPE), lambda i: (i % seq_tiles, 0)),
        pl.BlockSpec((tm, QK_ROPE), lambda i: (i % seq_tiles, 0)),
    ]
    args = [x, wk['norm_kv'], wk['w_dkv_ext'], wk['g_kv_lora'], cos32, sin32]
    out_specs = [pl.BlockSpec((tm, KV_ROW), lambda i: (i, 0))]
    out_shape = [jax.ShapeDtypeStruct((n, KV_ROW), F32)]
    if with_kv:
        in_specs += [pl.BlockSpec((KV_ROW, Q_SLOTS), const),
                     pl.BlockSpec((V_SLOTS, KV_LORA), const)]
        args += [wk['w_kslots'], wk['w_uvt_slots']]
        out_specs += [pl.BlockSpec((tm, Q_SLOTS), lambda i: (i, 0)),
                      pl.BlockSpec((1, V_SLOTS, tm), lambda i: (i // seq_tiles, 0, i % seq_tiles))]
        out_shape += [jax.ShapeDtypeStruct((n, Q_SLOTS), BF16),
                      jax.ShapeDtypeStruct((n // (tm * seq_tiles), V_SLOTS, tm * seq_tiles), BF16)]
    return pl.pallas_call(
        functools.partial(_kvrows_kernel, with_kv=with_kv),
        grid=(n // tm,),
        in_specs=in_specs, out_specs=out_specs, out_shape=out_shape,
        compiler_params=_params(1),
        name="kvrows",
    )(*args)


def _qproj_kernel(x_ref, g_ref, w_in_ref, gql_ref, wq_ref, gq_ref, trig_ref, *rest, absorb):
    if absorb:
        wabs_ref, qabs_ref, qm_ref = rest
    else:
        qf_ref, qm_ref = rest
    x = x_ref[...]
    h = _rms(x, g_ref[...]).astype(BF16)
    proj = _dot(h, w_in_ref[...])
    qm_ref[...] = proj[:, Q_LORA:]
    cq = _rms(proj[:, :Q_LORA], gql_ref[...]).astype(BF16)
    q_all = _dot(cq, wq_ref[...])
    tab = gq_ref[...] * trig_ref[...]
    is_q = lax.broadcasted_iota(jnp.int32, (1, SLOT), 1) < QK_DIM
    for hd in range(MLA_HEADS):
        sl = slice(hd * SLOT, (hd + 1) * SLOT)
        qh = q_all[:, sl]
        ssq = jnp.sum(jnp.where(is_q, qh * qh, 0.0), axis=-1, keepdims=True)
        inv = lax.rsqrt(ssq * (1.0 / QK_DIM) + EPS)
        qf = (qh * tab * (inv * (QK_DIM ** -0.5 * LOG2E))).astype(BF16)
        if absorb:
            qabs_ref[hd] = _dot(qf, wabs_ref[hd])
        else:
            qf_ref[:, sl] = qf


def _qproj(x, wl, wk, trig_t, *, tm, seq_tiles, absorb):
    n = x.shape[0]
    const = lambda i: (0, 0)
    in_specs = [
        pl.BlockSpec((tm, D_MODEL), lambda i: (i, 0)),
        pl.BlockSpec((1, D_MODEL), const),
        pl.BlockSpec((None, D_MODEL, Q_LORA + MEM_W), lambda i: (wl['l_in'], 0, 0)),
        pl.BlockSpec((1, Q_LORA), const),
        pl.BlockSpec((Q_LORA, Q_SLOTS), const),
        pl.BlockSpec((1, SLOT), const),
        pl.BlockSpec((tm, SLOT), lambda i: (i % seq_tiles, 0)),
    ]
    args = [x, wl['g_mix'], wl['w_in'], wl['g_q_lora'], wl['w_uq_slots'], wl['g_q_slot'],
            trig_t]
    if absorb:
        in_specs.append(pl.BlockSpec((MLA_HEADS, SLOT, KV_ROW), lambda i: (0, 0, 0)))
        args.append(wk['w_abs'])
        out_specs = [pl.BlockSpec((MLA_HEADS, tm, KV_ROW), lambda i: (0, i, 0))]
        out_shape = [jax.ShapeDtypeStruct((MLA_HEADS, n, KV_ROW), F32)]
    else:
        out_specs = [pl.BlockSpec((tm, Q_SLOTS), lambda i: (i, 0))]
        out_shape = [jax.ShapeDtypeStruct((n, Q_SLOTS), BF16)]
    out_specs.append(pl.BlockSpec((tm, MEM_W), lambda i: (i, 0)))
    out_shape.append(jax.ShapeDtypeStruct((n, MEM_W), F32))
    return pl.pallas_call(
        functools.partial(_qproj_kernel, absorb=absorb),
        grid=(n // tm,),
        in_specs=in_specs, out_specs=out_specs, out_shape=out_shape,
        compiler_params=_params(1),
        name="qproj",
    )(*args)


def _mla_prompt_kernel(q_ref, k_ref, vt_ref, o_ref, *, tq, nh):
    qi = pl.program_id(2)
    q = q_ref[0]
    q_idx = qi * tq + lax.broadcasted_iota(jnp.int32, (1, tq), 1)
    heads = range(nh)
    slots = [slice(hd * SLOT, (hd + 1) * SLOT) for hd in heads]

    def block(kb, carry, masked):
        start = pl.multiple_of(kb * tq, tq)
        k = k_ref[0, pl.ds(start, tq), :]
        vt = vt_ref[0, :, pl.ds(start, tq)]
        sts = [_dot_nt(k[:, sl], q[:, sl]) for sl in slots]
        if masked:
            k_idx = start + lax.broadcasted_iota(jnp.int32, (tq, 1), 0)
            sts = [jnp.where(k_idx <= q_idx, st, NEG) for st in sts]
        ms = [jnp.maximum(carry[2 * hd], jnp.max(sts[hd], axis=0, keepdims=True))
              for hd in heads]
        ps = [jnp.exp2(sts[hd] - ms[hd]).astype(BF16) for hd in heads]
        pvs = [_dot(vt[hd * V_SLOT:(hd + 1) * V_SLOT, :], ps[hd]) for hd in heads]
        new = []
        for hd in heads:
            alpha = jnp.exp2(carry[2 * hd] - ms[hd])
            new += [ms[hd], carry[2 * hd + 1] * alpha + pvs[hd]]
        return tuple(new)

    init = (jnp.full((1, tq), NEG, F32), jnp.zeros((V_SLOT, tq), F32)) * nh
    carry = lax.fori_loop(0, qi, lambda kb, c: block(kb, c, False), init)
    carry = block(qi, carry, True)
    outs = [carry[2 * hd + 1][:V_HEAD] / carry[2 * hd + 1][V_HEAD:V_HEAD + 1] for hd in heads]
    o_ref[0] = jnp.concatenate(outs, axis=0).T.astype(BF16)


def _mla_prompt(qf, kn, vt, *, tq, nh):
    b, s, _ = qf.shape
    assert s % tq == 0 and MLA_HEADS % nh == 0
    return pl.pallas_call(
        functools.partial(_mla_prompt_kernel, tq=tq, nh=nh),
        grid=(b, MLA_HEADS // nh, s // tq),
        in_specs=[
            pl.BlockSpec((1, tq, nh * SLOT), lambda i, hg, qi: (i, qi, hg)),
            pl.BlockSpec((1, s, nh * SLOT), lambda i, hg, qi: (i, 0, hg)),
            pl.BlockSpec((1, nh * V_SLOT, s), lambda i, hg, qi: (i, hg, 0)),
        ],
        out_specs=pl.BlockSpec((1, tq, nh * V_HEAD), lambda i, hg, qi: (i, qi, hg)),
        out_shape=jax.ShapeDtypeStruct((b, s, MLA_HEADS * V_HEAD), BF16),
        compiler_params=_params(3),
        name="mla_prompt",
    )(qf, kn, vt)


Q_ROWS = MLA_HEADS * 8
N_KN = MLA_HEADS * QK_NOPE
SAMPLE_SUB_PAGES = (8, 2)


def _mla_sample_kernel(pt_ref, qabs_ref, wukt_ref, kvnew_ref, cache_ref, *rest, pps, sub_pages,
                       n_tok, first_layer):
    if first_layer:
        o_ref, kinv_out_ref, m_ref, l_ref, acc_ref, lhs_ref, buf_ref, sem = rest
    else:
        kinv_in_ref, o_ref, m_ref, l_ref, acc_ref, lhs_ref, buf_ref, sem = rest
    seq = pl.program_id(0)
    step = pl.program_id(1)
    n_steps = pl.num_programs(1)
    flat = seq * n_steps + step
    total = pl.num_programs(0) * n_steps
    n_slots = buf_ref.shape[0]
    ahead = n_slots - 1
    slot = lax.rem(flat, n_slots)

    def page_copies(flat_i):
        seq_i, step_i, slot_i = flat_i // n_steps, lax.rem(flat_i, n_steps), lax.rem(flat_i, n_slots)
        return [pltpu.make_async_copy(cache_ref.at[pt_ref[seq_i, step_i * pps + i]],
                                      buf_ref.at[slot_i, i], sem.at[slot_i])
                for i in range(pps)]

    @pl.when(flat == 0)
    def _():
        for d in range(ahead):
            for cp in page_copies(flat + d):
                cp.start()

    @pl.when(flat + ahead < total)
    def _():
        for cp in page_copies(flat + ahead):
            cp.start()

    for cp in page_copies(flat):
        cp.wait()

    qabs = qabs_ref[...].reshape(Q_ROWS, KV_ROW)
    q_rope = qabs[:, KV_LORA:].astype(BF16)

    @pl.when(step == 0)
    def _():
        m_ref[...] = jnp.full(m_ref.shape, NEG, F32)
        l_ref[...] = jnp.zeros(l_ref.shape, F32)
        acc_ref[...] = jnp.zeros(acc_ref.shape, F32)
        lhs_ref[0:N_KN, :] = wukt_ref[...]
        lhs_ref[N_KN:, :] = qabs[:, :KV_LORA].astype(BF16)

    def scores(ct, krt, kinv, causal):
        t = ct.shape[1]
        if kinv is None:
            r1 = _dot(lhs_ref[...], ct)
            kn = r1[:N_KN]
            ss = jnp.sum((kn * kn).reshape(MLA_HEADS, QK_NOPE, t), axis=1)
            krsq = jnp.sum(krt * krt, axis=0, keepdims=True)
            kinv = lax.rsqrt((ss + krsq) * (1.0 / QK_DIM) + EPS)
            s_lat = r1[N_KN:]
        else:
            s_lat = _dot(lhs_ref[N_KN:, :], ct)
        sc = s_lat + _dot(q_rope, krt.astype(BF16))
        sc = sc.reshape(MLA_HEADS, n_tok, t) * kinv[:, None, :]
        if causal:
            tok = lax.broadcasted_iota(jnp.int32, (1, n_tok, t), 1)
            key = lax.broadcasted_iota(jnp.int32, (1, n_tok, t), 2)
            sc = jnp.where(key <= tok, sc, NEG)
        return sc.reshape(Q_ROWS, t), kinv

    def accumulate(state, sc, ct):
        m_prev, l_prev, acc = state
        m_new = jnp.maximum(m_prev, jnp.max(sc, axis=-1, keepdims=True))
        alpha = jnp.exp2(m_prev - m_new)
        p = jnp.exp2(sc - m_new)
        l_new = alpha * l_prev + jnp.sum(p, axis=-1, keepdims=True)
        return m_new, l_new, alpha * acc + _dot_nt(p.astype(BF16), ct)

    n_sub = pps // sub_pages
    t_sub = sub_pages * buf_ref.shape[3]

    def sub_scores(j):
        pages = range(j * sub_pages, (j + 1) * sub_pages)
        ct = jnp.concatenate([buf_ref[slot, i, 0:KV_LORA, :].astype(BF16) for i in pages], axis=1)
        krt = jnp.concatenate([buf_ref[slot, i, KV_LORA:, :] for i in pages], axis=1)
        if first_layer:
            sc, kinv = scores(ct, krt, None, False)
            kinv_out_ref[0, :, j * t_sub:(j + 1) * t_sub] = kinv
        else:
            sc, _ = scores(ct, krt, kinv_in_ref[0, :, j * t_sub:(j + 1) * t_sub], False)
        return sc, ct

    if first_layer:
        state = (m_ref[...], l_ref[...], acc_ref[...])
        pending = sub_scores(0)
        for j in range(1, n_sub):
            nxt = sub_scores(j)
            state = accumulate(state, *pending)
            pending = nxt
        m_ref[...], l_ref[...], acc_ref[...] = accumulate(state, *pending)
    else:
        subs = [sub_scores(j) for j in range(n_sub)]
        m_prev = m_ref[...]
        m_blk = functools.reduce(jnp.maximum, [sc for sc, _ in subs])
        m_new = jnp.maximum(m_prev, jnp.max(m_blk, axis=-1, keepdims=True))
        alpha = jnp.exp2(m_prev - m_new)
        ps = [jnp.exp2(sc - m_new) for sc, _ in subs]
        l_ref[...] = alpha * l_ref[...] + sum(jnp.sum(p, axis=-1, keepdims=True) for p in ps)
        pv = sum(_dot_nt(p.astype(BF16), ct) for p, (_, ct) in zip(ps, subs))
        acc_ref[...] = alpha * acc_ref[...] + pv
        m_ref[...] = m_new

    @pl.when(step == pl.num_programs(1) - 1)
    def _():
        kvt = kvnew_ref[0]
        ct = kvt[:KV_LORA].astype(BF16)
        sc, _ = scores(ct, kvt[KV_LORA:], None, True)
        _, l_fin, acc = accumulate((m_ref[...], l_ref[...], acc_ref[...]), sc, ct)
        o = acc / l_fin
        for hd in range(MLA_HEADS):
            o_ref[0, :, hd * KV_LORA:(hd + 1) * KV_LORA] = o[hd * n_tok:(hd + 1) * n_tok, :]


def _mla_sample(page_table, qabs, w_ukt, kv_new_t, cache_t, kinv, *, pps):
    n_seq, n_pages = page_table.shape
    n_tok = qabs.shape[1] // n_seq
    page = cache_t.shape[2]
    first_layer = kinv is None
    t_step = pps * page
    assert n_tok == 8 and n_pages % pps == 0

    in_specs = [
        pl.BlockSpec((MLA_HEADS, 1, n_tok, KV_ROW), lambda b, s, pt: (0, b, 0, 0)),
        pl.BlockSpec((N_KN, KV_LORA), lambda b, s, pt: (0, 0)),
        pl.BlockSpec((1, KV_ROW, LANE), lambda b, s, pt: (b, 0, 0)),
        pl.BlockSpec(memory_space=pl.ANY),
    ]
    args = [qabs.reshape(MLA_HEADS, n_seq, n_tok, KV_ROW), w_ukt, kv_new_t, cache_t]
    out_specs = [pl.BlockSpec((1, n_tok, MLA_HEADS * KV_LORA), lambda b, s, pt: (b, 0, 0))]
    out_shape = [jax.ShapeDtypeStruct((n_seq, n_tok, MLA_HEADS * KV_LORA), F32)]
    kinv_spec = pl.BlockSpec((1, MLA_HEADS, t_step), lambda b, s, pt: (b, 0, s))
    if first_layer:
        out_specs.append(kinv_spec)
        out_shape.append(jax.ShapeDtypeStruct((n_seq, MLA_HEADS, n_pages * page), F32))
    else:
        in_specs.append(kinv_spec)
        args.append(kinv)
    grid_spec = pltpu.PrefetchScalarGridSpec(
        num_scalar_prefetch=1,
        grid=(n_seq, n_pages // pps),
        in_specs=in_specs,
        out_specs=out_specs,
        scratch_shapes=[
            pltpu.VMEM((Q_ROWS, 1), F32),
            pltpu.VMEM((Q_ROWS, 1), F32),
            pltpu.VMEM((Q_ROWS, KV_LORA), F32),
            pltpu.VMEM((N_KN + Q_ROWS, KV_LORA), BF16),
            pltpu.VMEM((SAMPLE_PAGE_SLOTS, pps, KV_ROW, page), F32),
            pltpu.SemaphoreType.DMA((SAMPLE_PAGE_SLOTS,)),
        ],
    )
    return pl.pallas_call(
        functools.partial(_mla_sample_kernel, pps=pps,
                          sub_pages=SAMPLE_SUB_PAGES[0 if first_layer else 1], n_tok=n_tok,
                          first_layer=first_layer),
        grid_spec=grid_spec,
        out_shape=out_shape,
        compiler_params=_params(2),
        name="mla_sample",
    )(page_table, *args)


def _out_b_kernel(x_ref, tok_ref, qm_ref, mk_ref, mv_ref, gq_ref, gsum_ref, w_out_ref,
                  *rest, nb, ts, latent):
    if latent:
        wuv_ref, o_ref = rest
        tok = _dot(tok_ref[...].astype(BF16), wuv_ref[...]).astype(BF16)
    else:
        (o_ref,) = rest
        tok = tok_ref[...]
    mo = _mem_attend(qm_ref[...], mk_ref, mv_ref, gq_ref[...], gsum_ref[...], nb, ts)
    y = _dot(tok, w_out_ref[0:POOL_W, :]) + _dot(mo.astype(BF16), w_out_ref[POOL_W:, :])
    o_ref[...] = x_ref[...] + y


def _out_b(x, tok, qm, mk, mv, wl, wk, *, nb, ts, latent):
    n = x.shape[0]
    rows = nb * ts
    tok_w = tok.shape[1]
    const = lambda i: (0, 0)
    seq_len = n // mk.shape[1]
    mem_idx = lambda i: (wl['l'], i * rows // (seq_len * nb), 0, 0)
    in_specs = [
        pl.BlockSpec((rows, D_MODEL), lambda i: (i, 0)),
        pl.BlockSpec((rows, tok_w), lambda i: (i, 0)),
        pl.BlockSpec((rows, MEM_W), lambda i: (i, 0)),
        pl.BlockSpec((None, nb, MEM_W, N_MEM), mem_idx),
        pl.BlockSpec((None, nb, MEM_W, N_MEM), mem_idx),
        pl.BlockSpec((1, MEM_W), const),
        pl.BlockSpec((MEM_W, MEM_W), const),
        pl.BlockSpec((None, D_MODEL, D_MODEL), lambda i: (wl['l'], 0, 0)),
    ]
    args = [x, tok, qm, mk, mv, wl['g_mem_q'], wl['gsum'], wl['w_out']]
    if latent:
        in_specs.append(pl.BlockSpec((MLA_HEADS * KV_LORA, MLA_HEADS * V_HEAD), const))
        args.append(wk['w_uv_bd'])
    return pl.pallas_call(
        functools.partial(_out_b_kernel, nb=nb, ts=ts, latent=latent),
        grid=(n // rows,),
        in_specs=in_specs,
        out_specs=pl.BlockSpec((rows, D_MODEL), lambda i: (i, 0)),
        out_shape=jax.ShapeDtypeStruct((n, D_MODEL), F32),
        compiler_params=_params(1),
        name="out_b",
    )(*args)


def _rope_tables(pos):
    half = QK_ROPE // 2
    inv_freq = ROPE_BASE ** (-jnp.arange(half, dtype=F32) / half)
    ang = pos.astype(F32)[:, None] * inv_freq[None, :]
    cos32 = jnp.concatenate([jnp.cos(ang)] * 2, axis=-1)
    sin32 = jnp.concatenate([jnp.sin(ang)] * 2, axis=-1)
    n = pos.shape[0]
    trig_t = jnp.concatenate([jnp.ones((n, QK_NOPE), F32), cos32, sin32], axis=-1)
    return cos32, sin32, trig_t


def _rot_cols(w):
    half = QK_ROPE // 2
    return jnp.concatenate([-w[..., half:], w[..., :half]], axis=-1)


def _prep_weights(norm_mix, norm_ffn, w_out, w_ffn_in, w_ffn_out, g_mem_q, w_in_a, w_pool_grp,
                  pool_scale, w_in_b, g_q_lora, w_uq, g_q, norm_kv, w_dkv, g_kv_lora, w_uk, w_uv,
                  g_k_nope, g_k_rope):
    half = QK_ROPE // 2
    gsum = jnp.asarray(np.kron(np.eye(MEM_HEADS), np.ones((MEM_HEAD_DIM, MEM_HEAD_DIM))), BF16)
    w_out_b, w_ffn_in_b, w_ffn_out_b = (w.astype(BF16) for w in (w_out, w_ffn_in, w_ffn_out))
    w_in_a_b, w_in_b_b = w_in_a.astype(BF16), w_in_b.astype(BF16)
    w_pool_out = _pool_out_weights(w_pool_grp, pool_scale, w_out)
    layers = []
    for l in range(DEPTH):
        wl = dict(
            l=l, g_mix=norm_mix[l][None], g_ffn=norm_ffn[l][None],
            w_out=w_out_b, w_ffn_in=w_ffn_in_b, w_ffn_out=w_ffn_out_b,
            g_mem_q=jnp.tile(g_mem_q[l], MEM_HEADS)[None], gsum=gsum)
        if l < N_A_LAYERS:
            wl.update(w_in=w_in_a_b, l_in=l, w_pool_out=w_pool_out)
        else:
            j = l - N_A_LAYERS
            wq = w_uq[j].reshape(Q_LORA, MLA_HEADS, QK_DIM)
            w_slots = jnp.concatenate([wq, _rot_cols(wq[..., QK_NOPE:])],
                                      axis=-1).reshape(Q_LORA, Q_SLOTS)
            gk2 = jnp.concatenate([g_k_rope, g_k_rope])
            gq = g_q[j]
            g_rot = jnp.concatenate([gq[QK_NOPE + half:], gq[QK_NOPE:QK_NOPE + half]])
            g_slot = jnp.concatenate([gq[:QK_NOPE] * g_k_nope, gq[QK_NOPE:] * gk2, g_rot * gk2])
            wl.update(w_in=w_in_b_b, l_in=j, g_q_lora=g_q_lora[j][None],
                      w_uq_slots=w_slots.astype(BF16), g_q_slot=g_slot[None])
        layers.append(wl)

    zcol = jnp.zeros((D_MODEL, LANE - QK_ROPE), F32)
    w_dkv_ext = jnp.concatenate([w_dkv, zcol, _rot_cols(w_dkv[:, KV_LORA:]), zcol], axis=-1)
    wuk_h = w_uk.reshape(KV_LORA, MLA_HEADS, QK_NOPE)
    top = jnp.concatenate([wuk_h, jnp.zeros((KV_LORA, MLA_HEADS, SLOT - QK_NOPE), F32)], axis=-1)
    eye = jnp.eye(QK_ROPE, dtype=F32)
    bot = jnp.concatenate([jnp.zeros((QK_ROPE, QK_NOPE), F32), eye, eye], axis=-1)
    bot = jnp.broadcast_to(bot[:, None, :], (QK_ROPE, MLA_HEADS, SLOT))
    w_kslots = jnp.concatenate([top, bot], axis=0).reshape(KV_ROW, Q_SLOTS)
    w_abs = jnp.zeros((MLA_HEADS, SLOT, KV_ROW), F32)
    w_abs = w_abs.at[:, :QK_NOPE, :KV_LORA].set(jnp.transpose(wuk_h, (1, 2, 0)))
    eyes = jnp.broadcast_to(eye, (MLA_HEADS,) + eye.shape)
    w_abs = w_abs.at[:, QK_NOPE:QK_DIM, KV_LORA:].set(eyes).at[:, QK_DIM:, KV_LORA:].set(eyes)
    wuv_h = w_uv.reshape(KV_LORA, MLA_HEADS, V_HEAD)
    w_uvt_slots = jnp.concatenate(
        [jnp.transpose(wuv_h, (1, 2, 0)), jnp.zeros((MLA_HEADS, V_SLOT - V_HEAD, KV_LORA), F32)],
        axis=1).reshape(V_SLOTS, KV_LORA)
    w_uv_bd = jnp.einsum('chd,hg->hcgd', wuv_h, jnp.eye(MLA_HEADS, dtype=F32))
    w_uv_bd = w_uv_bd.reshape(MLA_HEADS * KV_LORA, MLA_HEADS * V_HEAD)
    shared = dict(
        norm_kv=norm_kv[None], g_kv_lora=g_kv_lora[None],
        w_dkv_ext=w_dkv_ext.astype(BF16), w_kslots=w_kslots.astype(BF16),
        w_uvt_slots=w_uvt_slots.astype(BF16), w_abs=w_abs.astype(BF16),
        w_ukt=jnp.transpose(w_uk).astype(BF16), w_uv_bd=w_uv_bd.astype(BF16))
    return layers, shared


def _trunk(x, pos0, pool_prev, mem_k, mem_v, layers, shared, *, nb, ts, tm, past):
    b, s, _ = x.shape
    n = b * s
    pos = pos0 + jnp.arange(s)
    tabs = _rope_tables(pos)
    if s >= tm:
        tmk, seq_tiles = tm, s // tm
    else:
        reps = min(tm, n) // s
        tmk, seq_tiles = reps * s, 1
        tabs = tuple(jnp.tile(t, (reps, 1)) for t in tabs)
    cos32, sin32, trig_t = tabs
    pool_new = []
    kv_new = None
    for l in range(DEPTH):
        wl = layers[l]
        if l < N_A_LAYERS:
            x, st = _mixer_a(x, pool_prev[l], mem_k, mem_v, wl, nb=nb, ts=ts, pos0=pos0)
            pool_new.append(st)
            xf = x.reshape(n, D_MODEL)
        else:
            xf = x.reshape(n, D_MODEL)
            if l == N_A_LAYERS:
                if past is None:
                    kv_new, kn, vt = _kvrows(xf, shared, cos32, sin32, tm=tmk,
                                             seq_tiles=seq_tiles, with_kv=True)
                else:
                    (kv_new,) = _kvrows(xf, shared, cos32, sin32, tm=tmk,
                                        seq_tiles=seq_tiles, with_kv=False)
            if past is None:
                qf, qm = _qproj(xf, wl, shared, trig_t, tm=tmk, seq_tiles=seq_tiles,
                                absorb=False)
                tok = _mla_prompt(qf.reshape(b, s, Q_SLOTS), kn.reshape(b, s, Q_SLOTS), vt,
                                  tq=ATTN_TILE, nh=ATTN_HEADS_PER_STEP)
                tok = tok.reshape(n, MLA_HEADS * V_HEAD)
                nb_o, ts_o = 1, tmk
            else:
                cache_t, page_table = past
                qabs, qm = _qproj(xf, wl, shared, trig_t, tm=tmk, seq_tiles=seq_tiles,
                                  absorb=True)
                if l == N_A_LAYERS:
                    kv_new_t = jnp.pad(jnp.swapaxes(kv_new.reshape(b, s, KV_ROW), 1, 2),
                                       ((0, 0), (0, 0), (0, LANE - s)))
                    tok, kinv = _mla_sample(page_table, qabs, shared['w_ukt'], kv_new_t, cache_t,
                                            None, pps=SAMPLE_PAGES_PER_STEP)
                else:
                    (tok,) = _mla_sample(page_table, qabs, shared['w_ukt'], kv_new_t, cache_t,
                                         kinv, pps=SAMPLE_PAGES_PER_STEP)
                tok = tok.reshape(n, MLA_HEADS * KV_LORA)
                nb_o, ts_o = nb, ts
            xf = _out_b(xf, tok, qm, mem_k, mem_v, wl, shared, nb=nb_o, ts=ts_o,
                        latent=past is not None)
        xf = _ffn(xf, wl['g_ffn'], wl['w_ffn_in'], wl['w_ffn_out'], wl['l'],
                  tm=min(FFN_ROWS, n))
        x = xf.reshape(b, s, D_MODEL)
    return x, kv_new.reshape(b, s, KV_ROW), jnp.stack(pool_new)


def kernel(x_prompt, x_sample, cache_kv, state_pool, cache_mem_k, cache_mem_v, page_table, mem_prompt,
           norm_mix, norm_ffn, w_out, w_ffn_in, w_ffn_out, norm_mem, w_mem_kv, g_mem_q, g_mem_k,
           w_in_a, w_pool_grp, pool_scale, w_in_b, g_q_lora, w_uq, g_q,
           norm_kv, w_dkv, g_kv_lora, w_uk, w_uv, g_k_nope, g_k_rope):
    layers, shared = _prep_weights(norm_mix, norm_ffn, w_out, w_ffn_in, w_ffn_out, g_mem_q, w_in_a,
                                   w_pool_grp, pool_scale, w_in_b, g_q_lora, w_uq, g_q, norm_kv,
                                   w_dkv, g_kv_lora, w_uk, w_uv, g_k_nope, g_k_rope)
    b_p, s_p, _ = x_prompt.shape
    n_seq, s_s, _ = x_sample.shape

    mk_p, mv_p = _memkv(mem_prompt, norm_mem[:, None, :],
                        jnp.swapaxes(w_mem_kv, 1, 2).astype(BF16), g_mem_k[:, None, :, None])
    pool0 = jnp.zeros((N_A_LAYERS, b_p, POOL_STATE, POOL_W), F32)
    y_p, kv_p, pool_p = _trunk(x_prompt, 0, pool0, mk_p, mv_p, layers, shared,
                               nb=1, ts=MIXER_SEQ_TILE, tm=TOKEN_ROWS, past=None)

    past_len = page_table.shape[1] * cache_kv.shape[1]
    mk_s = jnp.transpose(cache_mem_k, (0, 1, 3, 4, 2)).reshape(DEPTH, n_seq, MEM_W, N_MEM)
    mv_s = jnp.transpose(cache_mem_v, (0, 1, 3, 4, 2)).reshape(DEPTH, n_seq, MEM_W, N_MEM)
    cache_t = jnp.swapaxes(cache_kv, 1, 2)
    y_s, kv_s, pool_s = _trunk(x_sample, past_len, state_pool, mk_s, mv_s, layers, shared,
                               nb=SAMPLE_SEQS_PER_STEP, ts=s_s, tm=TOKEN_ROWS,
                               past=(cache_t, page_table))

    def mem_out(a):
        a = a.reshape(DEPTH, b_p, MEM_HEADS, MEM_HEAD_DIM, N_MEM)
        return jnp.transpose(a, (0, 1, 4, 2, 3))

    return (y_p, y_s, kv_p, kv_s, pool_p, pool_s, mem_out(mk_p), mem_out(mv_p))
```

```python
import functools

import numpy as np
import jax
import jax.numpy as jnp
from jax import lax
from jax.experimental import pallas as pl
from jax.experimental.pallas import tpu as pltpu

D_MODEL = 1024
DEPTH = 4
N_A_LAYERS = 2
MEM_HEADS = 4
MEM_HEAD_DIM = 64
MEM_W = MEM_HEADS * MEM_HEAD_DIM
N_MEM = 256
POOL_W = D_MODEL - MEM_W
POOL_WINDOWS = (2, 4, 8, 16)
POOL_GROUP_W = POOL_W // len(POOL_WINDOWS)
POOL_STATE = max(POOL_WINDOWS) - 1
POOL_PAD = POOL_STATE + 1
MLA_HEADS = 12
QK_NOPE = 64
QK_ROPE = 32
QK_DIM = QK_NOPE + QK_ROPE
V_HEAD = 64
KV_LORA = 256
Q_LORA = 384
KV_ROW = KV_LORA + QK_ROPE
ROPE_BASE = 10000.0
D_FF = 2816
EPS = 1e-6
NEG = -1e30
LOG2E = 1.4426950408889634

LANE = 128
BF16_SUBLANES = 16
SLOT = LANE
Q_SLOTS = MLA_HEADS * SLOT
V_SLOT = V_HEAD + BF16_SUBLANES
V_SLOTS = MLA_HEADS * V_SLOT
VMEM_LIMIT = 56 * 1024 * 1024

MIXER_SEQ_TILE = 512
TOKEN_ROWS = 1024
FFN_ROWS = 512
FFN_CHUNK = 2816
ATTN_TILE = 256
ATTN_HEADS_PER_STEP = 12
SAMPLE_SEQS_PER_STEP = 16
SAMPLE_PAGES_PER_STEP = 64
SAMPLE_PAGE_SLOTS = 3

BF16 = jnp.bfloat16
F32 = jnp.float32
NT_DIMS = (((1,), (1,)), ((), ()))


def _params(n_grid):
    return pltpu.CompilerParams(dimension_semantics=("arbitrary",) * n_grid,
                                vmem_limit_bytes=VMEM_LIMIT)


def _dot(a, b):
    return jnp.dot(a, b, preferred_element_type=F32)


def _dot_nt(a, b):
    return lax.dot_general(a, b, NT_DIMS, preferred_element_type=F32)


def _split_dot(x, w):
    hi = x.astype(BF16)
    lo = (x - hi.astype(F32)).astype(BF16)
    return _dot(hi, w) + _dot(lo, w)


def _rms(x, g):
    return x * lax.rsqrt(jnp.mean(x * x, axis=-1, keepdims=True) + EPS) * g


def _mem_attend(qm, mkt_ref, mvt_ref, gq, gsum, nb, ts):
    ssq = _split_dot(qm * qm, gsum)
    qn = qm * lax.rsqrt(ssq * (1.0 / MEM_HEAD_DIM) + EPS) * (gq * MEM_HEAD_DIM ** -0.5)
    q3 = qn.reshape(nb, ts, MEM_W)
    mkt = mkt_ref[...].astype(BF16)
    mvt = mvt_ref[...].astype(BF16)
    lane = lax.broadcasted_iota(jnp.int32, (1, 1, MEM_W), 2)
    heads = range(MEM_HEADS)
    in_head = [(lane >= h * MEM_HEAD_DIM) & (lane < (h + 1) * MEM_HEAD_DIM) for h in heads]
    qs = [jnp.where(in_head[h], q3, 0.0).astype(BF16) for h in heads]
    ss = [jnp.einsum('bqd,bdm->bqm', qs[h], mkt, preferred_element_type=F32) for h in heads]
    ps = [jnp.exp(s - jnp.max(s, axis=-1, keepdims=True)) for s in ss]
    ps = [(p / jnp.sum(p, axis=-1, keepdims=True)).astype(BF16) for p in ps]
    os_ = [jnp.einsum('bqm,bdm->bqd', p, mvt, preferred_element_type=F32) for p in ps]
    out = jnp.where(in_head[0], os_[0], 0.0)
    for h in range(1, MEM_HEADS):
        out = jnp.where(in_head[h], os_[h], out)
    return out.reshape(nb * ts, MEM_W)


def _memkv_kernel(mem_ref, gnorm_ref, wt_ref, gk_ref, kt_ref, vt_ref):
    x = mem_ref[0]
    xn = x * lax.rsqrt(jnp.mean(x * x, axis=-1, keepdims=True) + EPS)
    hm = (xn * gnorm_ref[0]).astype(BF16)
    kvt = _dot_nt(wt_ref[0], hm)
    kt = kvt[:MEM_W].reshape(MEM_HEADS, MEM_HEAD_DIM, N_MEM)
    inv = lax.rsqrt(jnp.mean(kt * kt, axis=1, keepdims=True) + EPS)
    kt_ref[0, 0] = (kt * inv * gk_ref[0]).reshape(MEM_W, N_MEM)
    vt_ref[0, 0] = kvt[MEM_W:]


def _memkv(mem, norm_mem, w_mem_kv_t, g_mem_k):
    b = mem.shape[0]
    return pl.pallas_call(
        _memkv_kernel,
        grid=(DEPTH, b),
        in_specs=[
            pl.BlockSpec((1, N_MEM, D_MODEL), lambda l, i: (i, 0, 0)),
            pl.BlockSpec((1, 1, D_MODEL), lambda l, i: (l, 0, 0)),
            pl.BlockSpec((1, 2 * MEM_W, D_MODEL), lambda l, i: (l, 0, 0)),
            pl.BlockSpec((1, 1, MEM_HEAD_DIM, 1), lambda l, i: (l, 0, 0, 0)),
        ],
        out_specs=[pl.BlockSpec((1, 1, MEM_W, N_MEM), lambda l, i: (l, i, 0, 0)),
                   pl.BlockSpec((1, 1, MEM_W, N_MEM), lambda l, i: (l, i, 0, 0))],
        out_shape=[jax.ShapeDtypeStruct((DEPTH, b, MEM_W, N_MEM), F32)] * 2,
        compiler_params=_params(2),
        name="memkv",
    )(mem, norm_mem, w_mem_kv_t, g_mem_k)


def _pool_out_kernel(wp_ref, scale_ref, w_out_ref, o_ref):
    for g in range(len(POOL_WINDOWS)):
        rows = slice(g * POOL_GROUP_W, (g + 1) * POOL_GROUP_W)
        a = wp_ref[0, g]
        b = w_out_ref[0, rows, :] * scale_ref[0, rows, :]
        a_hi, b_hi = a.astype(BF16), b.astype(BF16)
        a_lo = (a - a_hi.astype(F32)).astype(BF16)
        b_lo = (b - b_hi.astype(F32)).astype(BF16)
        o_ref[0, rows, :] = (_dot(a_hi, b_hi) + _dot(a_hi, b_lo) + _dot(a_lo, b_hi)).astype(BF16)


def _pool_out_weights(w_pool_grp, pool_scale, w_out):
    n_groups = len(POOL_WINDOWS)
    return pl.pallas_call(
        _pool_out_kernel,
        grid=(N_A_LAYERS,),
        in_specs=[
            pl.BlockSpec((1, n_groups, POOL_GROUP_W, POOL_GROUP_W), lambda l: (l, 0, 0, 0)),
            pl.BlockSpec((1, POOL_W, 1), lambda l: (l, 0, 0)),
            pl.BlockSpec((1, POOL_W, D_MODEL), lambda l: (l, 0, 0)),
        ],
        out_specs=pl.BlockSpec((1, POOL_W, D_MODEL), lambda l: (l, 0, 0)),
        out_shape=jax.ShapeDtypeStruct((N_A_LAYERS, POOL_W, D_MODEL), BF16),
        compiler_params=_params(1),
        name="pool_out_weights",
    )(w_pool_grp, pool_scale[:, :, None], w_out)


def _mixer_a_kernel(x_ref, g_ref, w_in_ref, prev_ref, wpo_ref,
                    mk_ref, mv_ref, gq_ref, gsum_ref, w_out_ref,
                    xo_ref, pool_ref, ext_ref, *, nb, ts, pos0):
    j = pl.program_id(1)
    rows = nb * ts
    x = x_ref[...].reshape(rows, D_MODEL)
    h = _rms(x, g_ref[...]).astype(BF16)
    proj = _dot(h, w_in_ref[...])
    u = proj[:, :POOL_W]
    qm = proj[:, POOL_W:]

    @pl.when(j == 0)
    def _():
        ext_ref[:, 0:POOL_PAD, :] = jnp.zeros((nb, POOL_PAD, POOL_W), F32)
        ext_ref[:, 1:POOL_PAD, :] = prev_ref[...]

    @pl.when(j > 0)
    def _():
        ext_ref[:, 0:POOL_PAD, :] = ext_ref[:, ts:ts + POOL_PAD, :]

    ext_ref[:, POOL_PAD:, :] = u.reshape(nb, ts, POOL_W)
    e = ext_ref[...].reshape(nb * (POOL_PAD + ts), POOL_W)
    s2 = e + pltpu.roll(e, 1, 0)
    s4 = s2 + pltpu.roll(s2, 2, 0)
    s8 = s4 + pltpu.roll(s4, 4, 0)
    s16 = s8 + pltpu.roll(s8, 8, 0)

    def tile_rows(a):
        return a.reshape(nb, POOL_PAD + ts, POOL_W)[:, POOL_PAD:, :]

    t_idx = lax.broadcasted_iota(jnp.int32, (1, ts, 1), 1)
    pos1 = (pos0 + 1 + j * ts + t_idx).astype(F32)
    col = lax.broadcasted_iota(jnp.int32, (1, 1, POOL_W), 2)
    pooled = None
    for g, (w, sw) in enumerate(zip(POOL_WINDOWS, (s2, s4, s8, s16))):
        val = tile_rows(sw) * (1.0 / jnp.minimum(float(w), pos1))
        if pooled is None:
            pooled = val
        else:
            pooled = jnp.where(col >= g * POOL_GROUP_W, val, pooled)
    diff = (pooled.reshape(rows, POOL_W) - u).astype(BF16)

    mo = _mem_attend(qm, mk_ref, mv_ref, gq_ref[...], gsum_ref[...], nb, ts)

    y = _dot(diff, wpo_ref[...]) + _dot(mo.astype(BF16), w_out_ref[POOL_W:, :])
    xo_ref[...] = (x + y).reshape(nb, ts, D_MODEL)

    @pl.when(j == pl.num_programs(1) - 1)
    def _():
        pool_ref[...] = ext_ref[:, ts + 1:ts + POOL_PAD, :]


def _mixer_a(x, prev, mk, mv, wl, *, nb, ts, pos0):
    b, s, _ = x.shape
    assert b % nb == 0 and s % ts == 0 and (ts >= POOL_PAD or s == ts)
    kern = functools.partial(_mixer_a_kernel, nb=nb, ts=ts, pos0=pos0)
    const = lambda i, j: (0, 0)
    return pl.pallas_call(
        kern,
        grid=(b // nb, s // ts),
        in_specs=[
            pl.BlockSpec((nb, ts, D_MODEL), lambda i, j: (i, j, 0)),
            pl.BlockSpec((1, D_MODEL), const),
            pl.BlockSpec((None, D_MODEL, D_MODEL), lambda i, j: (wl['l_in'], 0, 0)),
            pl.BlockSpec((nb, POOL_STATE, POOL_W), lambda i, j: (i, 0, 0)),
            pl.BlockSpec((None, POOL_W, D_MODEL), lambda i, j: (wl['l_in'], 0, 0)),
            pl.BlockSpec((None, nb, MEM_W, N_MEM), lambda i, j: (wl['l'], i, 0, 0)),
            pl.BlockSpec((None, nb, MEM_W, N_MEM), lambda i, j: (wl['l'], i, 0, 0)),
            pl.BlockSpec((1, MEM_W), const),
            pl.BlockSpec((MEM_W, MEM_W), const),
            pl.BlockSpec((None, D_MODEL, D_MODEL), lambda i, j: (wl['l'], 0, 0)),
        ],
        out_specs=[pl.BlockSpec((nb, ts, D_MODEL), lambda i, j: (i, j, 0)),
                   pl.BlockSpec((nb, POOL_STATE, POOL_W), lambda i, j: (i, 0, 0))],
        out_shape=[jax.ShapeDtypeStruct((b, s, D_MODEL), F32),
                   jax.ShapeDtypeStruct((b, POOL_STATE, POOL_W), F32)],
        scratch_shapes=[pltpu.VMEM((nb, POOL_PAD + ts, POOL_W), F32)],
        compiler_params=_params(2),
        name="mixer_a",
    )(x, wl['g_mix'], wl['w_in'], prev, wl['w_pool_out'],
      mk, mv, wl['g_mem_q'], wl['gsum'], wl['w_out'])


def _ffn_kernel(x_ref, g_ref, w_in_ref, w_out_ref, o_ref):
    x = x_ref[...]
    h = _rms(x, g_ref[...]).astype(BF16)
    acc = x
    for c0 in range(0, D_FF, FFN_CHUNK):
        gate = _dot(h, w_in_ref[:, c0:c0 + FFN_CHUNK])
        up = _dot(h, w_in_ref[:, D_FF + c0:D_FF + c0 + FFN_CHUNK])
        a = (gate * jax.nn.sigmoid(gate) * up).astype(BF16)
        acc = acc + _dot(a, w_out_ref[c0:c0 + FFN_CHUNK, :])
    o_ref[...] = acc


def _ffn(x, g, w_in, w_out, layer, *, tm):
    n = x.shape[0]
    assert n % tm == 0 and D_FF % FFN_CHUNK == 0 and FFN_CHUNK % LANE == 0
    const = lambda i: (0, 0)
    return pl.pallas_call(
        _ffn_kernel,
        grid=(n // tm,),
        in_specs=[
            pl.BlockSpec((tm, D_MODEL), lambda i: (i, 0)),
            pl.BlockSpec((1, D_MODEL), const),
            pl.BlockSpec((None, D_MODEL, 2 * D_FF), lambda i: (layer, 0, 0),
                         pipeline_mode=pl.Buffered(1)),
            pl.BlockSpec((None, D_FF, D_MODEL), lambda i: (layer, 0, 0),
                         pipeline_mode=pl.Buffered(1)),
        ],
        out_specs=pl.BlockSpec((tm, D_MODEL), lambda i: (i, 0)),
        out_shape=jax.ShapeDtypeStruct((n, D_MODEL), F32),
        compiler_params=_params(1),
        name="ffn",
    )(x, g, w_in, w_out)


def _kvrows_kernel(x_ref, g_ref, w_ref, gl_ref, cos_ref, sin_ref, *rest, with_kv):
    if with_kv:
        wk_ref, wv_ref, kv_ref, kn_ref, v_ref = rest
    else:
        (kv_ref,) = rest
    x = x_ref[...]
    h = _rms(x, g_ref[...]).astype(BF16)
    ck = _dot(h, w_ref[...])
    c = _rms(ck[:, :KV_LORA], gl_ref[...])
    kr = (ck[:, KV_LORA:KV_ROW] * cos_ref[...]
          + ck[:, KV_LORA + LANE:KV_LORA + LANE + QK_ROPE] * sin_ref[...])
    kv_ref[:, 0:KV_LORA] = c
    kv_ref[:, KV_LORA:KV_ROW] = kr
    if with_kv:
        cb = c.astype(BF16)
        kn = _dot(cb, wk_ref[0:KV_LORA, :]) + _dot(kr.astype(BF16), wk_ref[KV_LORA:KV_ROW, :])
        is_k = lax.broadcasted_iota(jnp.int32, (1, SLOT), 1) < QK_DIM
        for hd in range(MLA_HEADS):
            sl = slice(hd * SLOT, (hd + 1) * SLOT)
            kh = kn[:, sl]
            ssq = jnp.sum(jnp.where(is_k, kh * kh, 0.0), axis=-1, keepdims=True)
            inv = lax.rsqrt(ssq * (1.0 / QK_DIM) + EPS)
            kn_ref[:, sl] = (kh * inv).astype(BF16)
        vt = _dot_nt(wv_ref[...], cb)
        row = lax.broadcasted_iota(jnp.int32, (V_SLOTS, 1), 0)
        is_one = functools.reduce(
            jnp.logical_or, [row == hd * V_SLOT + V_HEAD for hd in range(MLA_HEADS)])
        v_ref[0] = jnp.where(is_one, 1.0, vt).astype(BF16)


def _kvrows(x, wk, cos32, sin32, *, tm, seq_tiles, with_kv):
    n = x.shape[0]
    const = lambda i: (0, 0)
    in_specs = [
        pl.BlockSpec((tm, D_MODEL), lambda i: (i, 0)),
        pl.BlockSpec((1, D_MODEL), const),
        pl.BlockSpec((D_MODEL, 4 * LANE), const),
        pl.BlockSpec((1, KV_LORA), const),
        pl.BlockSpec((tm, QK_ROPE), lambda i: (i % seq_tiles, 0)),
        pl.BlockSpec((tm, QK_ROPE), lambda i: (i % seq_tiles, 0)),
    ]
    args = [x, wk['norm_kv'], wk['w_dkv_ext'], wk['g_kv_lora'], cos32, sin32]
    out_specs = [pl.BlockSpec((tm, KV_ROW), lambda i: (i, 0))]
    out_shape = [jax.ShapeDtypeStruct((n, KV_ROW), F32)]
    if with_kv:
        in_specs += [pl.BlockSpec((KV_ROW, Q_SLOTS), const),
                     pl.BlockSpec((V_SLOTS, KV_LORA), const)]
        args += [wk['w_kslots'], wk['w_uvt_slots']]
        out_specs += [pl.BlockSpec((tm, Q_SLOTS), lambda i: (i, 0)),
                      pl.BlockSpec((1, V_SLOTS, tm), lambda i: (i // seq_tiles, 0, i % seq_tiles))]
        out_shape += [jax.ShapeDtypeStruct((n, Q_SLOTS), BF16),
                      jax.ShapeDtypeStruct((n // (tm * seq_tiles), V_SLOTS, tm * seq_tiles), BF16)]
    return pl.pallas_call(
        functools.partial(_kvrows_kernel, with_kv=with_kv),
        grid=(n // tm,),
        in_specs=in_specs, out_specs=out_specs, out_shape=out_shape,
        compiler_params=_params(1),
        name="kvrows",
    )(*args)


def _qproj_kernel(x_ref, g_ref, w_in_ref, gql_ref, wq_ref, gq_ref, trig_ref, *rest, absorb):
    if absorb:
        wabs_ref, qabs_ref, qm_ref = rest
    else:
        qf_ref, qm_ref = rest
    x = x_ref[...]
    h = _rms(x, g_ref[...]).astype(BF16)
    proj = _dot(h, w_in_ref[...])
    qm_ref[...] = proj[:, Q_LORA:]
    cq = _rms(proj[:, :Q_LORA], gql_ref[...]).astype(BF16)
    q_all = _dot(cq, wq_ref[...])
    tab = gq_ref[...] * trig_ref[...]
    is_q = lax.broadcasted_iota(jnp.int32, (1, SLOT), 1) < QK_DIM
    for hd in range(MLA_HEADS):
        sl = slice(hd * SLOT, (hd + 1) * SLOT)
        qh = q_all[:, sl]
        ssq = jnp.sum(jnp.where(is_q, qh * qh, 0.0), axis=-1, keepdims=True)
        inv = lax.rsqrt(ssq * (1.0 / QK_DIM) + EPS)
        qf = (qh * tab * (inv * (QK_DIM ** -0.5 * LOG2E))).astype(BF16)
        if absorb:
            qabs_ref[hd] = _dot(qf, wabs_ref[hd])
        else:
            qf_ref[:, sl] = qf


def _qproj(x, wl, wk, trig_t, *, tm, seq_tiles, absorb):
    n = x.shape[0]
    const = lambda i: (0, 0)
    in_specs = [
        pl.BlockSpec((tm, D_MODEL), lambda i: (i, 0)),
        pl.BlockSpec((1, D_MODEL), const),
        pl.BlockSpec((None, D_MODEL, Q_LORA + MEM_W), lambda i: (wl['l_in'], 0, 0)),
        pl.BlockSpec((1, Q_LORA), const),
        pl.BlockSpec((Q_LORA, Q_SLOTS), const),
        pl.BlockSpec((1, SLOT), const),
        pl.BlockSpec((tm, SLOT), lambda i: (i % seq_tiles, 0)),
    ]
    args = [x, wl['g_mix'], wl['w_in'], wl['g_q_lora'], wl['w_uq_slots'], wl['g_q_slot'],
            trig_t]
    if absorb:
        in_specs.append(pl.BlockSpec((MLA_HEADS, SLOT, KV_ROW), lambda i: (0, 0, 0)))
        args.append(wk['w_abs'])
        out_specs = [pl.BlockSpec((MLA_HEADS, tm, KV_ROW), lambda i: (0, i, 0))]
        out_shape = [jax.ShapeDtypeStruct((MLA_HEADS, n, KV_ROW), F32)]
    else:
        out_specs = [pl.BlockSpec((tm, Q_SLOTS), lambda i: (i, 0))]
        out_shape = [jax.ShapeDtypeStruct((n, Q_SLOTS), BF16)]
    out_specs.append(pl.BlockSpec((tm, MEM_W), lambda i: (i, 0)))
    out_shape.append(jax.ShapeDtypeStruct((n, MEM_W), F32))
    return pl.pallas_call(
        functools.partial(_qproj_kernel, absorb=absorb),
        grid=(n // tm,),
        in_specs=in_specs, out_specs=out_specs, out_shape=out_shape,
        compiler_params=_params(1),
        name="qproj",
    )(*args)


def _mla_prompt_kernel(q_ref, k_ref, vt_ref, o_ref, *, tq, nh):
    qi = pl.program_id(2)
    q = q_ref[0]
    q_idx = qi * tq + lax.broadcasted_iota(jnp.int32, (1, tq), 1)
    heads = range(nh)
    slots = [slice(hd * SLOT, (hd + 1) * SLOT) for hd in heads]

    def block(kb, carry, masked):
        start = pl.multiple_of(kb * tq, tq)
        k = k_ref[0, pl.ds(start, tq), :]
        vt = vt_ref[0, :, pl.ds(start, tq)]
        sts = [_dot_nt(k[:, sl], q[:, sl]) for sl in slots]
        if masked:
            k_idx = start + lax.broadcasted_iota(jnp.int32, (tq, 1), 0)
            sts = [jnp.where(k_idx <= q_idx, st, NEG) for st in sts]
        ms = [jnp.maximum(carry[2 * hd], jnp.max(sts[hd], axis=0, keepdims=True))
              for hd in heads]
        ps = [jnp.exp2(sts[hd] - ms[hd]).astype(BF16) for hd in heads]
        pvs = [_dot(vt[hd * V_SLOT:(hd + 1) * V_SLOT, :], ps[hd]) for hd in heads]
        new = []
        for hd in heads:
            alpha = jnp.exp2(carry[2 * hd] - ms[hd])
            new += [ms[hd], carry[2 * hd + 1] * alpha + pvs[hd]]
        return tuple(new)

    init = (jnp.full((1, tq), NEG, F32), jnp.zeros((V_SLOT, tq), F32)) * nh
    carry = lax.fori_loop(0, qi, lambda kb, c: block(kb, c, False), init)
    carry = block(qi, carry, True)
    outs = [carry[2 * hd + 1][:V_HEAD] / carry[2 * hd + 1][V_HEAD:V_HEAD + 1] for hd in heads]
    o_ref[0] = jnp.concatenate(outs, axis=0).T.astype(BF16)


def _mla_prompt(qf, kn, vt, *, tq, nh):
    b, s, _ = qf.shape
    assert s % tq == 0 and MLA_HEADS % nh == 0
    return pl.pallas_call(
        functools.partial(_mla_prompt_kernel, tq=tq, nh=nh),
        grid=(b, MLA_HEADS // nh, s // tq),
        in_specs=[
            pl.BlockSpec((1, tq, nh * SLOT), lambda i, hg, qi: (i, qi, hg)),
            pl.BlockSpec((1, s, nh * SLOT), lambda i, hg, qi: (i, 0, hg)),
            pl.BlockSpec((1, nh * V_SLOT, s), lambda i, hg, qi: (i, hg, 0)),
        ],
        out_specs=pl.BlockSpec((1, tq, nh * V_HEAD), lambda i, hg, qi: (i, qi, hg)),
        out_shape=jax.ShapeDtypeStruct((b, s, MLA_HEADS * V_HEAD), BF16),
        compiler_params=_params(3),
        name="mla_prompt",
    )(qf, kn, vt)


Q_ROWS = MLA_HEADS * 8
N_KN = MLA_HEADS * QK_NOPE
SAMPLE_SUB_PAGES = (8, 2)


def _mla_sample_kernel(pt_ref, qabs_ref, wukt_ref, kvnew_ref, cache_ref, *rest, pps, sub_pages,
                       n_tok, first_layer):
    if first_layer:
        o_ref, kinv_out_ref, m_ref, l_ref, acc_ref, lhs_ref, buf_ref, sem = rest
    else:
        kinv_in_ref, o_ref, m_ref, l_ref, acc_ref, lhs_ref, buf_ref, sem = rest
    seq = pl.program_id(0)
    step = pl.program_id(1)
    n_steps = pl.num_programs(1)
    flat = seq * n_steps + step
    total = pl.num_programs(0) * n_steps
    n_slots = buf_ref.shape[0]
    ahead = n_slots - 1
    slot = lax.rem(flat, n_slots)

    def page_copies(flat_i):
        seq_i, step_i, slot_i = flat_i // n_steps, lax.rem(flat_i, n_steps), lax.rem(flat_i, n_slots)
        return [pltpu.make_async_copy(cache_ref.at[pt_ref[seq_i, step_i * pps + i]],
                                      buf_ref.at[slot_i, i], sem.at[slot_i])
                for i in range(pps)]

    @pl.when(flat == 0)
    def _():
        for d in range(ahead):
            for cp in page_copies(flat + d):
                cp.start()

    @pl.when(flat + ahead < total)
    def _():
        for cp in page_copies(flat + ahead):
            cp.start()

    for cp in page_copies(flat):
        cp.wait()

    qabs = qabs_ref[...].reshape(Q_ROWS, KV_ROW)
    q_rope = qabs[:, KV_LORA:].astype(BF16)

    @pl.when(step == 0)
    def _():
        m_ref[...] = jnp.full(m_ref.shape, NEG, F32)
        l_ref[...] = jnp.zeros(l_ref.shape, F32)
        acc_ref[...] = jnp.zeros(acc_ref.shape, F32)
        lhs_ref[0:N_KN, :] = wukt_ref[...]
        lhs_ref[N_KN:, :] = qabs[:, :KV_LORA].astype(BF16)

    def scores(ct, krt, kinv, causal):
        t = ct.shape[1]
        if kinv is None:
            r1 = _dot(lhs_ref[...], ct)
            kn = r1[:N_KN]
            ss = jnp.sum((kn * kn).reshape(MLA_HEADS, QK_NOPE, t), axis=1)
            krsq = jnp.sum(krt * krt, axis=0, keepdims=True)
            kinv = lax.rsqrt((ss + krsq) * (1.0 / QK_DIM) + EPS)
            s_lat = r1[N_KN:]
        else:
            s_lat = _dot(lhs_ref[N_KN:, :], ct)
        sc = s_lat + _dot(q_rope, krt.astype(BF16))
        sc = sc.reshape(MLA_HEADS, n_tok, t) * kinv[:, None, :]
        if causal:
            tok = lax.broadcasted_iota(jnp.int32, (1, n_tok, t), 1)
            key = lax.broadcasted_iota(jnp.int32, (1, n_tok, t), 2)
            sc = jnp.where(key <= tok, sc, NEG)
        return sc.reshape(Q_ROWS, t), kinv

    def accumulate(state, sc, ct):
        m_prev, l_prev, acc = state
        m_new = jnp.maximum(m_prev, jnp.max(sc, axis=-1, keepdims=True))
        alpha = jnp.exp2(m_prev - m_new)
        p = jnp.exp2(sc - m_new)
        l_new = alpha * l_prev + jnp.sum(p, axis=-1, keepdims=True)
        return m_new, l_new, alpha * acc + _dot_nt(p.astype(BF16), ct)

    n_sub = pps // sub_pages
    t_sub = sub_pages * buf_ref.shape[3]

    def sub_scores(j):
        pages = range(j * sub_pages, (j + 1) * sub_pages)
        ct = jnp.concatenate([buf_ref[slot, i, 0:KV_LORA, :].astype(BF16) for i in pages], axis=1)
        krt = jnp.concatenate([buf_ref[slot, i, KV_LORA:, :] for i in pages], axis=1)
        if first_layer:
            sc, kinv = scores(ct, krt, None, False)
            kinv_out_ref[0, :, j * t_sub:(j + 1) * t_sub] = kinv
        else:
            sc, _ = scores(ct, krt, kinv_in_ref[0, :, j * t_sub:(j + 1) * t_sub], False)
        return sc, ct

    if first_layer:
        state = (m_ref[...], l_ref[...], acc_ref[...])
        pending = sub_scores(0)
        for j in range(1, n_sub):
            nxt = sub_scores(j)
            state = accumulate(state, *pending)
            pending = nxt
        m_ref[...], l_ref[...], acc_ref[...] = accumulate(state, *pending)
    else:
        subs = [sub_scores(j) for j in range(n_sub)]
        m_prev = m_ref[...]
        m_blk = functools.reduce(jnp.maximum, [sc for sc, _ in subs])
        m_new = jnp.maximum(m_prev, jnp.max(m_blk, axis=-1, keepdims=True))
        alpha = jnp.exp2(m_prev - m_new)
        ps = [jnp.exp2(sc - m_new) for sc, _ in subs]
        l_ref[...] = alpha * l_ref[...] + sum(jnp.sum(p, axis=-1, keepdims=True) for p in ps)
        pv = sum(_dot_nt(p.astype(BF16), ct) for p, (_, ct) in zip(ps, subs))
        acc_ref[...] = alpha * acc_ref[...] + pv
        m_ref[...] = m_new

    @pl.when(step == pl.num_programs(1) - 1)
    def _():
        kvt = kvnew_ref[0]
        ct = kvt[:KV_LORA].astype(BF16)
        sc, _ = scores(ct, kvt[KV_LORA:], None, True)
        _, l_fin, acc = accumulate((m_ref[...], l_ref[...], acc_ref[...]), sc, ct)
        o = acc / l_fin
        for hd in range(MLA_HEADS):
            o_ref[0, :, hd * KV_LORA:(hd + 1) * KV_LORA] = o[hd * n_tok:(hd + 1) * n_tok, :]


def _mla_sample(page_table, qabs, w_ukt, kv_new_t, cache_t, kinv, *, pps):
    n_seq, n_pages = page_table.shape
    n_tok = qabs.shape[1] // n_seq
    page = cache_t.shape[2]
    first_layer = kinv is None
    t_step = pps * page
    assert n_tok == 8 and n_pages % pps == 0

    in_specs = [
        pl.BlockSpec((MLA_HEADS, 1, n_tok, KV_ROW), lambda b, s, pt: (0, b, 0, 0)),
        pl.BlockSpec((N_KN, KV_LORA), lambda b, s, pt: (0, 0)),
        pl.BlockSpec((1, KV_ROW, LANE), lambda b, s, pt: (b, 0, 0)),
        pl.BlockSpec(memory_space=pl.ANY),
    ]
    args = [qabs.reshape(MLA_HEADS, n_seq, n_tok, KV_ROW), w_ukt, kv_new_t, cache_t]
    out_specs = [pl.BlockSpec((1, n_tok, MLA_HEADS * KV_LORA), lambda b, s, pt: (b, 0, 0))]
    out_shape = [jax.ShapeDtypeStruct((n_seq, n_tok, MLA_HEADS * KV_LORA), F32)]
    kinv_spec = pl.BlockSpec((1, MLA_HEADS, t_step), lambda b, s, pt: (b, 0, s))
    if first_layer:
        out_specs.append(kinv_spec)
        out_shape.append(jax.ShapeDtypeStruct((n_seq, MLA_HEADS, n_pages * page), F32))
    else:
        in_specs.append(kinv_spec)
        args.append(kinv)
    grid_spec = pltpu.PrefetchScalarGridSpec(
        num_scalar_prefetch=1,
        grid=(n_seq, n_pages // pps),
        in_specs=in_specs,
        out_specs=out_specs,
        scratch_shapes=[
            pltpu.VMEM((Q_ROWS, 1), F32),
            pltpu.VMEM((Q_ROWS, 1), F32),
            pltpu.VMEM((Q_ROWS, KV_LORA), F32),
            pltpu.VMEM((N_KN + Q_ROWS, KV_LORA), BF16),
            pltpu.VMEM((SAMPLE_PAGE_SLOTS, pps, KV_ROW, page), F32),
            pltpu.SemaphoreType.DMA((SAMPLE_PAGE_SLOTS,)),
        ],
    )
    return pl.pallas_call(
        functools.partial(_mla_sample_kernel, pps=pps,
                          sub_pages=SAMPLE_SUB_PAGES[0 if first_layer else 1], n_tok=n_tok,
                          first_layer=first_layer),
        grid_spec=grid_spec,
        out_shape=out_shape,
        compiler_params=_params(2),
        name="mla_sample",
    )(page_table, *args)


def _out_b_kernel(x_ref, tok_ref, qm_ref, mk_ref, mv_ref, gq_ref, gsum_ref, w_out_ref,
                  *rest, nb, ts, latent):
    if latent:
        wuv_ref, o_ref = rest
        tok = _dot(tok_ref[...].astype(BF16), wuv_ref[...]).astype(BF16)
    else:
        (o_ref,) = rest
        tok = tok_ref[...]
    mo = _mem_attend(qm_ref[...], mk_ref, mv_ref, gq_ref[...], gsum_ref[...], nb, ts)
    y = _dot(tok, w_out_ref[0:POOL_W, :]) + _dot(mo.astype(BF16), w_out_ref[POOL_W:, :])
    o_ref[...] = x_ref[...] + y


def _out_b(x, tok, qm, mk, mv, wl, wk, *, nb, ts, latent):
    n = x.shape[0]
    rows = nb * ts
    tok_w = tok.shape[1]
    const = lambda i: (0, 0)
    seq_len = n // mk.shape[1]
    mem_idx = lambda i: (wl['l'], i * rows // (seq_len * nb), 0, 0)
    in_specs = [
        pl.BlockSpec((rows, D_MODEL), lambda i: (i, 0)),
        pl.BlockSpec((rows, tok_w), lambda i: (i, 0)),
        pl.BlockSpec((rows, MEM_W), lambda i: (i, 0)),
        pl.BlockSpec((None, nb, MEM_W, N_MEM), mem_idx),
        pl.BlockSpec((None, nb, MEM_W, N_MEM), mem_idx),
        pl.BlockSpec((1, MEM_W), const),
        pl.BlockSpec((MEM_W, MEM_W), const),
        pl.BlockSpec((None, D_MODEL, D_MODEL), lambda i: (wl['l'], 0, 0)),
    ]
    args = [x, tok, qm, mk, mv, wl['g_mem_q'], wl['gsum'], wl['w_out']]
    if latent:
        in_specs.append(pl.BlockSpec((MLA_HEADS * KV_LORA, MLA_HEADS * V_HEAD), const))
        args.append(wk['w_uv_bd'])
    return pl.pallas_call(
        functools.partial(_out_b_kernel, nb=nb, ts=ts, latent=latent),
        grid=(n // rows,),
        in_specs=in_specs,
        out_specs=pl.BlockSpec((rows, D_MODEL), lambda i: (i, 0)),
        out_shape=jax.ShapeDtypeStruct((n, D_MODEL), F32),
        compiler_params=_params(1),
        name="out_b",
    )(*args)


def _rope_tables(pos):
    half = QK_ROPE // 2
    inv_freq = ROPE_BASE ** (-jnp.arange(half, dtype=F32) / half)
    ang = pos.astype(F32)[:, None] * inv_freq[None, :]
    cos32 = jnp.concatenate([jnp.cos(ang)] * 2, axis=-1)
    sin32 = jnp.concatenate([jnp.sin(ang)] * 2, axis=-1)
    n = pos.shape[0]
    trig_t = jnp.concatenate([jnp.ones((n, QK_NOPE), F32), cos32, sin32], axis=-1)
    return cos32, sin32, trig_t


def _rot_cols(w):
    half = QK_ROPE // 2
    return jnp.concatenate([-w[..., half:], w[..., :half]], axis=-1)


def _prep_weights(norm_mix, norm_ffn, w_out, w_ffn_in, w_ffn_out, g_mem_q, w_in_a, w_pool_grp,
                  pool_scale, w_in_b, g_q_lora, w_uq, g_q, norm_kv, w_dkv, g_kv_lora, w_uk, w_uv,
                  g_k_nope, g_k_rope):
    half = QK_ROPE // 2
    gsum = jnp.asarray(np.kron(np.eye(MEM_HEADS), np.ones((MEM_HEAD_DIM, MEM_HEAD_DIM))), BF16)
    w_out_b, w_ffn_in_b, w_ffn_out_b = (w.astype(BF16) for w in (w_out, w_ffn_in, w_ffn_out))
    w_in_a_b, w_in_b_b = w_in_a.astype(BF16), w_in_b.astype(BF16)
    w_pool_out = _pool_out_weights(w_pool_grp, pool_scale, w_out)
    layers = []
    for l in range(DEPTH):
        wl = dict(
            l=l, g_mix=norm_mix[l][None], g_ffn=norm_ffn[l][None],
            w_out=w_out_b, w_ffn_in=w_ffn_in_b, w_ffn_out=w_ffn_out_b,
            g_mem_q=jnp.tile(g_mem_q[l], MEM_HEADS)[None], gsum=gsum)
        if l < N_A_LAYERS:
            wl.update(w_in=w_in_a_b, l_in=l, w_pool_out=w_pool_out)
        else:
            j = l - N_A_LAYERS
            wq = w_uq[j].reshape(Q_LORA, MLA_HEADS, QK_DIM)
            w_slots = jnp.concatenate([wq, _rot_cols(wq[..., QK_NOPE:])],
                                      axis=-1).reshape(Q_LORA, Q_SLOTS)
            gk2 = jnp.concatenate([g_k_rope, g_k_rope])
            gq = g_q[j]
            g_rot = jnp.concatenate([gq[QK_NOPE + half:], gq[QK_NOPE:QK_NOPE + half]])
            g_slot = jnp.concatenate([gq[:QK_NOPE] * g_k_nope, gq[QK_NOPE:] * gk2, g_rot * gk2])
            wl.update(w_in=w_in_b_b, l_in=j, g_q_lora=g_q_lora[j][None],
                      w_uq_slots=w_slots.astype(BF16), g_q_slot=g_slot[None])
        layers.append(wl)

    zcol = jnp.zeros((D_MODEL, LANE - QK_ROPE), F32)
    w_dkv_ext = jnp.concatenate([w_dkv, zcol, _rot_cols(w_dkv[:, KV_LORA:]), zcol], axis=-1)
    wuk_h = w_uk.reshape(KV_LORA, MLA_HEADS, QK_NOPE)
    top = jnp.concatenate([wuk_h, jnp.zeros((KV_LORA, MLA_HEADS, SLOT - QK_NOPE), F32)], axis=-1)
    eye = jnp.eye(QK_ROPE, dtype=F32)
    bot = jnp.concatenate([jnp.zeros((QK_ROPE, QK_NOPE), F32), eye, eye], axis=-1)
    bot = jnp.broadcast_to(bot[:, None, :], (QK_ROPE, MLA_HEADS, SLOT))
    w_kslots = jnp.concatenate([top, bot], axis=0).reshape(KV_ROW, Q_SLOTS)
    w_abs = jnp.zeros((MLA_HEADS, SLOT, KV_ROW), F32)
    w_abs = w_abs.at[:, :QK_NOPE, :KV_LORA].set(jnp.transpose(wuk_h, (1, 2, 0)))
    eyes = jnp.broadcast_to(eye, (MLA_HEADS,) + eye.shape)
    w_abs = w_abs.at[:, QK_NOPE:QK_DIM, KV_LORA:].set(eyes).at[:, QK_DIM:, KV_LORA:].set(eyes)
    wuv_h = w_uv.reshape(KV_LORA, MLA_HEADS, V_HEAD)
    w_uvt_slots = jnp.concatenate(
        [jnp.transpose(wuv_h, (1, 2, 0)), jnp.zeros((MLA_HEADS, V_SLOT - V_HEAD, KV_LORA), F32)],
        axis=1).reshape(V_SLOTS, KV_LORA)
    w_uv_bd = jnp.einsum('chd,hg->hcgd', wuv_h, jnp.eye(MLA_HEADS, dtype=F32))
    w_uv_bd = w_uv_bd.reshape(MLA_HEADS * KV_LORA, MLA_HEADS * V_HEAD)
    shared = dict(
        norm_kv=norm_kv[None], g_kv_lora=g_kv_lora[None],
        w_dkv_ext=w_dkv_ext.astype(BF16), w_kslots=w_kslots.astype(BF16),
        w_uvt_slots=w_uvt_slots.astype(BF16), w_abs=w_abs.astype(BF16),
        w_ukt=jnp.transpose(w_uk).astype(BF16), w_uv_bd=w_uv_bd.astype(BF16))
    return layers, shared


def _trunk(x, pos0, pool_prev, mem_k, mem_v, layers, shared, *, nb, ts, tm, past):
    b, s, _ = x.shape
    n = b * s
    pos = pos0 + jnp.arange(s)
    tabs = _rope_tables(pos)
    if s >= tm:
        tmk, seq_tiles = tm, s // tm
    else:
        reps = min(tm, n) // s
        tmk, seq_tiles = reps * s, 1
        tabs = tuple(jnp.tile(t, (reps, 1)) for t in tabs)
    cos32, sin32, trig_t = tabs
    pool_new = []
    kv_new = None
    for l in range(DEPTH):
        wl = layers[l]
        if l < N_A_LAYERS:
            x, st = _mixer_a(x, pool_prev[l], mem_k, mem_v, wl, nb=nb, ts=ts, pos0=pos0)
            pool_new.append(st)
            xf = x.reshape(n, D_MODEL)
        else:
            xf = x.reshape(n, D_MODEL)
            if l == N_A_LAYERS:
                if past is None:
                    kv_new, kn, vt = _kvrows(xf, shared, cos32, sin32, tm=tmk,
                                             seq_tiles=seq_tiles, with_kv=True)
                else:
                    (kv_new,) = _kvrows(xf, shared, cos32, sin32, tm=tmk,
                                        seq_tiles=seq_tiles, with_kv=False)
            if past is None:
                qf, qm = _qproj(xf, wl, shared, trig_t, tm=tmk, seq_tiles=seq_tiles,
                                absorb=False)
                tok = _mla_prompt(qf.reshape(b, s, Q_SLOTS), kn.reshape(b, s, Q_SLOTS), vt,
                                  tq=ATTN_TILE, nh=ATTN_HEADS_PER_STEP)
                tok = tok.reshape(n, MLA_HEADS * V_HEAD)
                nb_o, ts_o = 1, tmk
            else:
                cache_t, page_table = past
                qabs, qm = _qproj(xf, wl, shared, trig_t, tm=tmk, seq_tiles=seq_tiles,
                                  absorb=True)
                if l == N_A_LAYERS:
                    kv_new_t = jnp.pad(jnp.swapaxes(kv_new.reshape(b, s, KV_ROW), 1, 2),
                                       ((0, 0), (0, 0), (0, LANE - s)))
                    tok, kinv = _mla_sample(page_table, qabs, shared['w_ukt'], kv_new_t, cache_t,
                                            None, pps=SAMPLE_PAGES_PER_STEP)
                else:
                    (tok,) = _mla_sample(page_table, qabs, shared['w_ukt'], kv_new_t, cache_t,
                                         kinv, pps=SAMPLE_PAGES_PER_STEP)
                tok = tok.reshape(n, MLA_HEADS * KV_LORA)
                nb_o, ts_o = nb, ts
            xf = _out_b(xf, tok, qm, mem_k, mem_v, wl, shared, nb=nb_o, ts=ts_o,
                        latent=past is not None)
        xf = _ffn(xf, wl['g_ffn'], wl['w_ffn_in'], wl['w_ffn_out'], wl['l'],
                  tm=min(FFN_ROWS, n))
        x = xf.reshape(b, s, D_MODEL)
    return x, kv_new.reshape(b, s, KV_ROW), jnp.stack(pool_new)


def kernel(x_prompt, x_sample, cache_kv, state_pool, cache_mem_k, cache_mem_v, page_table, mem_prompt,
           norm_mix, norm_ffn, w_out, w_ffn_in, w_ffn_out, norm_mem, w_mem_kv, g_mem_q, g_mem_k,
           w_in_a, w_pool_grp, pool_scale, w_in_b, g_q_lora, w_uq, g_q,
           norm_kv, w_dkv, g_kv_lora, w_uk, w_uv, g_k_nope, g_k_rope):
    layers, shared = _prep_weights(norm_mix, norm_ffn, w_out, w_ffn_in, w_ffn_out, g_mem_q, w_in_a,
                                   w_pool_grp, pool_scale, w_in_b, g_q_lora, w_uq, g_q, norm_kv,
                                   w_dkv, g_kv_lora, w_uk, w_uv, g_k_nope, g_k_rope)
    b_p, s_p, _ = x_prompt.shape
    n_seq, s_s, _ = x_sample.shape

    mk_p, mv_p = _memkv(mem_prompt, norm_mem[:, None, :],
                        jnp.swapaxes(w_mem_kv, 1, 2).astype(BF16), g_mem_k[:, None, :, None])
    pool0 = jnp.zeros((N_A_LAYERS, b_p, POOL_STATE, POOL_W), F32)
    y_p, kv_p, pool_p = _trunk(x_prompt, 0, pool0, mk_p, mv_p, layers, shared,
                               nb=1, ts=MIXER_SEQ_TILE, tm=TOKEN_ROWS, past=None)

    past_len = page_table.shape[1] * cache_kv.shape[1]
    mk_s = jnp.transpose(cache_mem_k, (0, 1, 3, 4, 2)).reshape(DEPTH, n_seq, MEM_W, N_MEM)
    mv_s = jnp.transpose(cache_mem_v, (0, 1, 3, 4, 2)).reshape(DEPTH, n_seq, MEM_W, N_MEM)
    cache_t = jnp.swapaxes(cache_kv, 1, 2)
    y_s, kv_s, pool_s = _trunk(x_sample, past_len, state_pool, mk_s, mv_s, layers, shared,
                               nb=SAMPLE_SEQS_PER_STEP, ts=s_s, tm=TOKEN_ROWS,
                               past=(cache_t, page_table))

    def mem_out(a):
        a = a.reshape(DEPTH, b_p, MEM_HEADS, MEM_HEAD_DIM, N_MEM)
        return jnp.transpose(a, (0, 1, 4, 2, 3))

    return (y_p, y_s, kv_p, kv_s, pool_p, pool_s, mem_out(mk_p), mem_out(mv_p))
```
